```python
import functools
import jax, jax.numpy as jnp
from jax import lax
import numpy as np

D_MODEL = 1024
BATCH = 4
SEQ = 4096
DEPTH = 4
DEC_BATCH = 32
DEC_SEQ = 8
PAST_LEN = 8192
PAGE_SIZE = 128

HEAD_DIM = 64
N_HEADS_ATTN = 8
N_KV_HEADS = 2
HEADS_PER_KV = N_HEADS_ATTN // N_KV_HEADS
D_ATTN = N_HEADS_ATTN * HEAD_DIM
D_KV = N_KV_HEADS * HEAD_DIM
N_BRANCH = 3
N_HEADS_RWKV = 8
D_RWKV = N_HEADS_RWKV * HEAD_DIM
D_MIX = D_ATTN + D_RWKV
ROT_DIM = HEAD_DIM // 4
ROPE_THETA = 500000.0
BLOCK_CMP = 32
STRIDE_CMP = 16
BLOCK_SLC = 64
N_SELECT = 16
N_LOCAL = 2
WINDOW = 512
QBLOCK = 128
RANK_W = 32
RANK_A = 32
RANK_G = 64
D_FF = 4 * D_MODEL
SHIFT_COLS = 3 * D_RWKV + RANK_W + RANK_A + RANK_G
IN_COLS = D_ATTN + 6 * D_KV + N_BRANCH * N_HEADS_ATTN + SHIFT_COLS
NORM_EPS = 1e-6
GN_EPS = 64e-5
NEG = -1e30
FORCED_SCORE = 1e9

kernel_name = 'nsa_rwkv7_hybrid_decode_step'


def _split(x, sizes):
    offs = [int(o) for o in np.cumsum(sizes)[:-1]]
    return jnp.split(x, offs, axis=-1)


def rms_norm(x, g):
    xf = x.astype(jnp.float32)
    y = xf * lax.rsqrt(jnp.mean(xf * xf, axis=-1, keepdims=True) + NORM_EPS)
    return (y * g.astype(jnp.float32)).astype(x.dtype)


def rope_tables(pos):
    inv = ROPE_THETA ** (-(jnp.arange(0, ROT_DIM, 2, dtype=jnp.float32) / ROT_DIM))
    ang = pos.astype(jnp.float32)[:, None] * inv[None, :]
    return jnp.cos(ang), jnp.sin(ang)


def apply_partial_rope(x, cos, sin):
    xr = x[..., :ROT_DIM].astype(jnp.float32)
    x1, x2 = xr[..., :ROT_DIM // 2], xr[..., ROT_DIM // 2:]
    c, s = cos[None, :, None, :], sin[None, :, None, :]
    rot = jnp.concatenate([x1 * c - x2 * s, x2 * c + x1 * s], axis=-1).astype(x.dtype)
    return jnp.concatenate([rot, x[..., ROT_DIM:]], axis=-1)


def masked_softmax(s, mask):
    p = jax.nn.softmax(jnp.where(mask, s, NEG), axis=-1)
    return jnp.where(mask, p, 0.0)


def gather_pages(pool, page_table):
    g = pool[page_table]
    return g.reshape(page_table.shape[0], -1, 2, N_KV_HEADS, HEAD_DIM)


def gather_blocks(blocks, idx):
    return jax.vmap(jax.vmap(lambda b, i: b[i]))(blocks, idx)


def nsa_heads(z_attn, pos, q_norm_g, k_norm_g):
    B, T = z_attn.shape[:2]
    parts = _split(z_attn, [D_ATTN] + [D_KV] * 6)
    cos, sin = rope_tables(pos)
    q = apply_partial_rope(rms_norm(parts[0].reshape(B, T, N_HEADS_ATTN, HEAD_DIM), q_norm_g), cos, sin)
    kvs = []
    for br in range(N_BRANCH):
        k = parts[1 + 2 * br].reshape(B, T, N_KV_HEADS, HEAD_DIM)
        v = parts[2 + 2 * br].reshape(B, T, N_KV_HEADS, HEAD_DIM)
        k = apply_partial_rope(rms_norm(k, k_norm_g[br]), cos, sin)
        kvs.append(jnp.stack([k, v], axis=2))
    return q, kvs


def compress_blocks(kv, cmp_pe, cmp_w):
    B, L = kv.shape[:2]
    n_cmp = (L - BLOCK_CMP) // STRIDE_CMP + 1
    r = BLOCK_CMP // STRIDE_CMP
    n_chunks = n_cmp + r - 1
    chunks = kv[:, :n_chunks * STRIDE_CMP].reshape(B, n_chunks, STRIDE_CMP, 2, N_KV_HEADS, HEAD_DIM)
    pe = cmp_pe.reshape(2, r, STRIDE_CMP, HEAD_DIM)
    w = cmp_w.reshape(2, r, STRIDE_CMP, HEAD_DIM, HEAD_DIM)
    comp = 0.0
    for i in range(r):
        part = chunks[:, i:i + n_cmp] + jnp.moveaxis(pe[:, i], 0, 1)[None, None, :, :, None, :]
        comp = comp + jnp.einsum('bnscgd,csde->bncge', part, w[:, i])
    c_end = jnp.arange(n_cmp, dtype=jnp.int32) * STRIDE_CMP + BLOCK_CMP - 1
    return comp[:, :, 0], comp[:, :, 1], c_end


def cmp_to_slc_map(n_cmp, n_slc):
    start = np.arange(n_cmp) * STRIDE_CMP
    first, last = start // BLOCK_SLC, (start + BLOCK_CMP - 1) // BLOCK_SLC
    j = np.arange(n_slc)
    return jnp.asarray(((first[:, None] <= j[None, :]) & (last[:, None] >= j[None, :])).astype(np.float32))


def to_slc_blocks(kv):
    B, L = kv.shape[:2]
    n_slc = -(-L // BLOCK_SLC)
    kv = jnp.pad(kv, ((0, 0), (0, n_slc * BLOCK_SLC - L), (0, 0), (0, 0), (0, 0)))
    return kv.reshape(B, n_slc, BLOCK_SLC, 2, N_KV_HEADS, HEAD_DIM).transpose(0, 4, 1, 2, 3, 5)


def nsa_query_block(q, q_pos, kc, vc, c_end, sel_map, slc_blocks, win_kv, w_pos, gates):
    B, Tq = q.shape[:2]
    f32 = jnp.float32
    qg = q.astype(f32).reshape(B, Tq, N_KV_HEADS, HEADS_PER_KV, HEAD_DIM) * (HEAD_DIM ** -0.5)
    s_c = jnp.einsum('bqghd,bngd->bghqn', qg, kc.astype(f32))
    p_c = masked_softmax(s_c, c_end[None, :] <= q_pos[:, None])
    o_c = jnp.einsum('bghqn,bngd->bqghd', p_c, vc.astype(f32))
    n_slc = slc_blocks.shape[2]
    imp = jnp.einsum('bghqn,ns->bgqs', p_c, sel_map)
    j = jnp.arange(n_slc, dtype=jnp.int32)[None, :]
    cur = (q_pos // BLOCK_SLC)[:, None]
    forced = (j == 0) | ((j <= cur) & (j > cur - N_LOCAL))
    causal = j * BLOCK_SLC <= q_pos[:, None]
    imp = jnp.where(causal, jnp.where(forced, FORCED_SCORE, imp), NEG)
    n_sel = min(N_SELECT, n_slc)
    _, idx = lax.top_k(imp, n_sel)
    g_kv = gather_blocks(slc_blocks, idx.reshape(B, N_KV_HEADS, Tq * n_sel))
    g_kv = g_kv.reshape(B, N_KV_HEADS, Tq, n_sel * BLOCK_SLC, 2, HEAD_DIM).astype(f32)
    k_pos = (idx[..., None] * BLOCK_SLC + jnp.arange(BLOCK_SLC, dtype=jnp.int32)).reshape(B, N_KV_HEADS, Tq, n_sel * BLOCK_SLC)
    s_s = jnp.einsum('bqghd,bgqkd->bghqk', qg, g_kv[..., 0, :])
    p_s = masked_softmax(s_s, (k_pos <= q_pos[:, None])[:, :, None])
    o_s = jnp.einsum('bghqk,bgqkd->bqghd', p_s, g_kv[..., 1, :])
    s_w = jnp.einsum('bqghd,bkgd->bghqk', qg, win_kv[:, :, 0].astype(f32))
    dist = q_pos[:, None] - w_pos[None, :]
    p_w = masked_softmax(s_w, (dist >= 0) & (dist < WINDOW) & (w_pos[None, :] >= 0))
    o_w = jnp.einsum('bghqk,bkgd->bqghd', p_w, win_kv[:, :, 1].astype(f32))
    g = gates.astype(f32).reshape(B, Tq, N_KV_HEADS, HEADS_PER_KV, N_BRANCH)
    o = g[..., 0:1] * o_c + g[..., 1:2] * o_s + g[..., 2:3] * o_w
    return o.reshape(B, Tq, D_ATTN).astype(q.dtype)


def prompt_attention(q, kvs, gates, lp):
    B, T = q.shape[:2]
    kc, vc, c_end = compress_blocks(kvs[0], lp['cmp_pe'], lp['cmp_w'])
    slc_blocks = to_slc_blocks(kvs[1])
    sel_map = cmp_to_slc_map(kc.shape[1], slc_blocks.shape[2])
    win_pad = jnp.pad(kvs[2], ((0, 0), (WINDOW, 0), (0, 0), (0, 0), (0, 0)))

    def one_query_block(q0):
        q_pos = q0 + jnp.arange(QBLOCK, dtype=jnp.int32)
        w_pos = q0 - WINDOW + jnp.arange(WINDOW + QBLOCK, dtype=jnp.int32)
        return nsa_query_block(
            lax.dynamic_slice_in_dim(q, q0, QBLOCK, axis=1), q_pos, kc, vc, c_end, sel_map, slc_blocks,
            lax.dynamic_slice_in_dim(win_pad, q0, WINDOW + QBLOCK, axis=1), w_pos,
            lax.dynamic_slice_in_dim(gates, q0, QBLOCK, axis=1))

    out = lax.map(one_query_block, jnp.arange(0, T, QBLOCK, dtype=jnp.int32))
    out = jnp.moveaxis(out, 0, 1).reshape(B, T, D_ATTN)
    cmp_pages = kvs[0].reshape(B, T // PAGE_SIZE, PAGE_SIZE, 2, N_KV_HEADS, HEAD_DIM)
    slc_pages = kvs[1].reshape(B, T // PAGE_SIZE, PAGE_SIZE, 2, N_KV_HEADS, HEAD_DIM)
    return out, (cmp_pages, slc_pages, kvs[2][:, -min(WINDOW, T):])


def sample_attention(q, kvs, gates, lp, cache_cmp, cache_slc, cache_win, page_table):
    Tn = q.shape[1]
    past = page_table.shape[1] * PAGE_SIZE
    w_buf = cache_win.shape[1]
    full_cmp = jnp.concatenate([gather_pages(cache_cmp, page_table), kvs[0]], axis=1)
    full_slc = jnp.concatenate([gather_pages(cache_slc, page_table), kvs[1]], axis=1)
    win = jnp.concatenate([cache_win, kvs[2]], axis=1)
    kc, vc, c_end = compress_blocks(full_cmp, lp['cmp_pe'], lp['cmp_w'])
    slc_blocks = to_slc_blocks(full_slc)
    sel_map = cmp_to_slc_map(kc.shape[1], slc_blocks.shape[2])
    q_pos = past + jnp.arange(Tn, dtype=jnp.int32)
    w_pos = past - w_buf + jnp.arange(w_buf + Tn, dtype=jnp.int32)
    out = nsa_query_block(q, q_pos, kc, vc, c_end, sel_map, slc_blocks, win, w_pos, gates)
    return out, (kvs[0], kvs[1], win[:, -w_buf:])


def rwkv7_time_mix(zs, shift_prev, wkv0, v_first, lp):
    f32 = jnp.float32
    B, T = zs.shape[:2]
    zf = zs.astype(f32)
    prev = jnp.concatenate([shift_prev.astype(f32)[:, None], zf[:, :-1]], axis=1)
    xs = zf + (prev - zf) * lp['mu']
    r, k, v, xw, xa, xg = _split(xs, [D_RWKV] * 3 + [RANK_W, RANK_A, RANK_G])
    w_raw = -jax.nn.softplus(-(lp['w0'] + jnp.tanh(xw) @ lp['w2'])) - 0.5
    decay = jnp.exp(-jnp.exp(w_raw))
    a = jax.nn.sigmoid(lp['a0'] + xa @ lp['a2'])
    if lp['vres'] is None:
        v_first = v
    else:
        vres0, vres2 = lp['vres']
        v = v + (v_first - v) * jax.nn.sigmoid(vres0 + xa @ vres2)
    g = jax.nn.sigmoid(xg) @ lp['g2']
    heads = lambda t: t.reshape(B, T, N_HEADS_RWKV, HEAD_DIM)
    kk = heads(k * lp['k_k'])
    kk = kk * lax.rsqrt(jnp.maximum(jnp.sum(kk * kk, axis=-1, keepdims=True), 1e-24))
    k = k * (1.0 + (a - 1.0) * lp['k_a'])
    r_h, k_h, v_h, a_h, d_h = heads(r), heads(k), heads(v), heads(a), heads(decay)

    def step(S, inp):
        r_t, d_t, kk_t, a_t, k_t, v_t = inp
        sa = jnp.einsum('bhvk,bhk->bhv', S, -kk_t)
        S = S * d_t[:, :, None, :] + sa[..., None] * (kk_t * a_t)[:, :, None, :] + v_t[..., None] * k_t[:, :, None, :]
        return S, jnp.einsum('bhvk,bhk->bhv', S, r_t)

    tm = lambda t: jnp.moveaxis(t, 1, 0)
    S_T, ys = lax.scan(step, wkv0.astype(f32), (tm(r_h), tm(d_h), tm(kk), tm(a_h), tm(k_h), tm(v_h)))
    y = jnp.moveaxis(ys, 0, 1)
    mean = jnp.mean(y, axis=-1, keepdims=True)
    var = jnp.mean(jnp.square(y - mean), axis=-1, keepdims=True)
    y = (y - mean) * lax.rsqrt(var + GN_EPS) * lp['lnx_w'].reshape(N_HEADS_RWKV, HEAD_DIM) + lp['lnx_b'].reshape(N_HEADS_RWKV, HEAD_DIM)
    y = y + jnp.sum(r_h * k_h * lp['r_k'], axis=-1, keepdims=True) * v_h
    y = y.reshape(B, T, D_RWKV) * g
    return y, S_T, zs[:, -1], v_first


def trunk_layer(x, pos, attn_fn, shift_prev, wkv0, v_first, lp):
    n = rms_norm(x, lp['norm1_g'])
    z = n @ lp['w_in']
    z_attn, z_gate, z_rwkv = _split(z, [D_ATTN + 6 * D_KV, N_BRANCH * N_HEADS_ATTN, SHIFT_COLS])
    q, kvs = nsa_heads(z_attn, pos, lp['q_norm_g'], lp['k_norm_g'])
    gates = jax.nn.sigmoid((z_gate + lp['gate_b']).astype(jnp.float32))
    o_attn, attn_state = attn_fn(q, kvs, gates, lp)
    o_rwkv, wkv_t, shift_t, v_first = rwkv7_time_mix(z_rwkv, shift_prev, wkv0, v_first, lp)
    mixed = jnp.concatenate([o_attn, o_rwkv.astype(o_attn.dtype)], axis=-1)
    h = x + (mixed @ lp['w_out']).astype(x.dtype)
    n2 = rms_norm(h, lp['norm2_g'])
    y = h + (jnp.square(jax.nn.relu(n2 @ lp['w_up'])) @ lp['w_down']).astype(x.dtype)
    return y, attn_state, wkv_t, shift_t, v_first


def setup_inputs(seed: int = 0) -> dict:
    key = jax.random.key(seed)
    ks = iter(jax.random.split(key, 40))
    f32 = jnp.float32
    nrm = lambda shape, s: jax.random.normal(next(ks), shape, f32) * s
    uni = lambda shape, lo, hi: jax.random.uniform(next(ks), shape, f32, lo, hi)
    n_pages = PAST_LEN // PAGE_SIZE
    used = DEC_BATCH * n_pages
    n_pool = used + max(1, used // 4)
    w_buf = min(WINDOW, PAST_LEN)
    pool_shape = (DEPTH, n_pool, PAGE_SIZE, 2, N_KV_HEADS, HEAD_DIM)
    return {
        'x_prompt': nrm((BATCH, SEQ, D_MODEL), 1.0),
        'x_sample': nrm((DEC_BATCH, DEC_SEQ, D_MODEL), 1.0),
        'cache_cmp_kv': nrm(pool_shape, 1.0),
        'cache_slc_kv': nrm(pool_shape, 1.0),
        'cache_win_kv': nrm((DEPTH, DEC_BATCH, w_buf, 2, N_KV_HEADS, HEAD_DIM), 1.0),
        'state_wkv': nrm((DEPTH, DEC_BATCH, N_HEADS_RWKV, HEAD_DIM, HEAD_DIM), 1.0),
        'state_shift': nrm((DEPTH, DEC_BATCH, SHIFT_COLS), 1.0),
        'page_table': jax.random.permutation(next(ks), n_pool)[:used].reshape(DEC_BATCH, n_pages).astype(jnp.int32),
        'norm1_g': 1.0 + nrm((DEPTH, D_MODEL), 0.02),
        'w_in': nrm((DEPTH, D_MODEL, IN_COLS), D_MODEL ** -0.5),
        'q_norm_g': 1.0 + nrm((DEPTH, HEAD_DIM), 0.02),
        'k_norm_g': 1.0 + nrm((DEPTH, N_BRANCH, HEAD_DIM), 0.02),
        'gate_b': nrm((DEPTH, N_BRANCH * N_HEADS_ATTN), 0.1),
        'cmp_pe': nrm((DEPTH, 2, BLOCK_CMP, HEAD_DIM), 0.1),
        'cmp_w': nrm((DEPTH, 2, BLOCK_CMP, HEAD_DIM, HEAD_DIM), (BLOCK_CMP * HEAD_DIM) ** -0.5),
        'shift_mu': uni((DEPTH, SHIFT_COLS), 0.0, 1.0),
        'w0': uni((DEPTH, D_RWKV), -5.0, 0.0),
        'w2': nrm((DEPTH, RANK_W, D_RWKV), 0.1 * RANK_W ** -0.5),
        'a0': nrm((DEPTH, D_RWKV), 0.1),
        'a2': nrm((DEPTH, RANK_A, D_RWKV), 0.5 * RANK_A ** -0.5),
        'vres0': nrm((DEPTH - 1, D_RWKV), 0.1),
        'vres2': nrm((DEPTH - 1, RANK_A, D_RWKV), 0.5 * RANK_A ** -0.5),
        'g2': nrm((DEPTH, RANK_G, D_RWKV), RANK_G ** -0.5),
        'k_k': 0.85 + nrm((DEPTH, D_RWKV), 0.05),
        'k_a': 1.0 + nrm((DEPTH, D_RWKV), 0.05),
        'r_k': nrm((DEPTH, N_HEADS_RWKV, HEAD_DIM), 0.1),
        'lnx_w': 1.0 + nrm((DEPTH, D_RWKV), 0.02),
        'lnx_b': nrm((DEPTH, D_RWKV), 0.02),
        'w_out': nrm((DEPTH, D_MIX, D_MODEL), D_MIX ** -0.5),
        'norm2_g': 1.0 + nrm((DEPTH, D_MODEL), 0.02),
        'w_up': nrm((DEPTH, D_MODEL, D_FF), D_MODEL ** -0.5),
        'w_down': nrm((DEPTH, D_FF, D_MODEL), D_FF ** -0.5),
    }


def reference(x_prompt, x_sample, cache_cmp_kv, cache_slc_kv, cache_win_kv, state_wkv, state_shift, page_table,
              norm1_g, w_in, q_norm_g, k_norm_g, gate_b, cmp_pe, cmp_w, shift_mu, w0, w2, a0, a2, vres0, vres2,
              g2, k_k, k_a, r_k, lnx_w, lnx_b, w_out, norm2_g, w_up, w_down):
    B, T = x_prompt.shape[:2]
    Tn = x_sample.shape[1]
    past = page_table.shape[1] * PAGE_SIZE
    pos_p = jnp.arange(T, dtype=jnp.int32)
    pos_s = past + jnp.arange(Tn, dtype=jnp.int32)
    shift_zero = jnp.zeros((B, SHIFT_COLS), x_prompt.dtype)
    wkv_zero = jnp.zeros((B, N_HEADS_RWKV, HEAD_DIM, HEAD_DIM), jnp.float32)
    xp, xs = x_prompt, x_sample
    vf_p, vf_s = None, None
    p_cmp, p_slc, p_win, p_wkv, p_shift = [], [], [], [], []
    s_cmp, s_slc, s_win, s_wkv, s_shift = [], [], [], [], []
    for l in range(DEPTH):
        lp = {'norm1_g': norm1_g[l], 'w_in': w_in[l], 'q_norm_g': q_norm_g[l], 'k_norm_g': k_norm_g[l],
              'gate_b': gate_b[l], 'cmp_pe': cmp_pe[l], 'cmp_w': cmp_w[l], 'mu': shift_mu[l], 'w0': w0[l],
              'w2': w2[l], 'a0': a0[l], 'a2': a2[l],
              'vres': None if l == 0 else (vres0[l - 1], vres2[l - 1]),
              'g2': g2[l], 'k_k': k_k[l], 'k_a': k_a[l], 'r_k': r_k[l], 'lnx_w': lnx_w[l], 'lnx_b': lnx_b[l],
              'w_out': w_out[l], 'norm2_g': norm2_g[l], 'w_up': w_up[l], 'w_down': w_down[l]}
        xp, (c_kv, sl_kv, w_kv), wkv_t, sh_t, vf_p = trunk_layer(
            xp, pos_p, prompt_attention, shift_zero, wkv_zero, vf_p, lp)
        p_cmp.append(c_kv); p_slc.append(sl_kv); p_win.append(w_kv); p_wkv.append(wkv_t); p_shift.append(sh_t)
        sample_fn = functools.partial(sample_attention, cache_cmp=cache_cmp_kv[l], cache_slc=cache_slc_kv[l],
                                      cache_win=cache_win_kv[l], page_table=page_table)
        xs, (c_kv, sl_kv, w_kv), wkv_t, sh_t, vf_s = trunk_layer(
            xs, pos_s, sample_fn, state_shift[l], state_wkv[l], vf_s, lp)
        s_cmp.append(c_kv); s_slc.append(sl_kv); s_win.append(w_kv); s_wkv.append(wkv_t); s_shift.append(sh_t)
    return (xp, xs, jnp.stack(p_cmp), jnp.stack(p_slc), jnp.stack(p_win), jnp.stack(p_wkv), jnp.stack(p_shift),
            jnp.stack(s_cmp), jnp.stack(s_slc), jnp.stack(s_win), jnp.stack(s_wkv), jnp.stack(s_shift))
```

```python
import functools
import jax, jax.numpy as jnp
from jax import lax
import numpy as np
from jax.experimental import pallas as pl
from jax.experimental.pallas import tpu as pltpu

D_MODEL = 1024
DEPTH = 4
PAGE_SIZE = 128
HEAD_DIM = 64
N_HEADS_ATTN = 8
N_KV_HEADS = 2
HEADS_PER_KV = N_HEADS_ATTN // N_KV_HEADS
D_ATTN = N_HEADS_ATTN * HEAD_DIM
D_KV = N_KV_HEADS * HEAD_DIM
N_BRANCH = 3
N_HEADS_RWKV = 8
D_RWKV = N_HEADS_RWKV * HEAD_DIM
ROT_DIM = HEAD_DIM // 4
ROPE_THETA = 500000.0
BLOCK_CMP = 32
STRIDE_CMP = 16
BLOCK_SLC = 64
N_SELECT = 16
N_LOCAL = 2
WINDOW = 512
QBLOCK = 128
RANK_W = 32
RANK_A = 32
RANK_G = 64
D_FF = 4 * D_MODEL
SHIFT_COLS = 3 * D_RWKV + RANK_W + RANK_A + RANK_G
IN_COLS = D_ATTN + 6 * D_KV + N_BRANCH * N_HEADS_ATTN + SHIFT_COLS
IN_COLS_PAD = 3072
NORM_EPS = 1e-6
GN_EPS = 64e-5
NEG = -1e30
FORCED_SCORE = 1e9

VMEM_LIMIT = 48 * 1024 * 1024
bf16 = jnp.bfloat16
f32 = jnp.float32


def _row_tile(m):
    return min(m, 1024)


def _norm_proj_kernel(x_ref, g_ref, w_ref, o_ref):
    x = x_ref[...]
    n = x * lax.rsqrt(jnp.mean(x * x, axis=-1, keepdims=True) + NORM_EPS) * g_ref[...]
    o_ref[...] = jnp.dot(n.astype(bf16), w_ref[...], preferred_element_type=f32)


def norm_proj(x, g, w):
    m, n = x.shape[0], w.shape[1]
    tm, tn = _row_tile(m), 512
    return pl.pallas_call(
        _norm_proj_kernel,
        grid=(m // tm, n // tn),
        in_specs=[pl.BlockSpec((tm, D_MODEL), lambda i, j: (i, 0)),
                  pl.BlockSpec((1, D_MODEL), lambda i, j: (0, 0)),
                  pl.BlockSpec((D_MODEL, tn), lambda i, j: (0, j))],
        out_specs=pl.BlockSpec((tm, tn), lambda i, j: (i, j)),
        out_shape=jax.ShapeDtypeStruct((m, n), f32),
        compiler_params=pltpu.CompilerParams(
            dimension_semantics=("parallel", "arbitrary"), vmem_limit_bytes=VMEM_LIMIT),
        name="norm_proj",
    )(x, g.reshape(1, D_MODEL), w)


def _out_proj_kernel(x_ref, m_ref, w_ref, o_ref):
    o_ref[...] = x_ref[...] + jnp.dot(m_ref[...].astype(bf16), w_ref[...], preferred_element_type=f32)


def out_proj(x, mixed, w):
    m = x.shape[0]
    tm = _row_tile(m)
    return pl.pallas_call(
        _out_proj_kernel,
        grid=(m // tm,),
        in_specs=[pl.BlockSpec((tm, D_MODEL), lambda i: (i, 0)),
                  pl.BlockSpec((tm, D_MODEL), lambda i: (i, 0)),
                  pl.BlockSpec((D_MODEL, D_MODEL), lambda i: (0, 0))],
        out_specs=pl.BlockSpec((tm, D_MODEL), lambda i: (i, 0)),
        out_shape=jax.ShapeDtypeStruct((m, D_MODEL), f32),
        compiler_params=pltpu.CompilerParams(
            dimension_semantics=("parallel",), vmem_limit_bytes=VMEM_LIMIT),
        name="out_proj",
    )(x, mixed, w)


def _ffn_kernel(h_ref, g_ref, wu_ref, wd_ref, o_ref, n_ref):
    j = pl.program_id(1)

    @pl.when(j == 0)
    def _():
        h = h_ref[...]
        n = h * lax.rsqrt(jnp.mean(h * h, axis=-1, keepdims=True) + NORM_EPS) * g_ref[...]
        n_ref[...] = n.astype(bf16)
        o_ref[...] = h

    u = jnp.dot(n_ref[...], wu_ref[...], preferred_element_type=f32)
    u = jnp.square(jnp.maximum(u, 0.0))
    o_ref[...] += jnp.dot(u.astype(bf16), wd_ref[...], preferred_element_type=f32)


def ffn(h, g, w_up, w_down):
    m = h.shape[0]
    tm, tf = _row_tile(m), 512
    return pl.pallas_call(
        _ffn_kernel,
        grid=(m // tm, D_FF // tf),
        in_specs=[pl.BlockSpec((tm, D_MODEL), lambda i, j: (i, 0)),
                  pl.BlockSpec((1, D_MODEL), lambda i, j: (0, 0)),
                  pl.BlockSpec((D_MODEL, tf), lambda i, j: (0, j)),
                  pl.BlockSpec((tf, D_MODEL), lambda i, j: (j, 0))],
        out_specs=pl.BlockSpec((tm, D_MODEL), lambda i, j: (i, 0)),
        out_shape=jax.ShapeDtypeStruct((m, D_MODEL), f32),
        scratch_shapes=[pltpu.VMEM((tm, D_MODEL), bf16)],
        compiler_params=pltpu.CompilerParams(
            dimension_semantics=("parallel", "arbitrary"), vmem_limit_bytes=VMEM_LIMIT),
        name="ffn",
    )(h, g.reshape(1, D_MODEL), w_up, w_down)


def _split(x, sizes):
    offs = [int(o) for o in np.cumsum(sizes)[:-1]]
    return jnp.split(x, offs, axis=-1)


def rms_norm(x, g):
    xf = x.astype(f32)
    y = xf * lax.rsqrt(jnp.mean(xf * xf, axis=-1, keepdims=True) + NORM_EPS)
    return (y * g.astype(f32)).astype(x.dtype)


def rope_tables(pos):
    inv = ROPE_THETA ** (-(jnp.arange(0, ROT_DIM, 2, dtype=f32) / ROT_DIM))
    ang = pos.astype(f32)[:, None] * inv[None, :]
    return jnp.cos(ang), jnp.sin(ang)


def apply_partial_rope(x, cos, sin):
    xr = x[..., :ROT_DIM].astype(f32)
    x1, x2 = xr[..., :ROT_DIM // 2], xr[..., ROT_DIM // 2:]
    c, s = cos[None, :, None, :], sin[None, :, None, :]
    rot = jnp.concatenate([x1 * c - x2 * s, x2 * c + x1 * s], axis=-1).astype(x.dtype)
    return jnp.concatenate([rot, x[..., ROT_DIM:]], axis=-1)


def masked_softmax(s, mask):
    p = jax.nn.softmax(jnp.where(mask, s, NEG), axis=-1)
    return jnp.where(mask, p, 0.0)


def gather_pages(pool, page_table):
    g = pool[page_table]
    return g.reshape(page_table.shape[0], -1, 2, N_KV_HEADS, HEAD_DIM)


def gather_blocks(blocks, idx):
    return jax.vmap(jax.vmap(lambda b, i: b[i]))(blocks, idx)


def nsa_heads(z_attn, pos, q_norm_g, k_norm_g):
    B, T = z_attn.shape[:2]
    parts = _split(z_attn, [D_ATTN] + [D_KV] * 6)
    cos, sin = rope_tables(pos)
    q = apply_partial_rope(rms_norm(parts[0].reshape(B, T, N_HEADS_ATTN, HEAD_DIM), q_norm_g), cos, sin)
    kvs = []
    for br in range(N_BRANCH):
        k = parts[1 + 2 * br].reshape(B, T, N_KV_HEADS, HEAD_DIM)
        v = parts[2 + 2 * br].reshape(B, T, N_KV_HEADS, HEAD_DIM)
        k = apply_partial_rope(rms_norm(k, k_norm_g[br]), cos, sin)
        kvs.append(jnp.stack([k, v], axis=2))
    return q, kvs


def compress_blocks(kv, cmp_pe, cmp_w):
    B, L = kv.shape[:2]
    n_cmp = (L - BLOCK_CMP) // STRIDE_CMP + 1
    r = BLOCK_CMP // STRIDE_CMP
    n_chunks = n_cmp + r - 1
    chunks = kv[:, :n_chunks * STRIDE_CMP].reshape(B, n_chunks, STRIDE_CMP, 2, N_KV_HEADS, HEAD_DIM)
    pe = cmp_pe.reshape(2, r, STRIDE_CMP, HEAD_DIM)
    w = cmp_w.reshape(2, r, STRIDE_CMP, HEAD_DIM, HEAD_DIM)
    comp = 0.0
    for i in range(r):
        part = chunks[:, i:i + n_cmp] + jnp.moveaxis(pe[:, i], 0, 1)[None, None, :, :, None, :]
        comp = comp + jnp.einsum('bnscgd,csde->bncge', part, w[:, i])
    c_end = jnp.arange(n_cmp, dtype=jnp.int32) * STRIDE_CMP + BLOCK_CMP - 1
    return comp[:, :, 0], comp[:, :, 1], c_end


def cmp_to_slc_map(n_cmp, n_slc):
    start = np.arange(n_cmp) * STRIDE_CMP
    first, last = start // BLOCK_SLC, (start + BLOCK_CMP - 1) // BLOCK_SLC
    j = np.arange(n_slc)
    return jnp.asarray(((first[:, None] <= j[None, :]) & (last[:, None] >= j[None, :])).astype(np.float32))


def to_slc_blocks(kv):
    B, L = kv.shape[:2]
    n_slc = -(-L // BLOCK_SLC)
    kv = jnp.pad(kv, ((0, 0), (0, n_slc * BLOCK_SLC - L), (0, 0), (0, 0), (0, 0)))
    return kv.reshape(B, n_slc, BLOCK_SLC, 2, N_KV_HEADS, HEAD_DIM).transpose(0, 4, 1, 2, 3, 5)


def nsa_query_block(q, q_pos, kc, vc, c_end, sel_map, slc_blocks, win_kv, w_pos, gates):
    B, Tq = q.shape[:2]
    qg = q.astype(f32).reshape(B, Tq, N_KV_HEADS, HEADS_PER_KV, HEAD_DIM) * (HEAD_DIM ** -0.5)
    s_c = jnp.einsum('bqghd,bngd->bghqn', qg, kc.astype(f32))
    p_c = masked_softmax(s_c, c_end[None, :] <= q_pos[:, None])
    o_c = jnp.einsum('bghqn,bngd->bqghd', p_c, vc.astype(f32))
    n_slc = slc_blocks.shape[2]
    imp = jnp.einsum('bghqn,ns->bgqs', p_c, sel_map)
    j = jnp.arange(n_slc, dtype=jnp.int32)[None, :]
    cur = (q_pos // BLOCK_SLC)[:, None]
    forced = (j == 0) | ((j <= cur) & (j > cur - N_LOCAL))
    causal = j * BLOCK_SLC <= q_pos[:, None]
    imp = jnp.where(causal, jnp.where(forced, FORCED_SCORE, imp), NEG)
    n_sel = min(N_SELECT, n_slc)
    _, idx = lax.top_k(imp, n_sel)
    g_kv = gather_blocks(slc_blocks, idx.reshape(B, N_KV_HEADS, Tq * n_sel))
    g_kv = g_kv.reshape(B, N_KV_HEADS, Tq, n_sel * BLOCK_SLC, 2, HEAD_DIM).astype(f32)
    k_pos = (idx[..., None] * BLOCK_SLC + jnp.arange(BLOCK_SLC, dtype=jnp.int32)).reshape(B, N_KV_HEADS, Tq, n_sel * BLOCK_SLC)
    s_s = jnp.einsum('bqghd,bgqkd->bghqk', qg, g_kv[..., 0, :])
    p_s = masked_softmax(s_s, (k_pos <= q_pos[:, None])[:, :, None])
    o_s = jnp.einsum('bghqk,bgqkd->bqghd', p_s, g_kv[..., 1, :])
    s_w = jnp.einsum('bqghd,bkgd->bghqk', qg, win_kv[:, :, 0].astype(f32))
    dist = q_pos[:, None] - w_pos[None, :]
    p_w = masked_softmax(s_w, (dist >= 0) & (dist < WINDOW) & (w_pos[None, :] >= 0))
    o_w = jnp.einsum('bghqk,bkgd->bqghd', p_w, win_kv[:, :, 1].astype(f32))
    g = gates.astype(f32).reshape(B, Tq, N_KV_HEADS, HEADS_PER_KV, N_BRANCH)
    o = g[..., 0:1] * o_c + g[..., 1:2] * o_s + g[..., 2:3] * o_w
    return o.reshape(B, Tq, D_ATTN).astype(q.dtype)


def prompt_attention(q, kvs, gates, lp):
    B, T = q.shape[:2]
    kc, vc, c_end = compress_blocks(kvs[0], lp['cmp_pe'], lp['cmp_w'])
    slc_blocks = to_slc_blocks(kvs[1])
    sel_map = cmp_to_slc_map(kc.shape[1], slc_blocks.shape[2])
    win_pad = jnp.pad(kvs[2], ((0, 0), (WINDOW, 0), (0, 0), (0, 0), (0, 0)))

    def one_query_block(q0):
        q_pos = q0 + jnp.arange(QBLOCK, dtype=jnp.int32)
        w_pos = q0 - WINDOW + jnp.arange(WINDOW + QBLOCK, dtype=jnp.int32)
        return nsa_query_block(
            lax.dynamic_slice_in_dim(q, q0, QBLOCK, axis=1), q_pos, kc, vc, c_end, sel_map, slc_blocks,
            lax.dynamic_slice_in_dim(win_pad, q0, WINDOW + QBLOCK, axis=1), w_pos,
            lax.dynamic_slice_in_dim(gates, q0, QBLOCK, axis=1))

    out = lax.map(one_query_block, jnp.arange(0, T, QBLOCK, dtype=jnp.int32))
    out = jnp.moveaxis(out, 0, 1).reshape(B, T, D_ATTN)
    cmp_pages = kvs[0].reshape(B, T // PAGE_SIZE, PAGE_SIZE, 2, N_KV_HEADS, HEAD_DIM)
    slc_pages = kvs[1].reshape(B, T // PAGE_SIZE, PAGE_SIZE, 2, N_KV_HEADS, HEAD_DIM)
    return out, (cmp_pages, slc_pages, kvs[2][:, -min(WINDOW, T):])


def sample_attention(q, kvs, gates, lp, cache_cmp, cache_slc, cache_win, page_table):
    Tn = q.shape[1]
    past = page_table.shape[1] * PAGE_SIZE
    w_buf = cache_win.shape[1]
    full_cmp = jnp.concatenate([gather_pages(cache_cmp, page_table), kvs[0]], axis=1)
    full_slc = jnp.concatenate([gather_pages(cache_slc, page_table), kvs[1]], axis=1)
    win = jnp.concatenate([cache_win, kvs[2]], axis=1)
    kc, vc, c_end = compress_blocks(full_cmp, lp['cmp_pe'], lp['cmp_w'])
    slc_blocks = to_slc_blocks(full_slc)
    sel_map = cmp_to_slc_map(kc.shape[1], slc_blocks.shape[2])
    q_pos = past + jnp.arange(Tn, dtype=jnp.int32)
    w_pos = past - w_buf + jnp.arange(w_buf + Tn, dtype=jnp.int32)
    out = nsa_query_block(q, q_pos, kc, vc, c_end, sel_map, slc_blocks, win, w_pos, gates)
    return out, (kvs[0], kvs[1], win[:, -w_buf:])


def rwkv7_time_mix(zs, shift_prev, wkv0, v_first, lp):
    B, T = zs.shape[:2]
    zf = zs.astype(f32)
    prev = jnp.concatenate([shift_prev.astype(f32)[:, None], zf[:, :-1]], axis=1)
    xs = zf + (prev - zf) * lp['mu']
    r, k, v, xw, xa, xg = _split(xs, [D_RWKV] * 3 + [RANK_W, RANK_A, RANK_G])
    w_raw = -jax.nn.softplus(-(lp['w0'] + jnp.tanh(xw) @ lp['w2'])) - 0.5
    decay = jnp.exp(-jnp.exp(w_raw))
    a = jax.nn.sigmoid(lp['a0'] + xa @ lp['a2'])
    if lp['vres'] is None:
        v_first = v
    else:
        vres0, vres2 = lp['vres']
        v = v + (v_first - v) * jax.nn.sigmoid(vres0 + xa @ vres2)
    g = jax.nn.sigmoid(xg) @ lp['g2']
    heads = lambda t: t.reshape(B, T, N_HEADS_RWKV, HEAD_DIM)
    kk = heads(k * lp['k_k'])
    kk = kk * lax.rsqrt(jnp.maximum(jnp.sum(kk * kk, axis=-1, keepdims=True), 1e-24))
    k = k * (1.0 + (a - 1.0) * lp['k_a'])
    r_h, k_h, v_h, a_h, d_h = heads(r), heads(k), heads(v), heads(a), heads(decay)

    def step(S, inp):
        r_t, d_t, kk_t, a_t, k_t, v_t = inp
        sa = jnp.einsum('bhvk,bhk->bhv', S, -kk_t)
        S = S * d_t[:, :, None, :] + sa[..., None] * (kk_t * a_t)[:, :, None, :] + v_t[..., None] * k_t[:, :, None, :]
        return S, jnp.einsum('bhvk,bhk->bhv', S, r_t)

    tm = lambda t: jnp.moveaxis(t, 1, 0)
    S_T, ys = lax.scan(step, wkv0.astype(f32), (tm(r_h), tm(d_h), tm(kk), tm(a_h), tm(k_h), tm(v_h)))
    y = jnp.moveaxis(ys, 0, 1)
    mean = jnp.mean(y, axis=-1, keepdims=True)
    var = jnp.mean(jnp.square(y - mean), axis=-1, keepdims=True)
    y = (y - mean) * lax.rsqrt(var + GN_EPS) * lp['lnx_w'].reshape(N_HEADS_RWKV, HEAD_DIM) + lp['lnx_b'].reshape(N_HEADS_RWKV, HEAD_DIM)
    y = y + jnp.sum(r_h * k_h * lp['r_k'], axis=-1, keepdims=True) * v_h
    y = y.reshape(B, T, D_RWKV) * g
    return y, S_T, zs[:, -1], v_first


def trunk_layer(x, pos, attn_fn, shift_prev, wkv0, v_first, lp):
    B, T = x.shape[:2]
    x2 = x.reshape(B * T, D_MODEL)
    z = norm_proj(x2, lp['norm1_g'], lp['w_in'])[:, :IN_COLS].reshape(B, T, IN_COLS)
    z_attn, z_gate, z_rwkv = _split(z, [D_ATTN + 6 * D_KV, N_BRANCH * N_HEADS_ATTN, SHIFT_COLS])
    q, kvs = nsa_heads(z_attn, pos, lp['q_norm_g'], lp['k_norm_g'])
    gates = jax.nn.sigmoid((z_gate + lp['gate_b']).astype(f32))
    o_attn, attn_state = attn_fn(q, kvs, gates, lp)
    o_rwkv, wkv_t, shift_t, v_first = rwkv7_time_mix(z_rwkv, shift_prev, wkv0, v_first, lp)
    mixed = jnp.concatenate([o_attn, o_rwkv.astype(o_attn.dtype)], axis=-1).reshape(B * T, D_MODEL)
    h = out_proj(x2, mixed, lp['w_out'])
    y = ffn(h, lp['norm2_g'], lp['w_up'], lp['w_down']).reshape(B, T, D_MODEL)
    return y, attn_state, wkv_t, shift_t, v_first


def kernel(x_prompt, x_sample, cache_cmp_kv, cache_slc_kv, cache_win_kv, state_wkv, state_shift, page_table,
           norm1_g, w_in, q_norm_g, k_norm_g, gate_b, cmp_pe, cmp_w, shift_mu, w0, w2, a0, a2, vres0, vres2,
           g2, k_k, k_a, r_k, lnx_w, lnx_b, w_out, norm2_g, w_up, w_down):
    B, T = x_prompt.shape[:2]
    Tn = x_sample.shape[1]
    past = page_table.shape[1] * PAGE_SIZE
    pos_p = jnp.arange(T, dtype=jnp.int32)
    pos_s = past + jnp.arange(Tn, dtype=jnp.int32)
    shift_zero = jnp.zeros((B, SHIFT_COLS), x_prompt.dtype)
    wkv_zero = jnp.zeros((B, N_HEADS_RWKV, HEAD_DIM, HEAD_DIM), f32)
    w_in_b = jnp.pad(w_in, ((0, 0), (0, 0), (0, IN_COLS_PAD - IN_COLS))).astype(bf16)
    w_out_b, w_up_b, w_down_b = w_out.astype(bf16), w_up.astype(bf16), w_down.astype(bf16)
    xp, xs = x_prompt, x_sample
    vf_p, vf_s = None, None
    p_cmp, p_slc, p_win, p_wkv, p_shift = [], [], [], [], []
    s_cmp, s_slc, s_win, s_wkv, s_shift = [], [], [], [], []
    for l in range(DEPTH):
        lp = {'norm1_g': norm1_g[l], 'w_in': w_in_b[l], 'q_norm_g': q_norm_g[l], 'k_norm_g': k_norm_g[l],
              'gate_b': gate_b[l], 'cmp_pe': cmp_pe[l], 'cmp_w': cmp_w[l], 'mu': shift_mu[l], 'w0': w0[l],
              'w2': w2[l], 'a0': a0[l], 'a2': a2[l],
              'vres': None if l == 0 else (vres0[l - 1], vres2[l - 1]),
              'g2': g2[l], 'k_k': k_k[l], 'k_a': k_a[l], 'r_k': r_k[l], 'lnx_w': lnx_w[l], 'lnx_b': lnx_b[l],
              'w_out': w_out_b[l], 'norm2_g': norm2_g[l], 'w_up': w_up_b[l], 'w_down': w_down_b[l]}
        xp, (c_kv, sl_kv, w_kv), wkv_t, sh_t, vf_p = trunk_layer(
            xp, pos_p, prompt_attention, shift_zero, wkv_zero, vf_p, lp)
        p_cmp.append(c_kv); p_slc.append(sl_kv); p_win.append(w_kv); p_wkv.append(wkv_t); p_shift.append(sh_t)
        sample_fn = functools.partial(sample_attention, cache_cmp=cache_cmp_kv[l], cache_slc=cache_slc_kv[l],
                                      cache_win=cache_win_kv[l], page_table=page_table)
        xs, (c_kv, sl_kv, w_kv), wkv_t, sh_t, vf_s = trunk_layer(
            xs, pos_s, sample_fn, state_shift[l], state_wkv[l], vf_s, lp)
        s_cmp.append(c_kv); s_slc.append(sl_kv); s_win.append(w_kv); s_wkv.append(wkv_t); s_shift.append(sh_t)
    return (xp, xs, jnp.stack(p_cmp), jnp.stack(p_slc), jnp.stack(p_win), jnp.stack(p_wkv), jnp.stack(p_shift),
            jnp.stack(s_cmp), jnp.stack(s_slc), jnp.stack(s_win), jnp.stack(s_wkv), jnp.stack(s_shift))
```

```python
import functools
import jax, jax.numpy as jnp
from jax import lax
import numpy as np
from jax.experimental import pallas as pl
from jax.experimental.pallas import tpu as pltpu

D_MODEL = 1024
DEPTH = 4
PAGE_SIZE = 128
HEAD_DIM = 64
N_HEADS_ATTN = 8
N_KV_HEADS = 2
HEADS_PER_KV = N_HEADS_ATTN // N_KV_HEADS
D_ATTN = N_HEADS_ATTN * HEAD_DIM
D_KV = N_KV_HEADS * HEAD_DIM
N_BRANCH = 3
N_HEADS_RWKV = 8
D_RWKV = N_HEADS_RWKV * HEAD_DIM
ROT_DIM = HEAD_DIM // 4
ROPE_THETA = 500000.0
BLOCK_CMP = 32
STRIDE_CMP = 16
BLOCK_SLC = 64
N_SELECT = 16
N_LOCAL = 2
WINDOW = 512
QBLOCK = 128
RANK_W = 32
RANK_A = 32
RANK_G = 64
D_FF = 4 * D_MODEL
SHIFT_COLS = 3 * D_RWKV + RANK_W + RANK_A + RANK_G
IN_COLS = D_ATTN + 6 * D_KV + N_BRANCH * N_HEADS_ATTN + SHIFT_COLS
IN_COLS_PAD = 3072
NORM_EPS = 1e-6
GN_EPS = 64e-5
NEG = -1e30
FORCED_SCORE = 1e9

VMEM_LIMIT = 48 * 1024 * 1024
bf16 = jnp.bfloat16
f32 = jnp.float32


def _row_tile(m):
    return min(m, 1024)


def _norm_proj_kernel(x_ref, g_ref, w_ref, o_ref):
    x = x_ref[...]
    n = x * lax.rsqrt(jnp.mean(x * x, axis=-1, keepdims=True) + NORM_EPS) * g_ref[...]
    o_ref[...] = jnp.dot(n.astype(bf16), w_ref[...], preferred_element_type=f32)


def norm_proj(x, g, w):
    m, n = x.shape[0], w.shape[1]
    tm, tn = _row_tile(m), 512
    return pl.pallas_call(
        _norm_proj_kernel,
        grid=(m // tm, n // tn),
        in_specs=[pl.BlockSpec((tm, D_MODEL), lambda i, j: (i, 0)),
                  pl.BlockSpec((1, D_MODEL), lambda i, j: (0, 0)),
                  pl.BlockSpec((D_MODEL, tn), lambda i, j: (0, j))],
        out_specs=pl.BlockSpec((tm, tn), lambda i, j: (i, j)),
        out_shape=jax.ShapeDtypeStruct((m, n), f32),
        compiler_params=pltpu.CompilerParams(
            dimension_semantics=("parallel", "arbitrary"), vmem_limit_bytes=VMEM_LIMIT),
        name="norm_proj",
    )(x, g.reshape(1, D_MODEL), w)


def _out_proj_kernel(x_ref, m_ref, w_ref, o_ref):
    o_ref[...] = x_ref[...] + jnp.dot(m_ref[...].astype(bf16), w_ref[...], preferred_element_type=f32)


def out_proj(x, mixed, w):
    m = x.shape[0]
    tm = _row_tile(m)
    return pl.pallas_call(
        _out_proj_kernel,
        grid=(m // tm,),
        in_specs=[pl.BlockSpec((tm, D_MODEL), lambda i: (i, 0)),
                  pl.BlockSpec((tm, D_MODEL), lambda i: (i, 0)),
                  pl.BlockSpec((D_MODEL, D_MODEL), lambda i: (0, 0))],
        out_specs=pl.BlockSpec((tm, D_MODEL), lambda i: (i, 0)),
        out_shape=jax.ShapeDtypeStruct((m, D_MODEL), f32),
        compiler_params=pltpu.CompilerParams(
            dimension_semantics=("parallel",), vmem_limit_bytes=VMEM_LIMIT),
        name="out_proj",
    )(x, mixed, w)


def _ffn_kernel(h_ref, g_ref, wu_ref, wd_ref, o_ref, n_ref):
    j = pl.program_id(1)

    @pl.when(j == 0)
    def _():
        h = h_ref[...]
        n = h * lax.rsqrt(jnp.mean(h * h, axis=-1, keepdims=True) + NORM_EPS) * g_ref[...]
        n_ref[...] = n.astype(bf16)
        o_ref[...] = h

    u = jnp.dot(n_ref[...], wu_ref[...], preferred_element_type=f32)
    u = jnp.square(jnp.maximum(u, 0.0))
    o_ref[...] += jnp.dot(u.astype(bf16), wd_ref[...], preferred_element_type=f32)


def ffn(h, g, w_up, w_down):
    m = h.shape[0]
    tm, tf = _row_tile(m), 512
    return pl.pallas_call(
        _ffn_kernel,
        grid=(m // tm, D_FF // tf),
        in_specs=[pl.BlockSpec((tm, D_MODEL), lambda i, j: (i, 0)),
                  pl.BlockSpec((1, D_MODEL), lambda i, j: (0, 0)),
                  pl.BlockSpec((D_MODEL, tf), lambda i, j: (0, j)),
                  pl.BlockSpec((tf, D_MODEL), lambda i, j: (j, 0))],
        out_specs=pl.BlockSpec((tm, D_MODEL), lambda i, j: (i, 0)),
        out_shape=jax.ShapeDtypeStruct((m, D_MODEL), f32),
        scratch_shapes=[pltpu.VMEM((tm, D_MODEL), bf16)],
        compiler_params=pltpu.CompilerParams(
            dimension_semantics=("parallel", "arbitrary"), vmem_limit_bytes=VMEM_LIMIT),
        name="ffn",
    )(h, g.reshape(1, D_MODEL), w_up, w_down)


Q_COLS = D_ATTN
KV_COLS = 2 * D_KV
GATE_COL0 = D_ATTN + 6 * D_KV
LANES = 128


def _seg_sumsq(x, p_ref):
    x2 = x * x
    hi = x2.astype(bf16)
    lo = (x2 - hi.astype(f32)).astype(bf16)
    return (jnp.dot(hi, p_ref[...], preferred_element_type=f32)
            + jnp.dot(lo, p_ref[...], preferred_element_type=f32))


def _norm_rope(x, g, p_ref, c, s1, s2):
    y = x * lax.rsqrt(_seg_sumsq(x, p_ref) * (1.0 / HEAD_DIM) + NORM_EPS) * g
    return y * c + pltpu.roll(y, LANES - ROT_DIM // 2, 1) * s1 + pltpu.roll(y, ROT_DIM // 2, 1) * s2


def _attn_prep_kernel(zq_ref, zc_ref, zs_ref, zw_ref, zg_ref, c_ref, s1_ref, s2_ref, p_ref, qg_ref, kg_ref, gb_ref,
                      q_ref, kvc_ref, kvs_ref, kvw_ref, kb_ref, vvb_ref, gate_ref):
    c, s1, s2 = c_ref[...], s1_ref[...], s2_ref[...]
    lane = lax.broadcasted_iota(jnp.int32, (zq_ref.shape[0], LANES), 1)
    for pair in range(N_HEADS_ATTN // 2):
        y = _norm_rope(zq_ref[:, pair * LANES:(pair + 1) * LANES], qg_ref[...], p_ref, c, s1, s2)
        y = y * (HEAD_DIM ** -0.5)
        y_sw = pltpu.roll(y, HEAD_DIM, 1)
        for half in range(2):
            h = 2 * pair + half
            grp = h // HEADS_PER_KV
            src = y if half == grp else y_sw
            keep = (lane >= HEAD_DIM) if grp == 1 else (lane < HEAD_DIM)
            q_ref[0, h] = jnp.where(keep, src, 0.0).astype(bf16)
    ins = (zc_ref, zs_ref, zw_ref)
    outs = (kvc_ref, kvs_ref, kvw_ref)
    for br in range(N_BRANCH):
        k = _norm_rope(ins[br][:, :D_KV], kg_ref[br:br + 1, :], p_ref, c, s1, s2)
        v = ins[br][:, D_KV:]
        outs[br][:, :D_KV] = k
        outs[br][:, D_KV:] = v
        if br > 0:
            kb_ref[br - 1] = k.astype(bf16)
            vvb_ref[br - 1, :, :D_KV] = v.astype(bf16)
            vvb_ref[br - 1, :, D_KV:] = pltpu.roll(v, HEAD_DIM, 1).astype(bf16)
    gate_ref[...] = jax.nn.sigmoid(zg_ref[...] + gb_ref[...])


def rope_lane_tables(pos):
    half = ROT_DIM // 2
    inv = ROPE_THETA ** (-(jnp.arange(0, ROT_DIM, 2, dtype=f32) / ROT_DIM))
    ang = pos.astype(f32)[:, None] * inv[None, :]
    cos, sin = jnp.cos(ang), jnp.sin(ang)
    t = pos.shape[0]
    one, zero = jnp.ones((t, HEAD_DIM - ROT_DIM), f32), jnp.zeros((t, HEAD_DIM - ROT_DIM), f32)
    zh = jnp.zeros((t, half), f32)
    c = jnp.concatenate([cos, cos, one], axis=1)
    s1 = jnp.concatenate([-sin, zh, zero], axis=1)
    s2 = jnp.concatenate([zh, sin, zero], axis=1)
    return tuple(jnp.tile(a, (1, 2)) for a in (c, s1, s2))


def attn_prep(z, rope_tabs, seg_ones, q_g, k_g, gate_b, batch, seq):
    m = batch * seq
    tm = 512
    tiles = seq // tm
    c, s1, s2 = rope_tabs
    row = lambda i: (i, 0)
    tab = lambda i: (i % tiles, 0)
    const = lambda i: (0, 0)
    kv_shape = jax.ShapeDtypeStruct((m, KV_COLS), f32)
    return pl.pallas_call(
        _attn_prep_kernel,
        grid=(m // tm,),
        in_specs=[pl.BlockSpec((tm, Q_COLS), row),
                  pl.BlockSpec((tm, KV_COLS), lambda i: (i, Q_COLS // KV_COLS)),
                  pl.BlockSpec((tm, KV_COLS), lambda i: (i, Q_COLS // KV_COLS + 1)),
                  pl.BlockSpec((tm, KV_COLS), lambda i: (i, Q_COLS // KV_COLS + 2)),
                  pl.BlockSpec((tm, LANES), lambda i: (i, GATE_COL0 // LANES)),
                  pl.BlockSpec((tm, LANES), tab), pl.BlockSpec((tm, LANES), tab), pl.BlockSpec((tm, LANES), tab),
                  pl.BlockSpec((LANES, LANES), const),
                  pl.BlockSpec((1, LANES), const), pl.BlockSpec((N_BRANCH, LANES), const),
                  pl.BlockSpec((1, LANES), const)],
        out_specs=[pl.BlockSpec((1, N_HEADS_ATTN, tm, LANES), lambda i: (i // tiles, 0, i % tiles, 0)),
                   pl.BlockSpec((tm, KV_COLS), row), pl.BlockSpec((tm, KV_COLS), row),
                   pl.BlockSpec((tm, KV_COLS), row),
                   pl.BlockSpec((2, tm, D_KV), lambda i: (0, i, 0)),
                   pl.BlockSpec((2, tm, KV_COLS), lambda i: (0, i, 0)),
                   pl.BlockSpec((tm, LANES), row)],
        out_shape=[jax.ShapeDtypeStruct((batch, N_HEADS_ATTN, seq, LANES), bf16),
                   kv_shape, kv_shape, kv_shape,
                   jax.ShapeDtypeStruct((2, m, D_KV), bf16),
                   jax.ShapeDtypeStruct((2, m, KV_COLS), bf16),
                   jax.ShapeDtypeStruct((m, LANES), f32)],
        compiler_params=pltpu.CompilerParams(
            dimension_semantics=("parallel",), vmem_limit_bytes=VMEM_LIMIT),
        name="attn_prep",
    )(z, z, z, z, z, c, s1, s2, seg_ones, q_g, k_g, gate_b)


CHUNK_COLS = STRIDE_CMP * KV_COLS
CMP_R = BLOCK_CMP // STRIDE_CMP


def compress_weights(cmp_pe, cmp_w):
    pe = cmp_pe.reshape(2, CMP_R, STRIDE_CMP, HEAD_DIM)
    w = cmp_w.reshape(2, CMP_R, STRIDE_CMP, HEAD_DIM, HEAD_DIM)
    eye_c, eye_g = jnp.eye(2, dtype=f32), jnp.eye(N_KV_HEADS, dtype=f32)
    wb = jnp.einsum('cisde,cx,gy->iscgdxye', w, eye_c, eye_g).reshape(CMP_R, CHUNK_COLS, KV_COLS)
    peb = jnp.broadcast_to(jnp.transpose(pe, (1, 2, 0, 3))[:, :, :, None, :],
                           (CMP_R, STRIDE_CMP, 2, N_KV_HEADS, HEAD_DIM)).reshape(CMP_R, 1, CHUNK_COLS)
    return wb.astype(bf16), peb


def _compress_kernel(x_ref, pe_ref, w_ref, kc_ref, vvc_ref):
    x = x_ref[0]
    n = x.shape[0]
    y0 = jnp.dot((x + pe_ref[0]).astype(bf16), w_ref[0], preferred_element_type=f32)
    y1 = jnp.dot((x + pe_ref[1]).astype(bf16), w_ref[1], preferred_element_type=f32)
    comp = y0 + pltpu.roll(y1, n - 1, 0)
    row = lax.broadcasted_iota(jnp.int32, comp.shape, 0)
    comp = jnp.where(row < n - 1, comp, 0.0)
    kc_ref[0] = comp[:, :D_KV].astype(bf16)
    vc = comp[:, D_KV:]
    vvc_ref[0, :, :D_KV] = vc.astype(bf16)
    vvc_ref[0, :, D_KV:] = pltpu.roll(vc, HEAD_DIM, 1).astype(bf16)


def compress(kv, wb, peb, batch):
    n_chunks = kv.shape[0] // batch // STRIDE_CMP
    x = kv.reshape(batch, n_chunks, CHUNK_COLS)
    return pl.pallas_call(
        _compress_kernel,
        grid=(batch,),
        in_specs=[pl.BlockSpec((1, n_chunks, CHUNK_COLS), lambda b: (b, 0, 0)),
                  pl.BlockSpec((CMP_R, 1, CHUNK_COLS), lambda b: (0, 0, 0)),
                  pl.BlockSpec((CMP_R, CHUNK_COLS, KV_COLS), lambda b: (0, 0, 0))],
        out_specs=[pl.BlockSpec((1, n_chunks, D_KV), lambda b: (b, 0, 0)),
                   pl.BlockSpec((1, n_chunks, KV_COLS), lambda b: (b, 0, 0))],
        out_shape=[jax.ShapeDtypeStruct((batch, n_chunks, D_KV), bf16),
                   jax.ShapeDtypeStruct((batch, n_chunks, KV_COLS), bf16)],
        compiler_params=pltpu.CompilerParams(
            dimension_semantics=("parallel",), vmem_limit_bytes=VMEM_LIMIT),
        name="compress",
    )(x, peb, wb)


KEY_CHUNK = 512
NT_DIMS = (((1,), (1,)), ((), ()))


def _softmax_rows(s3, mask):
    s3 = jnp.where(mask[None], s3, NEG)
    m = jnp.max(s3, axis=-1, keepdims=True)
    p = jnp.where(mask[None], jnp.exp(s3 - m), 0.0)
    l = jnp.sum(p, axis=-1, keepdims=True)
    return p * jnp.where(l > 0.0, 1.0 / l, 0.0)


def _nsa_prompt_kernel(q_ref, gate_ref, kc_ref, vvc_ref, ks_ref, vvs_ref, kw_ref, vvw_ref, smap_ref, e_ref,
                       o_ref, mask_ref, m_ref, l_ref, acc_ref, *, seq):
    n_slc = seq // BLOCK_SLC
    n_cmp = seq // STRIDE_CMP
    hq = HEADS_PER_KV * QBLOCK
    q0 = pl.program_id(1) * QBLOCK
    qpos_col = q0 + lax.broadcasted_iota(jnp.int32, (QBLOCK, 1), 0)

    heads_out = []
    for g in range(N_KV_HEADS):
        q = q_ref[0, g * HEADS_PER_KV:(g + 1) * HEADS_PER_KV].reshape(hq, LANES)

        s = lax.dot_general(q, kc_ref[0], NT_DIMS, preferred_element_type=f32)
        c_end = lax.broadcasted_iota(jnp.int32, (1, n_cmp), 1) * STRIDE_CMP + (BLOCK_CMP - 1)
        p_c = _softmax_rows(s.reshape(HEADS_PER_KV, QBLOCK, n_cmp), c_end <= qpos_col)
        p_c = p_c.astype(bf16).reshape(hq, n_cmp)
        o_c = jnp.dot(p_c, vvc_ref[0], preferred_element_type=f32)

        imp = jnp.zeros((n_slc, QBLOCK), f32)
        for h in range(HEADS_PER_KV):
            imp = imp + lax.dot_general(smap_ref[...], p_c[h * QBLOCK:(h + 1) * QBLOCK], NT_DIMS,
                                        preferred_element_type=f32)
        jj = lax.broadcasted_iota(jnp.int32, (n_slc, QBLOCK), 0)
        qp = q0 + lax.broadcasted_iota(jnp.int32, (n_slc, QBLOCK), 1)
        cur = qp // BLOCK_SLC
        forced = (jj == 0) | ((jj <= cur) & (jj > cur - N_LOCAL))
        imp = jnp.where(jj * BLOCK_SLC <= qp, jnp.where(forced, FORCED_SCORE, imp), NEG)
        rank = jnp.zeros((n_slc, QBLOCK), f32)
        for j in range(n_slc):
            row = imp[j:j + 1, :]
            rank = rank + jnp.where((row > imp) | ((row == imp) & (jj > j)), 1.0, 0.0)
        sel_t = jnp.where(rank < float(min(N_SELECT, n_slc)), 1.0, 0.0)
        if n_slc < LANES:
            sel_t = jnp.concatenate([sel_t, jnp.zeros((LANES - n_slc, QBLOCK), f32)], axis=0)
        sel = sel_t.T.astype(bf16)
        for c in range(seq // KEY_CHUNK):
            mask_ref[c] = jnp.dot(sel, e_ref[:, c * KEY_CHUNK:(c + 1) * KEY_CHUNK], preferred_element_type=f32)

        m_ref[...] = jnp.full(m_ref.shape, NEG, f32)
        l_ref[...] = jnp.zeros(l_ref.shape, f32)
        acc_ref[...] = jnp.zeros(acc_ref.shape, f32)

        def sel_step(c, carry):
            k0 = pl.multiple_of(c * KEY_CHUNK, KEY_CHUNK)
            sc = lax.dot_general(q, ks_ref[0, pl.ds(k0, KEY_CHUNK), :], NT_DIMS, preferred_element_type=f32)
            kpos = k0 + lax.broadcasted_iota(jnp.int32, (1, KEY_CHUNK), 1)
            msk = ((mask_ref[c] > 0.5) & (kpos <= qpos_col))[None]
            s3 = jnp.where(msk, sc.reshape(HEADS_PER_KV, QBLOCK, KEY_CHUNK), NEG)
            m_old = m_ref[...]
            m_new = jnp.maximum(m_old, jnp.max(s3, axis=-1, keepdims=True))
            alpha = jnp.exp(m_old - m_new)
            p = jnp.where(msk, jnp.exp(s3 - m_new), 0.0)
            l_ref[...] = alpha * l_ref[...] + jnp.sum(p, axis=-1, keepdims=True)
            pv = jnp.dot(p.astype(bf16).reshape(hq, KEY_CHUNK), vvs_ref[0, pl.ds(k0, KEY_CHUNK), :],
                         preferred_element_type=f32)
            acc_ref[...] = alpha * acc_ref[...] + pv.reshape(HEADS_PER_KV, QBLOCK, KV_COLS)
            m_ref[...] = m_new
            return carry

        lax.fori_loop(0, q0 // KEY_CHUNK + 1, sel_step, 0)
        o_s = (acc_ref[...] / l_ref[...]).reshape(hq, KV_COLS)

        n_win = WINDOW + QBLOCK
        w0 = pl.multiple_of(jnp.maximum(q0 - WINDOW, 0), QBLOCK)
        sw = lax.dot_general(q, kw_ref[0, pl.ds(w0, n_win), :], NT_DIMS, preferred_element_type=f32)
        dist = qpos_col - (w0 + lax.broadcasted_iota(jnp.int32, (1, n_win), 1))
        p_w = _softmax_rows(sw.reshape(HEADS_PER_KV, QBLOCK, n_win), (dist >= 0) & (dist < WINDOW))
        o_w = jnp.dot(p_w.astype(bf16).reshape(hq, n_win), vvw_ref[0, pl.ds(w0, n_win), :],
                      preferred_element_type=f32)

        for hh in range(HEADS_PER_KV):
            h = g * HEADS_PER_KV + hh
            half = 0 if (h % 2) == g else 1
            rows = slice(hh * QBLOCK, (hh + 1) * QBLOCK)
            cols = slice(half * LANES, (half + 1) * LANES)
            gc = gate_ref[:, N_BRANCH * h:N_BRANCH * h + 1]
            gs = gate_ref[:, N_BRANCH * h + 1:N_BRANCH * h + 2]
            gw = gate_ref[:, N_BRANCH * h + 2:N_BRANCH * h + 3]
            heads_out.append(gc * o_c[rows, cols] + gs * o_s[rows, cols] + gw * o_w[rows, cols])

    lane = lax.broadcasted_iota(jnp.int32, (QBLOCK, LANES), 1)
    for pair in range(N_HEADS_ATTN // 2):
        o_ref[0, :, pair * LANES:(pair + 1) * LANES] = jnp.where(
            lane < HEAD_DIM, heads_out[2 * pair], heads_out[2 * pair + 1])


def nsa_tables(seq):
    n_slc, n_cmp = seq // BLOCK_SLC, seq // STRIDE_CMP
    start = np.arange(n_cmp) * STRIDE_CMP
    first, last = start // BLOCK_SLC, (start + BLOCK_CMP - 1) // BLOCK_SLC
    j = np.arange(n_slc)
    smap = (first[None, :] <= j[:, None]) & (last[None, :] >= j[:, None]) & (np.arange(n_cmp)[None, :] < n_cmp - 1)
    expand = (np.arange(seq)[None, :] // BLOCK_SLC) == np.arange(max(n_slc, LANES))[:, None]
    return jnp.asarray(smap, bf16), jnp.asarray(expand, bf16)


def nsa_prompt(q_ext, gates, kc, vvc, kb, vvb, tables, batch, seq):
    smap, expand = tables
    n_cmp = seq // STRIDE_CMP
    hq = HEADS_PER_KV * QBLOCK
    per_b = lambda b, i: (b, 0, 0)
    const = lambda b, i: (0, 0)
    gates3 = gates.reshape(batch, seq, LANES)
    kb4 = kb.reshape(2, batch, seq, D_KV)
    vvb4 = vvb.reshape(2, batch, seq, KV_COLS)
    branch = lambda br: (lambda b, i: (br, b, 0, 0))
    return pl.pallas_call(
        functools.partial(_nsa_prompt_kernel, seq=seq),
        grid=(batch, seq // QBLOCK),
        in_specs=[pl.BlockSpec((1, N_HEADS_ATTN, QBLOCK, LANES), lambda b, i: (b, 0, i, 0)),
                  pl.BlockSpec((None, QBLOCK, LANES), lambda b, i: (b, i, 0)),
                  pl.BlockSpec((1, n_cmp, D_KV), per_b), pl.BlockSpec((1, n_cmp, KV_COLS), per_b),
                  pl.BlockSpec((None, 1, seq, D_KV), branch(0)), pl.BlockSpec((None, 1, seq, KV_COLS), branch(0)),
                  pl.BlockSpec((None, 1, seq, D_KV), branch(1)), pl.BlockSpec((None, 1, seq, KV_COLS), branch(1)),
                  pl.BlockSpec(smap.shape, const), pl.BlockSpec(expand.shape, const)],
        out_specs=pl.BlockSpec((1, QBLOCK, D_ATTN), lambda b, i: (b, i, 0)),
        out_shape=jax.ShapeDtypeStruct((batch, seq, D_ATTN), f32),
        scratch_shapes=[pltpu.VMEM((seq // KEY_CHUNK, QBLOCK, KEY_CHUNK), f32),
                        pltpu.VMEM((HEADS_PER_KV, QBLOCK, 1), f32),
                        pltpu.VMEM((HEADS_PER_KV, QBLOCK, 1), f32),
                        pltpu.VMEM((HEADS_PER_KV, QBLOCK, KV_COLS), f32)],
        compiler_params=pltpu.CompilerParams(
            dimension_semantics=("parallel", "arbitrary"), vmem_limit_bytes=VMEM_LIMIT),
        name="nsa_prompt",
    )(q_ext, gates3, kc, vvc, kb4, vvb4, kb4, vvb4, smap, expand)


def _split(x, sizes):
    offs = [int(o) for o in np.cumsum(sizes)[:-1]]
    return jnp.split(x, offs, axis=-1)


def rms_norm(x, g):
    xf = x.astype(f32)
    y = xf * lax.rsqrt(jnp.mean(xf * xf, axis=-1, keepdims=True) + NORM_EPS)
    return (y * g.astype(f32)).astype(x.dtype)


def rope_tables(pos):
    inv = ROPE_THETA ** (-(jnp.arange(0, ROT_DIM, 2, dtype=f32) / ROT_DIM))
    ang = pos.astype(f32)[:, None] * inv[None, :]
    return jnp.cos(ang), jnp.sin(ang)


def apply_partial_rope(x, cos, sin):
    xr = x[..., :ROT_DIM].astype(f32)
    x1, x2 = xr[..., :ROT_DIM // 2], xr[..., ROT_DIM // 2:]
    c, s = cos[None, :, None, :], sin[None, :, None, :]
    rot = jnp.concatenate([x1 * c - x2 * s, x2 * c + x1 * s], axis=-1).astype(x.dtype)
    return jnp.concatenate([rot, x[..., ROT_DIM:]], axis=-1)


def masked_softmax(s, mask):
    p = jax.nn.softmax(jnp.where(mask, s, NEG), axis=-1)
    return jnp.where(mask, p, 0.0)


def gather_pages(pool, page_table):
    g = pool[page_table]
    return g.reshape(page_table.shape[0], -1, 2, N_KV_HEADS, HEAD_DIM)


def gather_blocks(blocks, idx):
    return jax.vmap(jax.vmap(lambda b, i: b[i]))(blocks, idx)


def nsa_heads(z_attn, pos, q_norm_g, k_norm_g):
    B, T = z_attn.shape[:2]
    parts = _split(z_attn, [D_ATTN] + [D_KV] * 6)
    cos, sin = rope_tables(pos)
    q = apply_partial_rope(rms_norm(parts[0].reshape(B, T, N_HEADS_ATTN, HEAD_DIM), q_norm_g), cos, sin)
    kvs = []
    for br in range(N_BRANCH):
        k = parts[1 + 2 * br].reshape(B, T, N_KV_HEADS, HEAD_DIM)
        v = parts[2 + 2 * br].reshape(B, T, N_KV_HEADS, HEAD_DIM)
        k = apply_partial_rope(rms_norm(k, k_norm_g[br]), cos, sin)
        kvs.append(jnp.stack([k, v], axis=2))
    return q, kvs


def compress_blocks(kv, cmp_pe, cmp_w):
    B, L = kv.shape[:2]
    n_cmp = (L - BLOCK_CMP) // STRIDE_CMP + 1
    r = BLOCK_CMP // STRIDE_CMP
    n_chunks = n_cmp + r - 1
    chunks = kv[:, :n_chunks * STRIDE_CMP].reshape(B, n_chunks, STRIDE_CMP, 2, N_KV_HEADS, HEAD_DIM)
    pe = cmp_pe.reshape(2, r, STRIDE_CMP, HEAD_DIM)
    w = cmp_w.reshape(2, r, STRIDE_CMP, HEAD_DIM, HEAD_DIM)
    comp = 0.0
    for i in range(r):
        part = chunks[:, i:i + n_cmp] + jnp.moveaxis(pe[:, i], 0, 1)[None, None, :, :, None, :]
        comp = comp + jnp.einsum('bnscgd,csde->bncge', part, w[:, i])
    c_end = jnp.arange(n_cmp, dtype=jnp.int32) * STRIDE_CMP + BLOCK_CMP - 1
    return comp[:, :, 0], comp[:, :, 1], c_end


def cmp_to_slc_map(n_cmp, n_slc):
    start = np.arange(n_cmp) * STRIDE_CMP
    first, last = start // BLOCK_SLC, (start + BLOCK_CMP - 1) // BLOCK_SLC
    j = np.arange(n_slc)
    return jnp.asarray(((first[:, None] <= j[None, :]) & (last[:, None] >= j[None, :])).astype(np.float32))


def to_slc_blocks(kv):
    B, L = kv.shape[:2]
    n_slc = -(-L // BLOCK_SLC)
    kv = jnp.pad(kv, ((0, 0), (0, n_slc * BLOCK_SLC - L), (0, 0), (0, 0), (0, 0)))
    return kv.reshape(B, n_slc, BLOCK_SLC, 2, N_KV_HEADS, HEAD_DIM).transpose(0, 4, 1, 2, 3, 5)


def nsa_query_block(q, q_pos, kc, vc, c_end, sel_map, slc_blocks, win_kv, w_pos, gates):
    B, Tq = q.shape[:2]
    qg = q.astype(f32).reshape(B, Tq, N_KV_HEADS, HEADS_PER_KV, HEAD_DIM) * (HEAD_DIM ** -0.5)
    s_c = jnp.einsum('bqghd,bngd->bghqn', qg, kc.astype(f32))
    p_c = masked_softmax(s_c, c_end[None, :] <= q_pos[:, None])
    o_c = jnp.einsum('bghqn,bngd->bqghd', p_c, vc.astype(f32))
    n_slc = slc_blocks.shape[2]
    imp = jnp.einsum('bghqn,ns->bgqs', p_c, sel_map)
    j = jnp.arange(n_slc, dtype=jnp.int32)[None, :]
    cur = (q_pos // BLOCK_SLC)[:, None]
    forced = (j == 0) | ((j <= cur) & (j > cur - N_LOCAL))
    causal = j * BLOCK_SLC <= q_pos[:, None]
    imp = jnp.where(causal, jnp.where(forced, FORCED_SCORE, imp), NEG)
    n_sel = min(N_SELECT, n_slc)
    _, idx = lax.top_k(imp, n_sel)
    g_kv = gather_blocks(slc_blocks, idx.reshape(B, N_KV_HEADS, Tq * n_sel))
    g_kv = g_kv.reshape(B, N_KV_HEADS, Tq, n_sel * BLOCK_SLC, 2, HEAD_DIM).astype(f32)
    k_pos = (idx[..., None] * BLOCK_SLC + jnp.arange(BLOCK_SLC, dtype=jnp.int32)).reshape(B, N_KV_HEADS, Tq, n_sel * BLOCK_SLC)
    s_s = jnp.einsum('bqghd,bgqkd->bghqk', qg, g_kv[..., 0, :])
    p_s = masked_softmax(s_s, (k_pos <= q_pos[:, None])[:, :, None])
    o_s = jnp.einsum('bghqk,bgqkd->bqghd', p_s, g_kv[..., 1, :])
    s_w = jnp.einsum('bqghd,bkgd->bghqk', qg, win_kv[:, :, 0].astype(f32))
    dist = q_pos[:, None] - w_pos[None, :]
    p_w = masked_softmax(s_w, (dist >= 0) & (dist < WINDOW) & (w_pos[None, :] >= 0))
    o_w = jnp.einsum('bghqk,bkgd->bqghd', p_w, win_kv[:, :, 1].astype(f32))
    g = gates.astype(f32).reshape(B, Tq, N_KV_HEADS, HEADS_PER_KV, N_BRANCH)
    o = g[..., 0:1] * o_c + g[..., 1:2] * o_s + g[..., 2:3] * o_w
    return o.reshape(B, Tq, D_ATTN).astype(q.dtype)


def prompt_attention(q, kvs, gates, lp):
    B, T = q.shape[:2]
    kc, vc, c_end = compress_blocks(kvs[0], lp['cmp_pe'], lp['cmp_w'])
    slc_blocks = to_slc_blocks(kvs[1])
    sel_map = cmp_to_slc_map(kc.shape[1], slc_blocks.shape[2])
    win_pad = jnp.pad(kvs[2], ((0, 0), (WINDOW, 0), (0, 0), (0, 0), (0, 0)))

    def one_query_block(q0):
        q_pos = q0 + jnp.arange(QBLOCK, dtype=jnp.int32)
        w_pos = q0 - WINDOW + jnp.arange(WINDOW + QBLOCK, dtype=jnp.int32)
        return nsa_query_block(
            lax.dynamic_slice_in_dim(q, q0, QBLOCK, axis=1), q_pos, kc, vc, c_end, sel_map, slc_blocks,
            lax.dynamic_slice_in_dim(win_pad, q0, WINDOW + QBLOCK, axis=1), w_pos,
            lax.dynamic_slice_in_dim(gates, q0, QBLOCK, axis=1))

    out = lax.map(one_query_block, jnp.arange(0, T, QBLOCK, dtype=jnp.int32))
    out = jnp.moveaxis(out, 0, 1).reshape(B, T, D_ATTN)
    cmp_pages = kvs[0].reshape(B, T // PAGE_SIZE, PAGE_SIZE, 2, N_KV_HEADS, HEAD_DIM)
    slc_pages = kvs[1].reshape(B, T // PAGE_SIZE, PAGE_SIZE, 2, N_KV_HEADS, HEAD_DIM)
    return out, (cmp_pages, slc_pages, kvs[2][:, -min(WINDOW, T):])


def sample_attention(q, kvs, gates, lp, cache_cmp, cache_slc, cache_win, page_table):
    Tn = q.shape[1]
    past = page_table.shape[1] * PAGE_SIZE
    w_buf = cache_win.shape[1]
    full_cmp = jnp.concatenate([gather_pages(cache_cmp, page_table), kvs[0]], axis=1)
    full_slc = jnp.concatenate([gather_pages(cache_slc, page_table), kvs[1]], axis=1)
    win = jnp.concatenate([cache_win, kvs[2]], axis=1)
    kc, vc, c_end = compress_blocks(full_cmp, lp['cmp_pe'], lp['cmp_w'])
    slc_blocks = to_slc_blocks(full_slc)
    sel_map = cmp_to_slc_map(kc.shape[1], slc_blocks.shape[2])
    q_pos = past + jnp.arange(Tn, dtype=jnp.int32)
    w_pos = past - w_buf + jnp.arange(w_buf + Tn, dtype=jnp.int32)
    out = nsa_query_block(q, q_pos, kc, vc, c_end, sel_map, slc_blocks, win, w_pos, gates)
    return out, (kvs[0], kvs[1], win[:, -w_buf:])


def rwkv7_time_mix(zs, shift_prev, wkv0, v_first, lp):
    B, T = zs.shape[:2]
    zf = zs.astype(f32)
    prev = jnp.concatenate([shift_prev.astype(f32)[:, None], zf[:, :-1]], axis=1)
    xs = zf + (prev - zf) * lp['mu']
    r, k, v, xw, xa, xg = _split(xs, [D_RWKV] * 3 + [RANK_W, RANK_A, RANK_G])
    w_raw = -jax.nn.softplus(-(lp['w0'] + jnp.tanh(xw) @ lp['w2'])) - 0.5
    decay = jnp.exp(-jnp.exp(w_raw))
    a = jax.nn.sigmoid(lp['a0'] + xa @ lp['a2'])
    if lp['vres'] is None:
        v_first = v
    else:
        vres0, vres2 = lp['vres']
        v = v + (v_first - v) * jax.nn.sigmoid(vres0 + xa @ vres2)
    g = jax.nn.sigmoid(xg) @ lp['g2']
    heads = lambda t: t.reshape(B, T, N_HEADS_RWKV, HEAD_DIM)
    kk = heads(k * lp['k_k'])
    kk = kk * lax.rsqrt(jnp.maximum(jnp.sum(kk * kk, axis=-1, keepdims=True), 1e-24))
    k = k * (1.0 + (a - 1.0) * lp['k_a'])
    r_h, k_h, v_h, a_h, d_h = heads(r), heads(k), heads(v), heads(a), heads(decay)

    def step(S, inp):
        r_t, d_t, kk_t, a_t, k_t, v_t = inp
        sa = jnp.einsum('bhvk,bhk->bhv', S, -kk_t)
        S = S * d_t[:, :, None, :] + sa[..., None] * (kk_t * a_t)[:, :, None, :] + v_t[..., None] * k_t[:, :, None, :]
        return S, jnp.einsum('bhvk,bhk->bhv', S, r_t)

    tm = lambda t: jnp.moveaxis(t, 1, 0)
    S_T, ys = lax.scan(step, wkv0.astype(f32), (tm(r_h), tm(d_h), tm(kk), tm(a_h), tm(k_h), tm(v_h)))
    y = jnp.moveaxis(ys, 0, 1)
    mean = jnp.mean(y, axis=-1, keepdims=True)
    var = jnp.mean(jnp.square(y - mean), axis=-1, keepdims=True)
    y = (y - mean) * lax.rsqrt(var + GN_EPS) * lp['lnx_w'].reshape(N_HEADS_RWKV, HEAD_DIM) + lp['lnx_b'].reshape(N_HEADS_RWKV, HEAD_DIM)
    y = y + jnp.sum(r_h * k_h * lp['r_k'], axis=-1, keepdims=True) * v_h
    y = y.reshape(B, T, D_RWKV) * g
    return y, S_T, zs[:, -1], v_first


def trunk_layer(x, pos, attn_fn, shift_prev, wkv0, v_first, lp):
    B, T = x.shape[:2]
    x2 = x.reshape(B * T, D_MODEL)
    z = norm_proj(x2, lp['norm1_g'], lp['w_in'])[:, :IN_COLS].reshape(B, T, IN_COLS)
    z_attn, z_gate, z_rwkv = _split(z, [D_ATTN + 6 * D_KV, N_BRANCH * N_HEADS_ATTN, SHIFT_COLS])
    q, kvs = nsa_heads(z_attn, pos, lp['q_norm_g'], lp['k_norm_g'])
    gates = jax.nn.sigmoid((z_gate + lp['gate_b']).astype(f32))
    o_attn, attn_state = attn_fn(q, kvs, gates, lp)
    o_rwkv, wkv_t, shift_t, v_first = rwkv7_time_mix(z_rwkv, shift_prev, wkv0, v_first, lp)
    mixed = jnp.concatenate([o_attn, o_rwkv.astype(o_attn.dtype)], axis=-1).reshape(B * T, D_MODEL)
    h = out_proj(x2, mixed, lp['w_out'])
    y = ffn(h, lp['norm2_g'], lp['w_up'], lp['w_down']).reshape(B, T, D_MODEL)
    return y, attn_state, wkv_t, shift_t, v_first


def prompt_trunk_layer(x, shift_prev, wkv0, v_first, lp, consts):
    B, T = x.shape[:2]
    x2 = x.reshape(B * T, D_MODEL)
    z = norm_proj(x2, lp['norm1_g'], lp['w_in'])
    q_ext, kv_c, kv_s, kv_w, kb, vvb, gates = attn_prep(
        z, consts['rope'], consts['seg_ones'], lp['q_g_lanes'], lp['k_g_lanes'], lp['gate_b_lanes'], B, T)
    kc, vvc = compress(kv_c, lp['cmp_wb'], lp['cmp_peb'], B)
    o_attn = nsa_prompt(q_ext, gates, kc, vvc, kb, vvb, consts['nsa'], B, T)
    pages = lambda kv: kv.reshape(B, T // PAGE_SIZE, PAGE_SIZE, 2, N_KV_HEADS, HEAD_DIM)
    attn_state = (pages(kv_c), pages(kv_s),
                  kv_w.reshape(B, T, 2, N_KV_HEADS, HEAD_DIM)[:, -min(WINDOW, T):])
    z_rwkv = z[:, GATE_COL0 + N_BRANCH * N_HEADS_ATTN:IN_COLS].reshape(B, T, SHIFT_COLS)
    o_rwkv, wkv_t, shift_t, v_first = rwkv7_time_mix(z_rwkv, shift_prev, wkv0, v_first, lp)
    mixed = jnp.concatenate([o_attn, o_rwkv], axis=-1).reshape(B * T, D_MODEL)
    h = out_proj(x2, mixed, lp['w_out'])
    y = ffn(h, lp['norm2_g'], lp['w_up'], lp['w_down']).reshape(B, T, D_MODEL)
    return y, attn_state, wkv_t, shift_t, v_first


def kernel(x_prompt, x_sample, cache_cmp_kv, cache_slc_kv, cache_win_kv, state_wkv, state_shift, page_table,
           norm1_g, w_in, q_norm_g, k_norm_g, gate_b, cmp_pe, cmp_w, shift_mu, w0, w2, a0, a2, vres0, vres2,
           g2, k_k, k_a, r_k, lnx_w, lnx_b, w_out, norm2_g, w_up, w_down):
    B, T = x_prompt.shape[:2]
    Tn = x_sample.shape[1]
    past = page_table.shape[1] * PAGE_SIZE
    pos_p = jnp.arange(T, dtype=jnp.int32)
    pos_s = past + jnp.arange(Tn, dtype=jnp.int32)
    shift_zero = jnp.zeros((B, SHIFT_COLS), x_prompt.dtype)
    wkv_zero = jnp.zeros((B, N_HEADS_RWKV, HEAD_DIM, HEAD_DIM), f32)
    w_in_b = jnp.pad(w_in, ((0, 0), (0, 0), (0, IN_COLS_PAD - IN_COLS))).astype(bf16)
    w_out_b, w_up_b, w_down_b = w_out.astype(bf16), w_up.astype(bf16), w_down.astype(bf16)
    seg = np.arange(LANES) // HEAD_DIM
    consts = {'rope': rope_lane_tables(pos_p), 'nsa': nsa_tables(T),
              'seg_ones': jnp.asarray(seg[:, None] == seg[None, :], bf16)}
    xp, xs = x_prompt, x_sample
    vf_p, vf_s = None, None
    p_cmp, p_slc, p_win, p_wkv, p_shift = [], [], [], [], []
    s_cmp, s_slc, s_win, s_wkv, s_shift = [], [], [], [], []
    for l in range(DEPTH):
        lp = {'norm1_g': norm1_g[l], 'w_in': w_in_b[l], 'q_norm_g': q_norm_g[l], 'k_norm_g': k_norm_g[l],
              'gate_b': gate_b[l], 'cmp_pe': cmp_pe[l], 'cmp_w': cmp_w[l], 'mu': shift_mu[l], 'w0': w0[l],
              'w2': w2[l], 'a0': a0[l], 'a2': a2[l],
              'vres': None if l == 0 else (vres0[l - 1], vres2[l - 1]),
              'g2': g2[l], 'k_k': k_k[l], 'k_a': k_a[l], 'r_k': r_k[l], 'lnx_w': lnx_w[l], 'lnx_b': lnx_b[l],
              'w_out': w_out_b[l], 'norm2_g': norm2_g[l], 'w_up': w_up_b[l], 'w_down': w_down_b[l]}
        lp['cmp_wb'], lp['cmp_peb'] = compress_weights(cmp_pe[l], cmp_w[l])
        lp['q_g_lanes'] = jnp.tile(q_norm_g[l], 2).reshape(1, LANES)
        lp['k_g_lanes'] = jnp.tile(k_norm_g[l], (1, 2))
        lp['gate_b_lanes'] = jnp.pad(gate_b[l], (0, LANES - N_BRANCH * N_HEADS_ATTN)).reshape(1, LANES)
        xp, (c_kv, sl_kv, w_kv), wkv_t, sh_t, vf_p = prompt_trunk_layer(
            xp, shift_zero, wkv_zero, vf_p, lp, consts)
        p_cmp.append(c_kv); p_slc.append(sl_kv); p_win.append(w_kv); p_wkv.append(wkv_t); p_shift.append(sh_t)
        sample_fn = functools.partial(sample_attention, cache_cmp=cache_cmp_kv[l], cache_slc=cache_slc_kv[l],
                                      cache_win=cache_win_kv[l], page_table=page_table)
        xs, (c_kv, sl_kv, w_kv), wkv_t, sh_t, vf_s = trunk_layer(
            xs, pos_s, sample_fn, state_shift[l], state_wkv[l], vf_s, lp)
        s_cmp.append(c_kv); s_slc.append(sl_kv); s_win.append(w_kv); s_wkv.append(wkv_t); s_shift.append(sh_t)
    return (xp, xs, jnp.stack(p_cmp), jnp.stack(p_slc), jnp.stack(p_win), jnp.stack(p_wkv), jnp.stack(p_shift),
            jnp.stack(s_cmp), jnp.stack(s_slc), jnp.stack(s_win), jnp.stack(s_wkv), jnp.stack(s_shift))
```

```python
import functools
import jax, jax.numpy as jnp
from jax import lax
import numpy as np
from jax.experimental import pallas as pl
from jax.experimental.pallas import tpu as pltpu

D_MODEL = 1024
DEPTH = 4
PAGE_SIZE = 128
HEAD_DIM = 64
N_HEADS_ATTN = 8
N_KV_HEADS = 2
HEADS_PER_KV = N_HEADS_ATTN // N_KV_HEADS
D_ATTN = N_HEADS_ATTN * HEAD_DIM
D_KV = N_KV_HEADS * HEAD_DIM
N_BRANCH = 3
N_HEADS_RWKV = 8
D_RWKV = N_HEADS_RWKV * HEAD_DIM
ROT_DIM = HEAD_DIM // 4
ROPE_THETA = 500000.0
BLOCK_CMP = 32
STRIDE_CMP = 16
BLOCK_SLC = 64
N_SELECT = 16
N_LOCAL = 2
WINDOW = 512
QBLOCK = 128
RANK_W = 32
RANK_A = 32
RANK_G = 64
D_FF = 4 * D_MODEL
SHIFT_COLS = 3 * D_RWKV + RANK_W + RANK_A + RANK_G
IN_COLS = D_ATTN + 6 * D_KV + N_BRANCH * N_HEADS_ATTN + SHIFT_COLS
IN_COLS_PAD = 3072
NORM_EPS = 1e-6
GN_EPS = 64e-5
NEG = -1e30
FORCED_SCORE = 1e9

VMEM_LIMIT = 48 * 1024 * 1024
bf16 = jnp.bfloat16
f32 = jnp.float32


def _row_tile(m):
    return min(m, 1024)


def _norm_proj_kernel(x_ref, g_ref, w_ref, o_ref):
    x = x_ref[...]
    n = x * lax.rsqrt(jnp.mean(x * x, axis=-1, keepdims=True) + NORM_EPS) * g_ref[...]
    o_ref[...] = jnp.dot(n.astype(bf16), w_ref[...], preferred_element_type=f32)


def norm_proj(x, g, w):
    m, n = x.shape[0], w.shape[1]
    tm, tn = _row_tile(m), 512
    return pl.pallas_call(
        _norm_proj_kernel,
        grid=(m // tm, n // tn),
        in_specs=[pl.BlockSpec((tm, D_MODEL), lambda i, j: (i, 0)),
                  pl.BlockSpec((1, D_MODEL), lambda i, j: (0, 0)),
                  pl.BlockSpec((D_MODEL, tn), lambda i, j: (0, j))],
        out_specs=pl.BlockSpec((tm, tn), lambda i, j: (i, j)),
        out_shape=jax.ShapeDtypeStruct((m, n), f32),
        compiler_params=pltpu.CompilerParams(
            dimension_semantics=("parallel", "arbitrary"), vmem_limit_bytes=VMEM_LIMIT),
        name="norm_proj",
    )(x, g.reshape(1, D_MODEL), w)


def _out_proj_kernel(x_ref, a_ref, r_ref, w_ref, o_ref):
    o_ref[...] = (x_ref[...]
                  + jnp.dot(a_ref[...].astype(bf16), w_ref[:D_ATTN, :], preferred_element_type=f32)
                  + jnp.dot(r_ref[...].astype(bf16), w_ref[D_ATTN:, :], preferred_element_type=f32))


def out_proj(x, o_attn, o_rwkv, w):
    m = x.shape[0]
    tm = _row_tile(m)
    return pl.pallas_call(
        _out_proj_kernel,
        grid=(m // tm,),
        in_specs=[pl.BlockSpec((tm, D_MODEL), lambda i: (i, 0)),
                  pl.BlockSpec((tm, D_ATTN), lambda i: (i, 0)),
                  pl.BlockSpec((tm, D_RWKV), lambda i: (i, 0)),
                  pl.BlockSpec((D_MODEL, D_MODEL), lambda i: (0, 0))],
        out_specs=pl.BlockSpec((tm, D_MODEL), lambda i: (i, 0)),
        out_shape=jax.ShapeDtypeStruct((m, D_MODEL), f32),
        compiler_params=pltpu.CompilerParams(
            dimension_semantics=("parallel",), vmem_limit_bytes=VMEM_LIMIT),
        name="out_proj",
    )(x, o_attn, o_rwkv, w)


def _ffn_kernel(h_ref, g_ref, wu_ref, wd_ref, o_ref, n_ref):
    j = pl.program_id(1)

    @pl.when(j == 0)
    def _():
        h = h_ref[...]
        n = h * lax.rsqrt(jnp.mean(h * h, axis=-1, keepdims=True) + NORM_EPS) * g_ref[...]
        n_ref[...] = n.astype(bf16)
        o_ref[...] = h

    u = jnp.dot(n_ref[...], wu_ref[...], preferred_element_type=f32)
    u = jnp.square(jnp.maximum(u, 0.0))
    o_ref[...] += jnp.dot(u.astype(bf16), wd_ref[...], preferred_element_type=f32)


def ffn(h, g, w_up, w_down):
    m = h.shape[0]
    tm, tf = _row_tile(m), 512
    return pl.pallas_call(
        _ffn_kernel,
        grid=(m // tm, D_FF // tf),
        in_specs=[pl.BlockSpec((tm, D_MODEL), lambda i, j: (i, 0)),
                  pl.BlockSpec((1, D_MODEL), lambda i, j: (0, 0)),
                  pl.BlockSpec((D_MODEL, tf), lambda i, j: (0, j)),
                  pl.BlockSpec((tf, D_MODEL), lambda i, j: (j, 0))],
        out_specs=pl.BlockSpec((tm, D_MODEL), lambda i, j: (i, 0)),
        out_shape=jax.ShapeDtypeStruct((m, D_MODEL), f32),
        scratch_shapes=[pltpu.VMEM((tm, D_MODEL), bf16)],
        compiler_params=pltpu.CompilerParams(
            dimension_semantics=("parallel", "arbitrary"), vmem_limit_bytes=VMEM_LIMIT),
        name="ffn",
    )(h, g.reshape(1, D_MODEL), w_up, w_down)


LANES = 128
KV_COLS = 2 * D_KV
N_GATES = N_BRANCH * N_HEADS_ATTN
N_LOWRANK = RANK_W + RANK_A + RANK_G
Z_Q, Z_R, Z_K, Z_V = 0, D_ATTN, D_ATTN + D_RWKV, D_ATTN + 2 * D_RWKV
Z_KV = D_ATTN + 3 * D_RWKV
Z_GATE = Z_KV + N_BRANCH * KV_COLS
Z_LR = Z_GATE + LANES
assert N_LOWRANK == LANES and Z_LR + N_LOWRANK == IN_COLS_PAD and N_GATES <= LANES


def permute_w_in(w_in):
    o_kv, o_gate, o_rwkv = D_ATTN, D_ATTN + 6 * D_KV, D_ATTN + 6 * D_KV + N_GATES
    pad = jnp.zeros(w_in.shape[:-1] + (LANES - N_GATES,), w_in.dtype)
    return jnp.concatenate([w_in[..., :o_kv], w_in[..., o_rwkv:o_rwkv + 3 * D_RWKV], w_in[..., o_kv:o_gate],
                            w_in[..., o_gate:o_rwkv], pad, w_in[..., o_rwkv + 3 * D_RWKV:]], axis=-1)


def split_z(z):
    z_attn = jnp.concatenate([z[..., :D_ATTN], z[..., Z_KV:Z_GATE]], axis=-1)
    z_rwkv = jnp.concatenate([z[..., Z_R:Z_KV], z[..., Z_LR:]], axis=-1)
    return z_attn, z[..., Z_GATE:Z_GATE + N_GATES], z_rwkv


def _seg_sumsq(x, p_ref):
    x2 = x * x
    hi = x2.astype(bf16)
    lo = (x2 - hi.astype(f32)).astype(bf16)
    return (jnp.dot(hi, p_ref[...], preferred_element_type=f32)
            + jnp.dot(lo, p_ref[...], preferred_element_type=f32))


def _norm_rope(x, g, p_ref, c, s1, s2):
    y = x * lax.rsqrt(_seg_sumsq(x, p_ref) * (1.0 / HEAD_DIM) + NORM_EPS) * g
    return y * c + pltpu.roll(y, LANES - ROT_DIM // 2, 1) * s1 + pltpu.roll(y, ROT_DIM // 2, 1) * s2


def _attn_prep_kernel(zq_ref, zc_ref, zs_ref, zw_ref, zg_ref, c_ref, s1_ref, s2_ref, p_ref, qg_ref, kg_ref, gb_ref,
                      q_ref, kvc_ref, kvs_ref, kvw_ref, kb_ref, vvb_ref, gate_ref):
    c, s1, s2 = c_ref[...], s1_ref[...], s2_ref[...]
    lane = lax.broadcasted_iota(jnp.int32, (zq_ref.shape[0], LANES), 1)
    for pair in range(N_HEADS_ATTN // 2):
        y = _norm_rope(zq_ref[:, pair * LANES:(pair + 1) * LANES], qg_ref[...], p_ref, c, s1, s2)
        y = y * (HEAD_DIM ** -0.5)
        y_sw = pltpu.roll(y, HEAD_DIM, 1)
        for half in range(2):
            h = 2 * pair + half
            grp = h // HEADS_PER_KV
            src = y if half == grp else y_sw
            keep = (lane >= HEAD_DIM) if grp == 1 else (lane < HEAD_DIM)
            q_ref[0, h] = jnp.where(keep, src, 0.0).astype(bf16)
    ins = (zc_ref, zs_ref, zw_ref)
    outs = (kvc_ref, kvs_ref, kvw_ref)
    for br in range(N_BRANCH):
        k = _norm_rope(ins[br][:, :D_KV], kg_ref[br:br + 1, :], p_ref, c, s1, s2)
        v = ins[br][:, D_KV:]
        outs[br][:, :D_KV] = k
        outs[br][:, D_KV:] = v
        if br > 0:
            kb_ref[br - 1] = k.astype(bf16)
            vvb_ref[br - 1, :, :D_KV] = v.astype(bf16)
            vvb_ref[br - 1, :, D_KV:] = pltpu.roll(v, HEAD_DIM, 1).astype(bf16)
    gate_ref[...] = jax.nn.sigmoid(zg_ref[...] + gb_ref[...])


def rope_lane_tables(pos):
    half = ROT_DIM // 2
    inv = ROPE_THETA ** (-(jnp.arange(0, ROT_DIM, 2, dtype=f32) / ROT_DIM))
    ang = pos.astype(f32)[:, None] * inv[None, :]
    cos, sin = jnp.cos(ang), jnp.sin(ang)
    t = pos.shape[0]
    one, zero = jnp.ones((t, HEAD_DIM - ROT_DIM), f32), jnp.zeros((t, HEAD_DIM - ROT_DIM), f32)
    zh = jnp.zeros((t, half), f32)
    c = jnp.concatenate([cos, cos, one], axis=1)
    s1 = jnp.concatenate([-sin, zh, zero], axis=1)
    s2 = jnp.concatenate([zh, sin, zero], axis=1)
    return tuple(jnp.tile(a, (1, 2)) for a in (c, s1, s2))


def attn_prep(z, rope_tabs, seg_ones, q_g, k_g, gate_b, batch, seq):
    m = batch * seq
    tm = 512
    tiles = seq // tm
    c, s1, s2 = rope_tabs
    row = lambda i: (i, 0)
    tab = lambda i: (i % tiles, 0)
    const = lambda i: (0, 0)
    kv_shape = jax.ShapeDtypeStruct((m, KV_COLS), f32)
    return pl.pallas_call(
        _attn_prep_kernel,
        grid=(m // tm,),
        in_specs=[pl.BlockSpec((tm, D_ATTN), row),
                  pl.BlockSpec((tm, KV_COLS), lambda i: (i, Z_KV // KV_COLS)),
                  pl.BlockSpec((tm, KV_COLS), lambda i: (i, Z_KV // KV_COLS + 1)),
                  pl.BlockSpec((tm, KV_COLS), lambda i: (i, Z_KV // KV_COLS + 2)),
                  pl.BlockSpec((tm, LANES), lambda i: (i, Z_GATE // LANES)),
                  pl.BlockSpec((tm, LANES), tab), pl.BlockSpec((tm, LANES), tab), pl.BlockSpec((tm, LANES), tab),
                  pl.BlockSpec((LANES, LANES), const),
                  pl.BlockSpec((1, LANES), const), pl.BlockSpec((N_BRANCH, LANES), const),
                  pl.BlockSpec((1, LANES), const)],
        out_specs=[pl.BlockSpec((1, N_HEADS_ATTN, tm, LANES), lambda i: (i // tiles, 0, i % tiles, 0)),
                   pl.BlockSpec((tm, KV_COLS), row), pl.BlockSpec((tm, KV_COLS), row),
                   pl.BlockSpec((tm, KV_COLS), row),
                   pl.BlockSpec((2, tm, D_KV), lambda i: (0, i, 0)),
                   pl.BlockSpec((2, tm, KV_COLS), lambda i: (0, i, 0)),
                   pl.BlockSpec((tm, LANES), row)],
        out_shape=[jax.ShapeDtypeStruct((batch, N_HEADS_ATTN, seq, LANES), bf16),
                   kv_shape, kv_shape, kv_shape,
                   jax.ShapeDtypeStruct((2, m, D_KV), bf16),
                   jax.ShapeDtypeStruct((2, m, KV_COLS), bf16),
                   jax.ShapeDtypeStruct((m, LANES), f32)],
        compiler_params=pltpu.CompilerParams(
            dimension_semantics=("parallel",), vmem_limit_bytes=VMEM_LIMIT),
        name="attn_prep",
    )(z, z, z, z, z, c, s1, s2, seg_ones, q_g, k_g, gate_b)


CHUNK_COLS = STRIDE_CMP * KV_COLS
CMP_R = BLOCK_CMP // STRIDE_CMP


def compress_weights(cmp_pe, cmp_w):
    pe = cmp_pe.reshape(2, CMP_R, STRIDE_CMP, HEAD_DIM)
    w = cmp_w.reshape(2, CMP_R, STRIDE_CMP, HEAD_DIM, HEAD_DIM)
    eye_c, eye_g = jnp.eye(2, dtype=f32), jnp.eye(N_KV_HEADS, dtype=f32)
    wb = jnp.einsum('cisde,cx,gy->iscgdxye', w, eye_c, eye_g).reshape(CMP_R, CHUNK_COLS, KV_COLS)
    peb = jnp.broadcast_to(jnp.transpose(pe, (1, 2, 0, 3))[:, :, :, None, :],
                           (CMP_R, STRIDE_CMP, 2, N_KV_HEADS, HEAD_DIM)).reshape(CMP_R, 1, CHUNK_COLS)
    return wb.astype(bf16), peb


def _compress_kernel(x_ref, pe_ref, w_ref, kc_ref, vvc_ref):
    x = x_ref[0]
    n = x.shape[0]
    y0 = jnp.dot((x + pe_ref[0]).astype(bf16), w_ref[0], preferred_element_type=f32)
    y1 = jnp.dot((x + pe_ref[1]).astype(bf16), w_ref[1], preferred_element_type=f32)
    comp = y0 + pltpu.roll(y1, n - 1, 0)
    row = lax.broadcasted_iota(jnp.int32, comp.shape, 0)
    comp = jnp.where(row < n - 1, comp, 0.0)
    kc_ref[0] = comp[:, :D_KV].astype(bf16)
    vc = comp[:, D_KV:]
    vvc_ref[0, :, :D_KV] = vc.astype(bf16)
    vvc_ref[0, :, D_KV:] = pltpu.roll(vc, HEAD_DIM, 1).astype(bf16)


def compress(kv, wb, peb, batch):
    n_chunks = kv.shape[0] // batch // STRIDE_CMP
    x = kv.reshape(batch, n_chunks, CHUNK_COLS)
    return pl.pallas_call(
        _compress_kernel,
        grid=(batch,),
        in_specs=[pl.BlockSpec((1, n_chunks, CHUNK_COLS), lambda b: (b, 0, 0)),
                  pl.BlockSpec((CMP_R, 1, CHUNK_COLS), lambda b: (0, 0, 0)),
                  pl.BlockSpec((CMP_R, CHUNK_COLS, KV_COLS), lambda b: (0, 0, 0))],
        out_specs=[pl.BlockSpec((1, n_chunks, D_KV), lambda b: (b, 0, 0)),
                   pl.BlockSpec((1, n_chunks, KV_COLS), lambda b: (b, 0, 0))],
        out_shape=[jax.ShapeDtypeStruct((batch, n_chunks, D_KV), bf16),
                   jax.ShapeDtypeStruct((batch, n_chunks, KV_COLS), bf16)],
        compiler_params=pltpu.CompilerParams(
            dimension_semantics=("parallel",), vmem_limit_bytes=VMEM_LIMIT),
        name="compress",
    )(x, peb, wb)


KEY_CHUNK = 512
NT_DIMS = (((1,), (1,)), ((), ()))


def _softmax_rows(s3, mask):
    s3 = jnp.where(mask[None], s3, NEG)
    m = jnp.max(s3, axis=-1, keepdims=True)
    p = jnp.where(mask[None], jnp.exp(s3 - m), 0.0)
    l = jnp.sum(p, axis=-1, keepdims=True)
    return p * jnp.where(l > 0.0, 1.0 / l, 0.0)


def _nsa_prompt_kernel(q_ref, gate_ref, kc_ref, vvc_ref, ks_ref, vvs_ref, kw_ref, vvw_ref, smap_ref, e_ref,
                       o_ref, mask_ref, m_ref, l_ref, acc_ref, *, seq):
    n_slc = seq // BLOCK_SLC
    n_cmp = seq // STRIDE_CMP
    hq = HEADS_PER_KV * QBLOCK
    q0 = pl.program_id(1) * QBLOCK
    qpos_col = q0 + lax.broadcasted_iota(jnp.int32, (QBLOCK, 1), 0)

    heads_out = []
    for g in range(N_KV_HEADS):
        q = q_ref[0, g * HEADS_PER_KV:(g + 1) * HEADS_PER_KV].reshape(hq, LANES)

        s = lax.dot_general(q, kc_ref[0], NT_DIMS, preferred_element_type=f32)
        c_end = lax.broadcasted_iota(jnp.int32, (1, n_cmp), 1) * STRIDE_CMP + (BLOCK_CMP - 1)
        p_c = _softmax_rows(s.reshape(HEADS_PER_KV, QBLOCK, n_cmp), c_end <= qpos_col)
        p_c = p_c.astype(bf16).reshape(hq, n_cmp)
        o_c = jnp.dot(p_c, vvc_ref[0], preferred_element_type=f32)

        imp = jnp.zeros((n_slc, QBLOCK), f32)
        for h in range(HEADS_PER_KV):
            imp = imp + lax.dot_general(smap_ref[...], p_c[h * QBLOCK:(h + 1) * QBLOCK], NT_DIMS,
                                        preferred_element_type=f32)
        jj = lax.broadcasted_iota(jnp.int32, (n_slc, QBLOCK), 0)
        qp = q0 + lax.broadcasted_iota(jnp.int32, (n_slc, QBLOCK), 1)
        cur = qp // BLOCK_SLC
        forced = (jj == 0) | ((jj <= cur) & (jj > cur - N_LOCAL))
        imp = jnp.where(jj * BLOCK_SLC <= qp, jnp.where(forced, FORCED_SCORE, imp), NEG)
        rank = jnp.zeros((n_slc, QBLOCK), f32)
        for j in range(n_slc):
            row = imp[j:j + 1, :]
            rank = rank + jnp.where((row > imp) | ((row == imp) & (jj > j)), 1.0, 0.0)
        sel_t = jnp.where(rank < float(min(N_SELECT, n_slc)), 1.0, 0.0)
        if n_slc < LANES:
            sel_t = jnp.concatenate([sel_t, jnp.zeros((LANES - n_slc, QBLOCK), f32)], axis=0)
        sel = sel_t.T.astype(bf16)
        for c in range(seq // KEY_CHUNK):
            mask_ref[c] = jnp.dot(sel, e_ref[:, c * KEY_CHUNK:(c + 1) * KEY_CHUNK], preferred_element_type=f32)

        m_ref[...] = jnp.full(m_ref.shape, NEG, f32)
        l_ref[...] = jnp.zeros(l_ref.shape, f32)
        acc_ref[...] = jnp.zeros(acc_ref.shape, f32)

        def sel_step(c, carry):
            k0 = pl.multiple_of(c * KEY_CHUNK, KEY_CHUNK)
            sc = lax.dot_general(q, ks_ref[0, pl.ds(k0, KEY_CHUNK), :], NT_DIMS, preferred_element_type=f32)
            kpos = k0 + lax.broadcasted_iota(jnp.int32, (1, KEY_CHUNK), 1)
            msk = ((mask_ref[c] > 0.5) & (kpos <= qpos_col))[None]
            s3 = jnp.where(msk, sc.reshape(HEADS_PER_KV, QBLOCK, KEY_CHUNK), NEG)
            m_old = m_ref[...]
            m_new = jnp.maximum(m_old, jnp.max(s3, axis=-1, keepdims=True))
            alpha = jnp.exp(m_old - m_new)
            p = jnp.where(msk, jnp.exp(s3 - m_new), 0.0)
            l_ref[...] = alpha * l_ref[...] + jnp.sum(p, axis=-1, keepdims=True)
            pv = jnp.dot(p.astype(bf16).reshape(hq, KEY_CHUNK), vvs_ref[0, pl.ds(k0, KEY_CHUNK), :],
                         preferred_element_type=f32)
            acc_ref[...] = alpha * acc_ref[...] + pv.reshape(HEADS_PER_KV, QBLOCK, KV_COLS)
            m_ref[...] = m_new
            return carry

        lax.fori_loop(0, q0 // KEY_CHUNK + 1, sel_step, 0)
        o_s = (acc_ref[...] / l_ref[...]).reshape(hq, KV_COLS)

        n_win = WINDOW + QBLOCK
        w0 = pl.multiple_of(jnp.maximum(q0 - WINDOW, 0), QBLOCK)
        sw = lax.dot_general(q, kw_ref[0, pl.ds(w0, n_win), :], NT_DIMS, preferred_element_type=f32)
        dist = qpos_col - (w0 + lax.broadcasted_iota(jnp.int32, (1, n_win), 1))
        p_w = _softmax_rows(sw.reshape(HEADS_PER_KV, QBLOCK, n_win), (dist >= 0) & (dist < WINDOW))
        o_w = jnp.dot(p_w.astype(bf16).reshape(hq, n_win), vvw_ref[0, pl.ds(w0, n_win), :],
                      preferred_element_type=f32)

        for hh in range(HEADS_PER_KV):
            h = g * HEADS_PER_KV + hh
            half = 0 if (h % 2) == g else 1
            rows = slice(hh * QBLOCK, (hh + 1) * QBLOCK)
            cols = slice(half * LANES, (half + 1) * LANES)
            gc = gate_ref[:, N_BRANCH * h:N_BRANCH * h + 1]
            gs = gate_ref[:, N_BRANCH * h + 1:N_BRANCH * h + 2]
            gw = gate_ref[:, N_BRANCH * h + 2:N_BRANCH * h + 3]
            heads_out.append(gc * o_c[rows, cols] + gs * o_s[rows, cols] + gw * o_w[rows, cols])

    lane = lax.broadcasted_iota(jnp.int32, (QBLOCK, LANES), 1)
    for pair in range(N_HEADS_ATTN // 2):
        o_ref[0, :, pair * LANES:(pair + 1) * LANES] = jnp.where(
            lane < HEAD_DIM, heads_out[2 * pair], heads_out[2 * pair + 1])


def nsa_tables(seq):
    n_slc, n_cmp = seq // BLOCK_SLC, seq // STRIDE_CMP
    start = np.arange(n_cmp) * STRIDE_CMP
    first, last = start // BLOCK_SLC, (start + BLOCK_CMP - 1) // BLOCK_SLC
    j = np.arange(n_slc)
    smap = (first[None, :] <= j[:, None]) & (last[None, :] >= j[:, None]) & (np.arange(n_cmp)[None, :] < n_cmp - 1)
    expand = (np.arange(seq)[None, :] // BLOCK_SLC) == np.arange(max(n_slc, LANES))[:, None]
    return jnp.asarray(smap, bf16), jnp.asarray(expand, bf16)


def nsa_prompt(q_ext, gates, kc, vvc, kb, vvb, tables, batch, seq):
    smap, expand = tables
    n_cmp = seq // STRIDE_CMP
    hq = HEADS_PER_KV * QBLOCK
    per_b = lambda b, i: (b, 0, 0)
    const = lambda b, i: (0, 0)
    gates3 = gates.reshape(batch, seq, LANES)
    kb4 = kb.reshape(2, batch, seq, D_KV)
    vvb4 = vvb.reshape(2, batch, seq, KV_COLS)
    branch = lambda br: (lambda b, i: (br, b, 0, 0))
    return pl.pallas_call(
        functools.partial(_nsa_prompt_kernel, seq=seq),
        grid=(batch, seq // QBLOCK),
        in_specs=[pl.BlockSpec((1, N_HEADS_ATTN, QBLOCK, LANES), lambda b, i: (b, 0, i, 0)),
                  pl.BlockSpec((None, QBLOCK, LANES), lambda b, i: (b, i, 0)),
                  pl.BlockSpec((1, n_cmp, D_KV), per_b), pl.BlockSpec((1, n_cmp, KV_COLS), per_b),
                  pl.BlockSpec((None, 1, seq, D_KV), branch(0)), pl.BlockSpec((None, 1, seq, KV_COLS), branch(0)),
                  pl.BlockSpec((None, 1, seq, D_KV), branch(1)), pl.BlockSpec((None, 1, seq, KV_COLS), branch(1)),
                  pl.BlockSpec(smap.shape, const), pl.BlockSpec(expand.shape, const)],
        out_specs=pl.BlockSpec((1, QBLOCK, D_ATTN), lambda b, i: (b, i, 0)),
        out_shape=jax.ShapeDtypeStruct((batch, seq, D_ATTN), f32),
        scratch_shapes=[pltpu.VMEM((seq // KEY_CHUNK, QBLOCK, KEY_CHUNK), f32),
                        pltpu.VMEM((HEADS_PER_KV, QBLOCK, 1), f32),
                        pltpu.VMEM((HEADS_PER_KV, QBLOCK, 1), f32),
                        pltpu.VMEM((HEADS_PER_KV, QBLOCK, KV_COLS), f32)],
        compiler_params=pltpu.CompilerParams(
            dimension_semantics=("parallel", "arbitrary"), vmem_limit_bytes=VMEM_LIMIT),
        name="nsa_prompt",
    )(q_ext, gates3, kc, vvc, kb4, vvb4, kb4, vvb4, smap, expand)


RW_CHUNK = 64
N_PAIRS = N_HEADS_RWKV // 2
TN_DIMS = (((0,), (0,)), ((), ()))


def _mm(a, b):
    return jnp.dot(a.astype(bf16), b.astype(bf16), preferred_element_type=f32)


def _mm_nt(a, b):
    return lax.dot_general(a.astype(bf16), b.astype(bf16), NT_DIMS, preferred_element_type=f32)


def _mm_tn(a, b):
    return lax.dot_general(a.astype(bf16), b.astype(bf16), TN_DIMS, preferred_element_type=f32)


def _mm_split(a01, x):
    hi = x.astype(bf16)
    lo = (x - hi.astype(f32)).astype(bf16)
    return jnp.dot(a01, hi, preferred_element_type=f32) + jnp.dot(a01, lo, preferred_element_type=f32)


def _seg_sum(x, p_ref):
    hi = x.astype(bf16)
    lo = (x - hi.astype(f32)).astype(bf16)
    return (jnp.dot(hi, p_ref[...], preferred_element_type=f32)
            + jnp.dot(lo, p_ref[...], preferred_element_type=f32))


def _stack_heads(x):
    lane = lax.broadcasted_iota(jnp.int32, x.shape, 1)
    return jnp.concatenate([jnp.where(lane < HEAD_DIM, x, 0.0), jnp.where(lane >= HEAD_DIM, x, 0.0)], axis=0)


def _chunk_pair(r, cum, lw, alpha, beta, k, v, st):
    c = r.shape[0]
    eg, egi, egp = jnp.exp(cum), jnp.exp(-cum), jnp.exp(cum - lw)
    g_end = eg[c - 1:c, :]
    x_a, x_r = _stack_heads(alpha * egp), _stack_heads(r * eg)
    bt, kt = beta * egi, k * egi
    b_b, b_k = jnp.concatenate([bt, bt], axis=0), jnp.concatenate([kt, kt], axis=0)
    x_bg, x_kg, v_s = _stack_heads(bt * g_end), _stack_heads(kt * g_end), _stack_heads(v)
    row = lax.broadcasted_iota(jnp.int32, (2 * c, 2 * c), 0)
    col = lax.broadcasted_iota(jnp.int32, (2 * c, 2 * c), 1)
    same = (row // c) == (col // c)
    strict = same & (col < row)
    incl = same & (col <= row)
    n = jnp.where(strict, _mm_nt(x_a, b_b), 0.0)
    a_ak = jnp.where(strict, _mm_nt(x_a, b_k), 0.0)
    a_rb = jnp.where(incl, _mm_nt(x_r, b_b), 0.0)
    a_rk = jnp.where(incl, _mm_nt(x_r, b_k), 0.0)
    def level(k):
        return jnp.where(((row >> (k + 1)) == (col >> (k + 1))) & ((row >> k) != (col >> k)), n, 0.0)
    inv = jnp.where(row == col, 1.0, 0.0) + level(0)
    for k in range(1, int(np.log2(c))):
        inv = inv + _mm(_mm(inv, level(k)), inv)
    st_b = st.astype(bf16)
    p = _mm(x_a, st_b) + _mm(a_ak, v_s)
    e = _mm(inv, p)
    y_s = _mm(x_r, st_b) + _mm(a_rb, e) + _mm(a_rk, v_s)
    g_col = jnp.broadcast_to(g_end, (2 * c, 2 * c)).T
    st_new = st * g_col + _mm_tn(x_bg, e) + _mm_tn(x_kg, v_s)
    return y_s[:c] + y_s[c:], st_new


def _rwkv_prompt_kernel(*refs, first_layer):
    zr_ref, zk_ref, zv_ref, zl_ref, sp_ref, mu_ref, vec_ref, lr_ref, p_ref, tri_ref = refs[:10]
    if first_layer:
        o_ref, st_ref, vfo_ref, prev_ref = refs[10:]
    else:
        vf_ref, o_ref, st_ref, prev_ref = refs[10:]
    c = RW_CHUNK
    ci = pl.program_id(1)

    @pl.when(ci == 0)
    def _():
        prev_ref[0:1, :] = sp_ref[...]
        st_ref[...] = jnp.zeros(st_ref.shape, f32)

    def shifted(z_ref, lo, hi):
        z = z_ref[...]
        first = lax.broadcasted_iota(jnp.int32, z.shape, 0) == 0
        prev = jnp.where(first, prev_ref[0:1, lo:hi], pltpu.roll(z, 1, 0))
        xs = z + (prev - z) * mu_ref[:, lo:hi]
        prev_ref[0:1, lo:hi] = z[c - 1:c, :]
        return xs

    xr = shifted(zr_ref, 0, D_RWKV)
    xk = shifted(zk_ref, D_RWKV, 2 * D_RWKV)
    xv = shifted(zv_ref, 2 * D_RWKV, 3 * D_RWKV)
    xl = shifted(zl_ref, 3 * D_RWKV, SHIFT_COLS)
    w0, a0, vres0, k_k = vec_ref[0:1, :], vec_ref[1:2, :], vec_ref[2:3, :], vec_ref[3:4, :]
    k_a, r_k, lnx_w, lnx_b = vec_ref[4:5, :], vec_ref[5:6, :], vec_ref[6:7, :], vec_ref[7:8, :]

    xl_b = xl.astype(bf16)
    u = -(w0 + jnp.dot(jnp.tanh(xl).astype(bf16), lr_ref[0], preferred_element_type=f32))
    softplus = jnp.maximum(u, 0.0) + jnp.log(1.0 + jnp.exp(-jnp.abs(u)))
    lw = -jnp.exp(-softplus - 0.5)
    a = jax.nn.sigmoid(a0 + jnp.dot(xl_b, lr_ref[1], preferred_element_type=f32))
    if first_layer:
        v = xv
        vfo_ref[...] = xv
    else:
        v = xv + (vf_ref[...] - xv) * jax.nn.sigmoid(vres0 + jnp.dot(xl_b, lr_ref[2], preferred_element_type=f32))
    gate = jnp.dot(jax.nn.sigmoid(xl).astype(bf16), lr_ref[3], preferred_element_type=f32)
    kk = xk * k_k
    k2 = xk * (1.0 + (a - 1.0) * k_a)
    cum = _mm_split(tri_ref[...], lw)

    for pair in range(N_PAIRS):
        cols = slice(pair * LANES, (pair + 1) * LANES)
        kk_p = kk[:, cols]
        kk_p = kk_p * lax.rsqrt(jnp.maximum(_seg_sum(kk_p * kk_p, p_ref), 1e-24))
        r_p, k_p, v_p = xr[:, cols], k2[:, cols], v[:, cols]
        y, st_new = _chunk_pair(r_p, cum[:, cols], lw[:, cols], -kk_p, kk_p * a[:, cols], k_p, v_p, st_ref[pair])
        st_ref[pair] = st_new
        mean = _seg_sum(y, p_ref) * (1.0 / HEAD_DIM)
        d = y - mean
        var = _seg_sum(d * d, p_ref) * (1.0 / HEAD_DIM)
        yn = d * lax.rsqrt(var + GN_EPS) * lnx_w[:, cols] + lnx_b[:, cols]
        bonus = _seg_sum(r_p * k_p * r_k[:, cols], p_ref) * v_p
        o_ref[:, cols] = (yn + bonus) * gate[:, cols]


def rwkv_prompt(z, v_first, lp, consts, batch, seq):
    m = batch * seq
    c = RW_CHUNK
    n_chunks = seq // c
    first_layer = v_first is None
    rows = lambda j: (lambda b, i: (b * n_chunks + i, j))
    const2 = lambda b, i: (0, 0)
    tok = pl.BlockSpec((c, D_RWKV), rows(0))
    tok_shape = jax.ShapeDtypeStruct((m, D_RWKV), f32)
    out = pl.pallas_call(
        functools.partial(_rwkv_prompt_kernel, first_layer=first_layer),
        grid=(batch, n_chunks),
        in_specs=[pl.BlockSpec((c, D_RWKV), rows(Z_R // D_RWKV)),
                  pl.BlockSpec((c, D_RWKV), rows(Z_K // D_RWKV)),
                  pl.BlockSpec((c, D_RWKV), rows(Z_V // D_RWKV)),
                  pl.BlockSpec((c, LANES), rows(Z_LR // LANES)),
                  pl.BlockSpec((None, 1, SHIFT_COLS), lambda b, i: (b, 0, 0)),
                  pl.BlockSpec((1, SHIFT_COLS), const2),
                  pl.BlockSpec((8, D_RWKV), const2),
                  pl.BlockSpec((4, LANES, D_RWKV), lambda b, i: (0, 0, 0)),
                  pl.BlockSpec((LANES, LANES), const2),
                  pl.BlockSpec((c, c), const2)] + ([] if first_layer else [tok]),
        out_specs=[tok, pl.BlockSpec((None, N_PAIRS, LANES, LANES), lambda b, i: (b, 0, 0, 0))]
                  + ([tok] if first_layer else []),
        out_shape=[tok_shape, jax.ShapeDtypeStruct((batch, N_PAIRS, LANES, LANES), f32)]
                  + ([tok_shape] if first_layer else []),
        scratch_shapes=[pltpu.VMEM((8, SHIFT_COLS), f32)],
        compiler_params=pltpu.CompilerParams(
            dimension_semantics=("parallel", "arbitrary"), vmem_limit_bytes=VMEM_LIMIT),
        name="rwkv_prompt",
    )(z, z, z, z, consts['shift_zero'], lp['mu_row'], lp['rwkv_vecs'], lp['rwkv_lowrank'],
      consts['seg_ones'], consts['tri'], *([] if first_layer else [v_first]))
    return out[0], out[1], (out[2] if first_layer else v_first)


def state_from_pair_tiles(st):
    b = st.shape[0]
    t = st.reshape(b, N_PAIRS, 2, HEAD_DIM, 2, HEAD_DIM)
    diag = jnp.stack([t[:, :, 0, :, 0, :], t[:, :, 1, :, 1, :]], axis=2)
    return jnp.swapaxes(diag, -1, -2).reshape(b, N_HEADS_RWKV, HEAD_DIM, HEAD_DIM)


def rwkv_params(lp_raw, layer):
    zrow = jnp.zeros((D_RWKV,), f32)
    vres0 = lp_raw['vres0'][layer - 1] if layer > 0 else zrow
    vecs = jnp.stack([lp_raw['w0'][layer], lp_raw['a0'][layer], vres0, lp_raw['k_k'][layer], lp_raw['k_a'][layer],
                      lp_raw['r_k'][layer].reshape(D_RWKV), lp_raw['lnx_w'][layer], lp_raw['lnx_b'][layer]])
    def rows(w, lo):
        return jnp.zeros((LANES, D_RWKV), f32).at[lo:lo + w.shape[0]].set(w)
    vres2 = lp_raw['vres2'][layer - 1] if layer > 0 else jnp.zeros((RANK_A, D_RWKV), f32)
    lowrank = jnp.stack([rows(lp_raw['w2'][layer], 0), rows(lp_raw['a2'][layer], RANK_W),
                         rows(vres2, RANK_W), rows(lp_raw['g2'][layer], RANK_W + RANK_A)]).astype(bf16)
    return vecs, lowrank


def _split(x, sizes):
    offs = [int(o) for o in np.cumsum(sizes)[:-1]]
    return jnp.split(x, offs, axis=-1)


def rms_norm(x, g):
    xf = x.astype(f32)
    y = xf * lax.rsqrt(jnp.mean(xf * xf, axis=-1, keepdims=True) + NORM_EPS)
    return (y * g.astype(f32)).astype(x.dtype)


def rope_tables(pos):
    inv = ROPE_THETA ** (-(jnp.arange(0, ROT_DIM, 2, dtype=f32) / ROT_DIM))
    ang = pos.astype(f32)[:, None] * inv[None, :]
    return jnp.cos(ang), jnp.sin(ang)


def apply_partial_rope(x, cos, sin):
    xr = x[..., :ROT_DIM].astype(f32)
    x1, x2 = xr[..., :ROT_DIM // 2], xr[..., ROT_DIM // 2:]
    c, s = cos[None, :, None, :], sin[None, :, None, :]
    rot = jnp.concatenate([x1 * c - x2 * s, x2 * c + x1 * s], axis=-1).astype(x.dtype)
    return jnp.concatenate([rot, x[..., ROT_DIM:]], axis=-1)


def masked_softmax(s, mask):
    p = jax.nn.softmax(jnp.where(mask, s, NEG), axis=-1)
    return jnp.where(mask, p, 0.0)


def gather_pages(pool, page_table):
    g = pool[page_table]
    return g.reshape(page_table.shape[0], -1, 2, N_KV_HEADS, HEAD_DIM)


def gather_blocks(blocks, idx):
    return jax.vmap(jax.vmap(lambda b, i: b[i]))(blocks, idx)


def nsa_heads(z_attn, pos, q_norm_g, k_norm_g):
    B, T = z_attn.shape[:2]
    parts = _split(z_attn, [D_ATTN] + [D_KV] * 6)
    cos, sin = rope_tables(pos)
    q = apply_partial_rope(rms_norm(parts[0].reshape(B, T, N_HEADS_ATTN, HEAD_DIM), q_norm_g), cos, sin)
    kvs = []
    for br in range(N_BRANCH):
        k = parts[1 + 2 * br].reshape(B, T, N_KV_HEADS, HEAD_DIM)
        v = parts[2 + 2 * br].reshape(B, T, N_KV_HEADS, HEAD_DIM)
        k = apply_partial_rope(rms_norm(k, k_norm_g[br]), cos, sin)
        kvs.append(jnp.stack([k, v], axis=2))
    return q, kvs


def compress_blocks(kv, cmp_pe, cmp_w):
    B, L = kv.shape[:2]
    n_cmp = (L - BLOCK_CMP) // STRIDE_CMP + 1
    r = BLOCK_CMP // STRIDE_CMP
    n_chunks = n_cmp + r - 1
    chunks = kv[:, :n_chunks * STRIDE_CMP].reshape(B, n_chunks, STRIDE_CMP, 2, N_KV_HEADS, HEAD_DIM)
    pe = cmp_pe.reshape(2, r, STRIDE_CMP, HEAD_DIM)
    w = cmp_w.reshape(2, r, STRIDE_CMP, HEAD_DIM, HEAD_DIM)
    comp = 0.0
    for i in range(r):
        part = chunks[:, i:i + n_cmp] + jnp.moveaxis(pe[:, i], 0, 1)[None, None, :, :, None, :]
        comp = comp + jnp.einsum('bnscgd,csde->bncge', part, w[:, i])
    c_end = jnp.arange(n_cmp, dtype=jnp.int32) * STRIDE_CMP + BLOCK_CMP - 1
    return comp[:, :, 0], comp[:, :, 1], c_end


def cmp_to_slc_map(n_cmp, n_slc):
    start = np.arange(n_cmp) * STRIDE_CMP
    first, last = start // BLOCK_SLC, (start + BLOCK_CMP - 1) // BLOCK_SLC
    j = np.arange(n_slc)
    return jnp.asarray(((first[:, None] <= j[None, :]) & (last[:, None] >= j[None, :])).astype(np.float32))


def to_slc_blocks(kv):
    B, L = kv.shape[:2]
    n_slc = -(-L // BLOCK_SLC)
    kv = jnp.pad(kv, ((0, 0), (0, n_slc * BLOCK_SLC - L), (0, 0), (0, 0), (0, 0)))
    return kv.reshape(B, n_slc, BLOCK_SLC, 2, N_KV_HEADS, HEAD_DIM).transpose(0, 4, 1, 2, 3, 5)


def nsa_query_block(q, q_pos, kc, vc, c_end, sel_map, slc_blocks, win_kv, w_pos, gates):
    B, Tq = q.shape[:2]
    qg = q.astype(f32).reshape(B, Tq, N_KV_HEADS, HEADS_PER_KV, HEAD_DIM) * (HEAD_DIM ** -0.5)
    s_c = jnp.einsum('bqghd,bngd->bghqn', qg, kc.astype(f32))
    p_c = masked_softmax(s_c, c_end[None, :] <= q_pos[:, None])
    o_c = jnp.einsum('bghqn,bngd->bqghd', p_c, vc.astype(f32))
    n_slc = slc_blocks.shape[2]
    imp = jnp.einsum('bghqn,ns->bgqs', p_c, sel_map)
    j = jnp.arange(n_slc, dtype=jnp.int32)[None, :]
    cur = (q_pos // BLOCK_SLC)[:, None]
    forced = (j == 0) | ((j <= cur) & (j > cur - N_LOCAL))
    causal = j * BLOCK_SLC <= q_pos[:, None]
    imp = jnp.where(causal, jnp.where(forced, FORCED_SCORE, imp), NEG)
    n_sel = min(N_SELECT, n_slc)
    _, idx = lax.top_k(imp, n_sel)
    g_kv = gather_blocks(slc_blocks, idx.reshape(B, N_KV_HEADS, Tq * n_sel))
    g_kv = g_kv.reshape(B, N_KV_HEADS, Tq, n_sel * BLOCK_SLC, 2, HEAD_DIM).astype(f32)
    k_pos = (idx[..., None] * BLOCK_SLC + jnp.arange(BLOCK_SLC, dtype=jnp.int32)).reshape(B, N_KV_HEADS, Tq, n_sel * BLOCK_SLC)
    s_s = jnp.einsum('bqghd,bgqkd->bghqk', qg, g_kv[..., 0, :])
    p_s = masked_softmax(s_s, (k_pos <= q_pos[:, None])[:, :, None])
    o_s = jnp.einsum('bghqk,bgqkd->bqghd', p_s, g_kv[..., 1, :])
    s_w = jnp.einsum('bqghd,bkgd->bghqk', qg, win_kv[:, :, 0].astype(f32))
    dist = q_pos[:, None] - w_pos[None, :]
    p_w = masked_softmax(s_w, (dist >= 0) & (dist < WINDOW) & (w_pos[None, :] >= 0))
    o_w = jnp.einsum('bghqk,bkgd->bqghd', p_w, win_kv[:, :, 1].astype(f32))
    g = gates.astype(f32).reshape(B, Tq, N_KV_HEADS, HEADS_PER_KV, N_BRANCH)
    o = g[..., 0:1] * o_c + g[..., 1:2] * o_s + g[..., 2:3] * o_w
    return o.reshape(B, Tq, D_ATTN).astype(q.dtype)


def prompt_attention(q, kvs, gates, lp):
    B, T = q.shape[:2]
    kc, vc, c_end = compress_blocks(kvs[0], lp['cmp_pe'], lp['cmp_w'])
    slc_blocks = to_slc_blocks(kvs[1])
    sel_map = cmp_to_slc_map(kc.shape[1], slc_blocks.shape[2])
    win_pad = jnp.pad(kvs[2], ((0, 0), (WINDOW, 0), (0, 0), (0, 0), (0, 0)))

    def one_query_block(q0):
        q_pos = q0 + jnp.arange(QBLOCK, dtype=jnp.int32)
        w_pos = q0 - WINDOW + jnp.arange(WINDOW + QBLOCK, dtype=jnp.int32)
        return nsa_query_block(
            lax.dynamic_slice_in_dim(q, q0, QBLOCK, axis=1), q_pos, kc, vc, c_end, sel_map, slc_blocks,
            lax.dynamic_slice_in_dim(win_pad, q0, WINDOW + QBLOCK, axis=1), w_pos,
            lax.dynamic_slice_in_dim(gates, q0, QBLOCK, axis=1))

    out = lax.map(one_query_block, jnp.arange(0, T, QBLOCK, dtype=jnp.int32))
    out = jnp.moveaxis(out, 0, 1).reshape(B, T, D_ATTN)
    cmp_pages = kvs[0].reshape(B, T // PAGE_SIZE, PAGE_SIZE, 2, N_KV_HEADS, HEAD_DIM)
    slc_pages = kvs[1].reshape(B, T // PAGE_SIZE, PAGE_SIZE, 2, N_KV_HEADS, HEAD_DIM)
    return out, (cmp_pages, slc_pages, kvs[2][:, -min(WINDOW, T):])


def sample_attention(q, kvs, gates, lp, cache_cmp, cache_slc, cache_win, page_table):
    Tn = q.shape[1]
    past = page_table.shape[1] * PAGE_SIZE
    w_buf = cache_win.shape[1]
    full_cmp = jnp.concatenate([gather_pages(cache_cmp, page_table), kvs[0]], axis=1)
    full_slc = jnp.concatenate([gather_pages(cache_slc, page_table), kvs[1]], axis=1)
    win = jnp.concatenate([cache_win, kvs[2]], axis=1)
    kc, vc, c_end = compress_blocks(full_cmp, lp['cmp_pe'], lp['cmp_w'])
    slc_blocks = to_slc_blocks(full_slc)
    sel_map = cmp_to_slc_map(kc.shape[1], slc_blocks.shape[2])
    q_pos = past + jnp.arange(Tn, dtype=jnp.int32)
    w_pos = past - w_buf + jnp.arange(w_buf + Tn, dtype=jnp.int32)
    out = nsa_query_block(q, q_pos, kc, vc, c_end, sel_map, slc_blocks, win, w_pos, gates)
    return out, (kvs[0], kvs[1], win[:, -w_buf:])


def rwkv7_time_mix(zs, shift_prev, wkv0, v_first, lp):
    B, T = zs.shape[:2]
    zf = zs.astype(f32)
    prev = jnp.concatenate([shift_prev.astype(f32)[:, None], zf[:, :-1]], axis=1)
    xs = zf + (prev - zf) * lp['mu']
    r, k, v, xw, xa, xg = _split(xs, [D_RWKV] * 3 + [RANK_W, RANK_A, RANK_G])
    w_raw = -jax.nn.softplus(-(lp['w0'] + jnp.tanh(xw) @ lp['w2'])) - 0.5
    decay = jnp.exp(-jnp.exp(w_raw))
    a = jax.nn.sigmoid(lp['a0'] + xa @ lp['a2'])
    if lp['vres'] is None:
        v_first = v
    else:
        vres0, vres2 = lp['vres']
        v = v + (v_first - v) * jax.nn.sigmoid(vres0 + xa @ vres2)
    g = jax.nn.sigmoid(xg) @ lp['g2']
    heads = lambda t: t.reshape(B, T, N_HEADS_RWKV, HEAD_DIM)
    kk = heads(k * lp['k_k'])
    kk = kk * lax.rsqrt(jnp.maximum(jnp.sum(kk * kk, axis=-1, keepdims=True), 1e-24))
    k = k * (1.0 + (a - 1.0) * lp['k_a'])
    r_h, k_h, v_h, a_h, d_h = heads(r), heads(k), heads(v), heads(a), heads(decay)

    def step(S, inp):
        r_t, d_t, kk_t, a_t, k_t, v_t = inp
        sa = jnp.einsum('bhvk,bhk->bhv', S, -kk_t)
        S = S * d_t[:, :, None, :] + sa[..., None] * (kk_t * a_t)[:, :, None, :] + v_t[..., None] * k_t[:, :, None, :]
        return S, jnp.einsum('bhvk,bhk->bhv', S, r_t)

    tm = lambda t: jnp.moveaxis(t, 1, 0)
    S_T, ys = lax.scan(step, wkv0.astype(f32), (tm(r_h), tm(d_h), tm(kk), tm(a_h), tm(k_h), tm(v_h)))
    y = jnp.moveaxis(ys, 0, 1)
    mean = jnp.mean(y, axis=-1, keepdims=True)
    var = jnp.mean(jnp.square(y - mean), axis=-1, keepdims=True)
    y = (y - mean) * lax.rsqrt(var + GN_EPS) * lp['lnx_w'].reshape(N_HEADS_RWKV, HEAD_DIM) + lp['lnx_b'].reshape(N_HEADS_RWKV, HEAD_DIM)
    y = y + jnp.sum(r_h * k_h * lp['r_k'], axis=-1, keepdims=True) * v_h
    y = y.reshape(B, T, D_RWKV) * g
    return y, S_T, zs[:, -1], v_first


def trunk_layer(x, pos, attn_fn, shift_prev, wkv0, v_first, lp):
    B, T = x.shape[:2]
    x2 = x.reshape(B * T, D_MODEL)
    z = norm_proj(x2, lp['norm1_g'], lp['w_in']).reshape(B, T, IN_COLS_PAD)
    z_attn, z_gate, z_rwkv = split_z(z)
    q, kvs = nsa_heads(z_attn, pos, lp['q_norm_g'], lp['k_norm_g'])
    gates = jax.nn.sigmoid((z_gate + lp['gate_b']).astype(f32))
    o_attn, attn_state = attn_fn(q, kvs, gates, lp)
    o_rwkv, wkv_t, shift_t, v_first = rwkv7_time_mix(z_rwkv, shift_prev, wkv0, v_first, lp)
    h = out_proj(x2, o_attn.reshape(B * T, D_ATTN), o_rwkv.reshape(B * T, D_RWKV), lp['w_out'])
    y = ffn(h, lp['norm2_g'], lp['w_up'], lp['w_down']).reshape(B, T, D_MODEL)
    return y, attn_state, wkv_t, shift_t, v_first


def prompt_trunk_layer(x, v_first, lp, consts):
    B, T = x.shape[:2]
    x2 = x.reshape(B * T, D_MODEL)
    z = norm_proj(x2, lp['norm1_g'], lp['w_in'])
    q_ext, kv_c, kv_s, kv_w, kb, vvb, gates = attn_prep(
        z, consts['rope'], consts['seg_ones'], lp['q_g_lanes'], lp['k_g_lanes'], lp['gate_b_lanes'], B, T)
    kc, vvc = compress(kv_c, lp['cmp_wb'], lp['cmp_peb'], B)
    o_attn = nsa_prompt(q_ext, gates, kc, vvc, kb, vvb, consts['nsa'], B, T)
    pages = lambda kv: kv.reshape(B, T // PAGE_SIZE, PAGE_SIZE, 2, N_KV_HEADS, HEAD_DIM)
    attn_state = (pages(kv_c), pages(kv_s),
                  kv_w.reshape(B, T, 2, N_KV_HEADS, HEAD_DIM)[:, -min(WINDOW, T):])
    o_rwkv, st, v_first = rwkv_prompt(z, v_first, lp, consts, B, T)
    z_last = z.reshape(B, T, IN_COLS_PAD)[:, -1]
    shift_t = jnp.concatenate([z_last[:, Z_R:Z_KV], z_last[:, Z_LR:]], axis=-1)
    h = out_proj(x2, o_attn.reshape(B * T, D_ATTN), o_rwkv, lp['w_out'])
    y = ffn(h, lp['norm2_g'], lp['w_up'], lp['w_down']).reshape(B, T, D_MODEL)
    return y, attn_state, state_from_pair_tiles(st), shift_t, v_first


def kernel(x_prompt, x_sample, cache_cmp_kv, cache_slc_kv, cache_win_kv, state_wkv, state_shift, page_table,
           norm1_g, w_in, q_norm_g, k_norm_g, gate_b, cmp_pe, cmp_w, shift_mu, w0, w2, a0, a2, vres0, vres2,
           g2, k_k, k_a, r_k, lnx_w, lnx_b, w_out, norm2_g, w_up, w_down):
    B, T = x_prompt.shape[:2]
    Tn = x_sample.shape[1]
    past = page_table.shape[1] * PAGE_SIZE
    pos_p = jnp.arange(T, dtype=jnp.int32)
    pos_s = past + jnp.arange(Tn, dtype=jnp.int32)
    w_in_b = permute_w_in(w_in).astype(bf16)
    w_out_b, w_up_b, w_down_b = w_out.astype(bf16), w_up.astype(bf16), w_down.astype(bf16)
    seg = np.arange(LANES) // HEAD_DIM
    consts = {'rope': rope_lane_tables(pos_p), 'nsa': nsa_tables(T),
              'seg_ones': jnp.asarray(seg[:, None] == seg[None, :], bf16),
              'tri': jnp.asarray(np.tril(np.ones((RW_CHUNK, RW_CHUNK))), bf16),
              'shift_zero': jnp.zeros((B, 1, SHIFT_COLS), f32)}
    rwkv_raw = {'w0': w0, 'w2': w2, 'a0': a0, 'a2': a2, 'vres0': vres0, 'vres2': vres2, 'g2': g2, 'k_k': k_k,
                'k_a': k_a, 'r_k': r_k, 'lnx_w': lnx_w, 'lnx_b': lnx_b}
    xp, xs = x_prompt, x_sample
    vf_p, vf_s = None, None
    p_cmp, p_slc, p_win, p_wkv, p_shift = [], [], [], [], []
    s_cmp, s_slc, s_win, s_wkv, s_shift = [], [], [], [], []
    for l in range(DEPTH):
        lp = {'norm1_g': norm1_g[l], 'w_in': w_in_b[l], 'q_norm_g': q_norm_g[l], 'k_norm_g': k_norm_g[l],
              'gate_b': gate_b[l], 'cmp_pe': cmp_pe[l], 'cmp_w': cmp_w[l], 'mu': shift_mu[l], 'w0': w0[l],
              'w2': w2[l], 'a0': a0[l], 'a2': a2[l],
              'vres': None if l == 0 else (vres0[l - 1], vres2[l - 1]),
              'g2': g2[l], 'k_k': k_k[l], 'k_a': k_a[l], 'r_k': r_k[l], 'lnx_w': lnx_w[l], 'lnx_b': lnx_b[l],
              'w_out': w_out_b[l], 'norm2_g': norm2_g[l], 'w_up': w_up_b[l], 'w_down': w_down_b[l]}
        lp['cmp_wb'], lp['cmp_peb'] = compress_weights(cmp_pe[l], cmp_w[l])
        lp['q_g_lanes'] = jnp.tile(q_norm_g[l], 2).reshape(1, LANES)
        lp['k_g_lanes'] = jnp.tile(k_norm_g[l], (1, 2))
        lp['gate_b_lanes'] = jnp.pad(gate_b[l], (0, LANES - N_BRANCH * N_HEADS_ATTN)).reshape(1, LANES)
        lp['mu_row'] = shift_mu[l].reshape(1, SHIFT_COLS)
        lp['rwkv_vecs'], lp['rwkv_lowrank'] = rwkv_params(rwkv_raw, l)
        xp, (c_kv, sl_kv, w_kv), wkv_t, sh_t, vf_p = prompt_trunk_layer(xp, vf_p, lp, consts)
        p_cmp.append(c_kv); p_slc.append(sl_kv); p_win.append(w_kv); p_wkv.append(wkv_t); p_shift.append(sh_t)
        sample_fn = functools.partial(sample_attention, cache_cmp=cache_cmp_kv[l], cache_slc=cache_slc_kv[l],
                                      cache_win=cache_win_kv[l], page_table=page_table)
        xs, (c_kv, sl_kv, w_kv), wkv_t, sh_t, vf_s = trunk_layer(
            xs, pos_s, sample_fn, state_shift[l], state_wkv[l], vf_s, lp)
        s_cmp.append(c_kv); s_slc.append(sl_kv); s_win.append(w_kv); s_wkv.append(wkv_t); s_shift.append(sh_t)
    return (xp, xs, jnp.stack(p_cmp), jnp.stack(p_slc), jnp.stack(p_win), jnp.stack(p_wkv), jnp.stack(p_shift),
            jnp.stack(s_cmp), jnp.stack(s_slc), jnp.stack(s_win), jnp.stack(s_wkv), jnp.stack(s_shift))
```

```python
import functools
import jax, jax.numpy as jnp
from jax import lax
import numpy as np
from jax.experimental import pallas as pl
from jax.experimental.pallas import tpu as pltpu

D_MODEL = 1024
DEPTH = 4
PAGE_SIZE = 128
HEAD_DIM = 64
N_HEADS_ATTN = 8
N_KV_HEADS = 2
HEADS_PER_KV = N_HEADS_ATTN // N_KV_HEADS
D_ATTN = N_HEADS_ATTN * HEAD_DIM
D_KV = N_KV_HEADS * HEAD_DIM
N_BRANCH = 3
N_HEADS_RWKV = 8
D_RWKV = N_HEADS_RWKV * HEAD_DIM
ROT_DIM = HEAD_DIM // 4
ROPE_THETA = 500000.0
BLOCK_CMP = 32
STRIDE_CMP = 16
BLOCK_SLC = 64
N_SELECT = 16
N_LOCAL = 2
WINDOW = 512
QBLOCK = 128
RANK_W = 32
RANK_A = 32
RANK_G = 64
D_FF = 4 * D_MODEL
SHIFT_COLS = 3 * D_RWKV + RANK_W + RANK_A + RANK_G
IN_COLS = D_ATTN + 6 * D_KV + N_BRANCH * N_HEADS_ATTN + SHIFT_COLS
IN_COLS_PAD = 3072
NORM_EPS = 1e-6
GN_EPS = 64e-5
NEG = -1e30
FORCED_SCORE = 1e9

VMEM_LIMIT = 48 * 1024 * 1024
bf16 = jnp.bfloat16
f32 = jnp.float32


def _row_tile(m):
    return min(m, 1024)


def _norm_proj_kernel(x_ref, g_ref, w_ref, o_ref):
    x = x_ref[...]
    n = x * lax.rsqrt(jnp.mean(x * x, axis=-1, keepdims=True) + NORM_EPS) * g_ref[...]
    o_ref[...] = jnp.dot(n.astype(bf16), w_ref[...], preferred_element_type=f32)


def norm_proj(x, g, w):
    m, n = x.shape[0], w.shape[1]
    tm, tn = _row_tile(m), 512
    return pl.pallas_call(
        _norm_proj_kernel,
        grid=(m // tm, n // tn),
        in_specs=[pl.BlockSpec((tm, D_MODEL), lambda i, j: (i, 0)),
                  pl.BlockSpec((1, D_MODEL), lambda i, j: (0, 0)),
                  pl.BlockSpec((D_MODEL, tn), lambda i, j: (0, j))],
        out_specs=pl.BlockSpec((tm, tn), lambda i, j: (i, j)),
        out_shape=jax.ShapeDtypeStruct((m, n), f32),
        compiler_params=pltpu.CompilerParams(
            dimension_semantics=("parallel", "arbitrary"), vmem_limit_bytes=VMEM_LIMIT),
        name="norm_proj",
    )(x, g.reshape(1, D_MODEL), w)


def _out_proj_kernel(x_ref, a_ref, r_ref, w_ref, o_ref):
    o_ref[...] = (x_ref[...]
                  + jnp.dot(a_ref[...].astype(bf16), w_ref[:D_ATTN, :], preferred_element_type=f32)
                  + jnp.dot(r_ref[...].astype(bf16), w_ref[D_ATTN:, :], preferred_element_type=f32))


def out_proj(x, o_attn, o_rwkv, w):
    m = x.shape[0]
    tm = _row_tile(m)
    return pl.pallas_call(
        _out_proj_kernel,
        grid=(m // tm,),
        in_specs=[pl.BlockSpec((tm, D_MODEL), lambda i: (i, 0)),
                  pl.BlockSpec((tm, D_ATTN), lambda i: (i, 0)),
                  pl.BlockSpec((tm, D_RWKV), lambda i: (i, 0)),
                  pl.BlockSpec((D_MODEL, D_MODEL), lambda i: (0, 0))],
        out_specs=pl.BlockSpec((tm, D_MODEL), lambda i: (i, 0)),
        out_shape=jax.ShapeDtypeStruct((m, D_MODEL), f32),
        compiler_params=pltpu.CompilerParams(
            dimension_semantics=("parallel",), vmem_limit_bytes=VMEM_LIMIT),
        name="out_proj",
    )(x, o_attn, o_rwkv, w)


def _ffn_kernel(h_ref, g_ref, wu_ref, wd_ref, o_ref, n_ref):
    j = pl.program_id(1)

    @pl.when(j == 0)
    def _():
        h = h_ref[...]
        n = h * lax.rsqrt(jnp.mean(h * h, axis=-1, keepdims=True) + NORM_EPS) * g_ref[...]
        n_ref[...] = n.astype(bf16)
        o_ref[...] = h

    u = jnp.dot(n_ref[...], wu_ref[...], preferred_element_type=f32)
    u = jnp.square(jnp.maximum(u, 0.0))
    o_ref[...] += jnp.dot(u.astype(bf16), wd_ref[...], preferred_element_type=f32)


def ffn(h, g, w_up, w_down):
    m = h.shape[0]
    tm, tf = _row_tile(m), 512
    return pl.pallas_call(
        _ffn_kernel,
        grid=(m // tm, D_FF // tf),
        in_specs=[pl.BlockSpec((tm, D_MODEL), lambda i, j: (i, 0)),
                  pl.BlockSpec((1, D_MODEL), lambda i, j: (0, 0)),
                  pl.BlockSpec((D_MODEL, tf), lambda i, j: (0, j)),
                  pl.BlockSpec((tf, D_MODEL), lambda i, j: (j, 0))],
        out_specs=pl.BlockSpec((tm, D_MODEL), lambda i, j: (i, 0)),
        out_shape=jax.ShapeDtypeStruct((m, D_MODEL), f32),
        scratch_shapes=[pltpu.VMEM((tm, D_MODEL), bf16)],
        compiler_params=pltpu.CompilerParams(
            dimension_semantics=("parallel", "arbitrary"), vmem_limit_bytes=VMEM_LIMIT),
        name="ffn",
    )(h, g.reshape(1, D_MODEL), w_up, w_down)


LANES = 128
KV_COLS = 2 * D_KV
N_GATES = N_BRANCH * N_HEADS_ATTN
N_LOWRANK = RANK_W + RANK_A + RANK_G
Z_Q, Z_R, Z_K, Z_V = 0, D_ATTN, D_ATTN + D_RWKV, D_ATTN + 2 * D_RWKV
Z_KV = D_ATTN + 3 * D_RWKV
Z_GATE = Z_KV + N_BRANCH * KV_COLS
Z_LR = Z_GATE + LANES
assert N_LOWRANK == LANES and Z_LR + N_LOWRANK == IN_COLS_PAD and N_GATES <= LANES


def permute_w_in(w_in):
    o_kv, o_gate, o_rwkv = D_ATTN, D_ATTN + 6 * D_KV, D_ATTN + 6 * D_KV + N_GATES
    pad = jnp.zeros(w_in.shape[:-1] + (LANES - N_GATES,), w_in.dtype)
    return jnp.concatenate([w_in[..., :o_kv], w_in[..., o_rwkv:o_rwkv + 3 * D_RWKV], w_in[..., o_kv:o_gate],
                            w_in[..., o_gate:o_rwkv], pad, w_in[..., o_rwkv + 3 * D_RWKV:]], axis=-1)


def split_z(z):
    z_attn = jnp.concatenate([z[..., :D_ATTN], z[..., Z_KV:Z_GATE]], axis=-1)
    z_rwkv = jnp.concatenate([z[..., Z_R:Z_KV], z[..., Z_LR:]], axis=-1)
    return z_attn, z[..., Z_GATE:Z_GATE + N_GATES], z_rwkv


def _seg_sumsq(x, p_ref):
    x2 = x * x
    hi = x2.astype(bf16)
    lo = (x2 - hi.astype(f32)).astype(bf16)
    return (jnp.dot(hi, p_ref[...], preferred_element_type=f32)
            + jnp.dot(lo, p_ref[...], preferred_element_type=f32))


def _norm_rope(x, g, p_ref, c, s1, s2):
    y = x * lax.rsqrt(_seg_sumsq(x, p_ref) * (1.0 / HEAD_DIM) + NORM_EPS) * g
    return y * c + pltpu.roll(y, LANES - ROT_DIM // 2, 1) * s1 + pltpu.roll(y, ROT_DIM // 2, 1) * s2


def _attn_prep_kernel(zq_ref, zc_ref, zs_ref, zw_ref, zg_ref, c_ref, s1_ref, s2_ref, p_ref, qg_ref, kg_ref, gb_ref,
                      q_ref, kvc_ref, kvs_ref, kvw_ref, kb_ref, vvb_ref, gate_ref):
    c, s1, s2 = c_ref[...], s1_ref[...], s2_ref[...]
    lane = lax.broadcasted_iota(jnp.int32, (zq_ref.shape[0], LANES), 1)
    for pair in range(N_HEADS_ATTN // 2):
        y = _norm_rope(zq_ref[:, pair * LANES:(pair + 1) * LANES], qg_ref[...], p_ref, c, s1, s2)
        y = y * (HEAD_DIM ** -0.5)
        y_sw = pltpu.roll(y, HEAD_DIM, 1)
        for half in range(2):
            h = 2 * pair + half
            grp = h // HEADS_PER_KV
            src = y if half == grp else y_sw
            keep = (lane >= HEAD_DIM) if grp == 1 else (lane < HEAD_DIM)
            q_ref[0, h] = jnp.where(keep, src, 0.0).astype(bf16)
    ins = (zc_ref, zs_ref, zw_ref)
    outs = (kvc_ref, kvs_ref, kvw_ref)
    for br in range(N_BRANCH):
        k = _norm_rope(ins[br][:, :D_KV], kg_ref[br:br + 1, :], p_ref, c, s1, s2)
        v = ins[br][:, D_KV:]
        outs[br][:, :D_KV] = k
        outs[br][:, D_KV:] = v
        if br > 0:
            kb_ref[br - 1] = k.astype(bf16)
            vvb_ref[br - 1, :, :D_KV] = v.astype(bf16)
            vvb_ref[br - 1, :, D_KV:] = pltpu.roll(v, HEAD_DIM, 1).astype(bf16)
    gate_ref[...] = jax.nn.sigmoid(zg_ref[...] + gb_ref[...])


def rope_lane_tables(pos):
    half = ROT_DIM // 2
    inv = ROPE_THETA ** (-(jnp.arange(0, ROT_DIM, 2, dtype=f32) / ROT_DIM))
    ang = pos.astype(f32)[:, None] * inv[None, :]
    cos, sin = jnp.cos(ang), jnp.sin(ang)
    t = pos.shape[0]
    one, zero = jnp.ones((t, HEAD_DIM - ROT_DIM), f32), jnp.zeros((t, HEAD_DIM - ROT_DIM), f32)
    zh = jnp.zeros((t, half), f32)
    c = jnp.concatenate([cos, cos, one], axis=1)
    s1 = jnp.concatenate([-sin, zh, zero], axis=1)
    s2 = jnp.concatenate([zh, sin, zero], axis=1)
    return tuple(jnp.tile(a, (1, 2)) for a in (c, s1, s2))


def attn_prep(z, rope_tabs, seg_ones, q_g, k_g, gate_b, batch, seq):
    m = batch * seq
    tm = min(512, seq)
    tiles = seq // tm
    c, s1, s2 = rope_tabs
    row = lambda i: (i, 0)
    tab = lambda i: (i % tiles, 0)
    const = lambda i: (0, 0)
    kv_shape = jax.ShapeDtypeStruct((m, KV_COLS), f32)
    return pl.pallas_call(
        _attn_prep_kernel,
        grid=(m // tm,),
        in_specs=[pl.BlockSpec((tm, D_ATTN), row),
                  pl.BlockSpec((tm, KV_COLS), lambda i: (i, Z_KV // KV_COLS)),
                  pl.BlockSpec((tm, KV_COLS), lambda i: (i, Z_KV // KV_COLS + 1)),
                  pl.BlockSpec((tm, KV_COLS), lambda i: (i, Z_KV // KV_COLS + 2)),
                  pl.BlockSpec((tm, LANES), lambda i: (i, Z_GATE // LANES)),
                  pl.BlockSpec((tm, LANES), tab), pl.BlockSpec((tm, LANES), tab), pl.BlockSpec((tm, LANES), tab),
                  pl.BlockSpec((LANES, LANES), const),
                  pl.BlockSpec((1, LANES), const), pl.BlockSpec((N_BRANCH, LANES), const),
                  pl.BlockSpec((1, LANES), const)],
        out_specs=[pl.BlockSpec((1, N_HEADS_ATTN, tm, LANES), lambda i: (i // tiles, 0, i % tiles, 0)),
                   pl.BlockSpec((tm, KV_COLS), row), pl.BlockSpec((tm, KV_COLS), row),
                   pl.BlockSpec((tm, KV_COLS), row),
                   pl.BlockSpec((2, tm, D_KV), lambda i: (0, i, 0)),
                   pl.BlockSpec((2, tm, KV_COLS), lambda i: (0, i, 0)),
                   pl.BlockSpec((tm, LANES), row)],
        out_shape=[jax.ShapeDtypeStruct((batch, N_HEADS_ATTN, seq, LANES), bf16),
                   kv_shape, kv_shape, kv_shape,
                   jax.ShapeDtypeStruct((2, m, D_KV), bf16),
                   jax.ShapeDtypeStruct((2, m, KV_COLS), bf16),
                   jax.ShapeDtypeStruct((m, LANES), f32)],
        compiler_params=pltpu.CompilerParams(
            dimension_semantics=("parallel",), vmem_limit_bytes=VMEM_LIMIT),
        name="attn_prep",
    )(z, z, z, z, z, c, s1, s2, seg_ones, q_g, k_g, gate_b)


CHUNK_COLS = STRIDE_CMP * KV_COLS
CMP_R = BLOCK_CMP // STRIDE_CMP


def compress_weights(cmp_pe, cmp_w):
    pe = cmp_pe.reshape(2, CMP_R, STRIDE_CMP, HEAD_DIM)
    w = cmp_w.reshape(2, CMP_R, STRIDE_CMP, HEAD_DIM, HEAD_DIM)
    eye_c, eye_g = jnp.eye(2, dtype=f32), jnp.eye(N_KV_HEADS, dtype=f32)
    wb = jnp.einsum('cisde,cx,gy->iscgdxye', w, eye_c, eye_g).reshape(CMP_R, CHUNK_COLS, KV_COLS)
    peb = jnp.broadcast_to(jnp.transpose(pe, (1, 2, 0, 3))[:, :, :, None, :],
                           (CMP_R, STRIDE_CMP, 2, N_KV_HEADS, HEAD_DIM)).reshape(CMP_R, 1, CHUNK_COLS)
    return wb.astype(bf16), peb


def _compress_kernel(x_ref, pe_ref, w_ref, kc_ref, vvc_ref):
    x = x_ref[0]
    n = x.shape[0]
    y0 = jnp.dot((x + pe_ref[0]).astype(bf16), w_ref[0], preferred_element_type=f32)
    y1 = jnp.dot((x + pe_ref[1]).astype(bf16), w_ref[1], preferred_element_type=f32)
    comp = y0 + pltpu.roll(y1, n - 1, 0)
    row = lax.broadcasted_iota(jnp.int32, comp.shape, 0)
    comp = jnp.where(row < n - 1, comp, 0.0)
    kc_ref[0] = comp[:, :D_KV].astype(bf16)
    vc = comp[:, D_KV:]
    vvc_ref[0, :, :D_KV] = vc.astype(bf16)
    vvc_ref[0, :, D_KV:] = pltpu.roll(vc, HEAD_DIM, 1).astype(bf16)


def compress(kv, wb, peb, batch):
    n_chunks = kv.shape[0] // batch // STRIDE_CMP
    x = kv.reshape(batch, n_chunks, CHUNK_COLS)
    return pl.pallas_call(
        _compress_kernel,
        grid=(batch,),
        in_specs=[pl.BlockSpec((1, n_chunks, CHUNK_COLS), lambda b: (b, 0, 0)),
                  pl.BlockSpec((CMP_R, 1, CHUNK_COLS), lambda b: (0, 0, 0)),
                  pl.BlockSpec((CMP_R, CHUNK_COLS, KV_COLS), lambda b: (0, 0, 0))],
        out_specs=[pl.BlockSpec((1, n_chunks, D_KV), lambda b: (b, 0, 0)),
                   pl.BlockSpec((1, n_chunks, KV_COLS), lambda b: (b, 0, 0))],
        out_shape=[jax.ShapeDtypeStruct((batch, n_chunks, D_KV), bf16),
                   jax.ShapeDtypeStruct((batch, n_chunks, KV_COLS), bf16)],
        compiler_params=pltpu.CompilerParams(
            dimension_semantics=("parallel",), vmem_limit_bytes=VMEM_LIMIT),
        name="compress",
    )(x, peb, wb)


KEY_CHUNK = 512
NT_DIMS = (((1,), (1,)), ((), ()))


def _softmax_rows(s3, mask):
    s3 = jnp.where(mask[None], s3, NEG)
    m = jnp.max(s3, axis=-1, keepdims=True)
    p = jnp.where(mask[None], jnp.exp(s3 - m), 0.0)
    l = jnp.sum(p, axis=-1, keepdims=True)
    return p * jnp.where(l > 0.0, 1.0 / l, 0.0)


def _nsa_prompt_kernel(q_ref, gate_ref, kc_ref, vvc_ref, ks_ref, vvs_ref, kw_ref, vvw_ref, smap_ref, e_ref,
                       o_ref, mask_ref, m_ref, l_ref, acc_ref, *, seq):
    n_slc = seq // BLOCK_SLC
    n_cmp = seq // STRIDE_CMP
    hq = HEADS_PER_KV * QBLOCK
    q0 = pl.program_id(1) * QBLOCK
    qpos_col = q0 + lax.broadcasted_iota(jnp.int32, (QBLOCK, 1), 0)

    heads_out = []
    for g in range(N_KV_HEADS):
        q = q_ref[0, g * HEADS_PER_KV:(g + 1) * HEADS_PER_KV].reshape(hq, LANES)

        s = lax.dot_general(q, kc_ref[0], NT_DIMS, preferred_element_type=f32)
        c_end = lax.broadcasted_iota(jnp.int32, (1, n_cmp), 1) * STRIDE_CMP + (BLOCK_CMP - 1)
        p_c = _softmax_rows(s.reshape(HEADS_PER_KV, QBLOCK, n_cmp), c_end <= qpos_col)
        p_c = p_c.astype(bf16).reshape(hq, n_cmp)
        o_c = jnp.dot(p_c, vvc_ref[0], preferred_element_type=f32)

        imp = jnp.zeros((n_slc, QBLOCK), f32)
        for h in range(HEADS_PER_KV):
            imp = imp + lax.dot_general(smap_ref[...], p_c[h * QBLOCK:(h + 1) * QBLOCK], NT_DIMS,
                                        preferred_element_type=f32)
        jj = lax.broadcasted_iota(jnp.int32, (n_slc, QBLOCK), 0)
        qp = q0 + lax.broadcasted_iota(jnp.int32, (n_slc, QBLOCK), 1)
        cur = qp // BLOCK_SLC
        forced = (jj == 0) | ((jj <= cur) & (jj > cur - N_LOCAL))
        imp = jnp.where(jj * BLOCK_SLC <= qp, jnp.where(forced, FORCED_SCORE, imp), NEG)
        rank = jnp.zeros((n_slc, QBLOCK), f32)
        for j in range(n_slc):
            row = imp[j:j + 1, :]
            rank = rank + jnp.where((row > imp) | ((row == imp) & (jj > j)), 1.0, 0.0)
        sel_t = jnp.where(rank < float(min(N_SELECT, n_slc)), 1.0, 0.0)
        if n_slc < LANES:
            sel_t = jnp.concatenate([sel_t, jnp.zeros((LANES - n_slc, QBLOCK), f32)], axis=0)
        sel = sel_t.T.astype(bf16)
        for c in range(seq // KEY_CHUNK):
            mask_ref[c] = jnp.dot(sel, e_ref[:, c * KEY_CHUNK:(c + 1) * KEY_CHUNK], preferred_element_type=f32)

        m_ref[...] = jnp.full(m_ref.shape, NEG, f32)
        l_ref[...] = jnp.zeros(l_ref.shape, f32)
        acc_ref[...] = jnp.zeros(acc_ref.shape, f32)

        def sel_step(c, carry):
            k0 = pl.multiple_of(c * KEY_CHUNK, KEY_CHUNK)
            sc = lax.dot_general(q, ks_ref[0, pl.ds(k0, KEY_CHUNK), :], NT_DIMS, preferred_element_type=f32)
            kpos = k0 + lax.broadcasted_iota(jnp.int32, (1, KEY_CHUNK), 1)
            msk = ((mask_ref[c] > 0.5) & (kpos <= qpos_col))[None]
            s3 = jnp.where(msk, sc.reshape(HEADS_PER_KV, QBLOCK, KEY_CHUNK), NEG)
            m_old = m_ref[...]
            m_new = jnp.maximum(m_old, jnp.max(s3, axis=-1, keepdims=True))
            alpha = jnp.exp(m_old - m_new)
            p = jnp.where(msk, jnp.exp(s3 - m_new), 0.0)
            l_ref[...] = alpha * l_ref[...] + jnp.sum(p, axis=-1, keepdims=True)
            pv = jnp.dot(p.astype(bf16).reshape(hq, KEY_CHUNK), vvs_ref[0, pl.ds(k0, KEY_CHUNK), :],
                         preferred_element_type=f32)
            acc_ref[...] = alpha * acc_ref[...] + pv.reshape(HEADS_PER_KV, QBLOCK, KV_COLS)
            m_ref[...] = m_new
            return carry

        lax.fori_loop(0, q0 // KEY_CHUNK + 1, sel_step, 0)
        o_s = (acc_ref[...] / l_ref[...]).reshape(hq, KV_COLS)

        n_win = WINDOW + QBLOCK
        w0 = pl.multiple_of(jnp.maximum(q0 - WINDOW, 0), QBLOCK)
        sw = lax.dot_general(q, kw_ref[0, pl.ds(w0, n_win), :], NT_DIMS, preferred_element_type=f32)
        dist = qpos_col - (w0 + lax.broadcasted_iota(jnp.int32, (1, n_win), 1))
        p_w = _softmax_rows(sw.reshape(HEADS_PER_KV, QBLOCK, n_win), (dist >= 0) & (dist < WINDOW))
        o_w = jnp.dot(p_w.astype(bf16).reshape(hq, n_win), vvw_ref[0, pl.ds(w0, n_win), :],
                      preferred_element_type=f32)

        for hh in range(HEADS_PER_KV):
            h = g * HEADS_PER_KV + hh
            half = 0 if (h % 2) == g else 1
            rows = slice(hh * QBLOCK, (hh + 1) * QBLOCK)
            cols = slice(half * LANES, (half + 1) * LANES)
            gc = gate_ref[:, N_BRANCH * h:N_BRANCH * h + 1]
            gs = gate_ref[:, N_BRANCH * h + 1:N_BRANCH * h + 2]
            gw = gate_ref[:, N_BRANCH * h + 2:N_BRANCH * h + 3]
            heads_out.append(gc * o_c[rows, cols] + gs * o_s[rows, cols] + gw * o_w[rows, cols])

    lane = lax.broadcasted_iota(jnp.int32, (QBLOCK, LANES), 1)
    for pair in range(N_HEADS_ATTN // 2):
        o_ref[0, :, pair * LANES:(pair + 1) * LANES] = jnp.where(
            lane < HEAD_DIM, heads_out[2 * pair], heads_out[2 * pair + 1])


def nsa_tables(seq):
    n_slc, n_cmp = seq // BLOCK_SLC, seq // STRIDE_CMP
    start = np.arange(n_cmp) * STRIDE_CMP
    first, last = start // BLOCK_SLC, (start + BLOCK_CMP - 1) // BLOCK_SLC
    j = np.arange(n_slc)
    smap = (first[None, :] <= j[:, None]) & (last[None, :] >= j[:, None]) & (np.arange(n_cmp)[None, :] < n_cmp - 1)
    expand = (np.arange(seq)[None, :] // BLOCK_SLC) == np.arange(max(n_slc, LANES))[:, None]
    return jnp.asarray(smap, bf16), jnp.asarray(expand, bf16)


def nsa_prompt(q_ext, gates, kc, vvc, kb, vvb, tables, batch, seq):
    smap, expand = tables
    n_cmp = seq // STRIDE_CMP
    hq = HEADS_PER_KV * QBLOCK
    per_b = lambda b, i: (b, 0, 0)
    const = lambda b, i: (0, 0)
    gates3 = gates.reshape(batch, seq, LANES)
    kb4 = kb.reshape(2, batch, seq, D_KV)
    vvb4 = vvb.reshape(2, batch, seq, KV_COLS)
    branch = lambda br: (lambda b, i: (br, b, 0, 0))
    return pl.pallas_call(
        functools.partial(_nsa_prompt_kernel, seq=seq),
        grid=(batch, seq // QBLOCK),
        in_specs=[pl.BlockSpec((1, N_HEADS_ATTN, QBLOCK, LANES), lambda b, i: (b, 0, i, 0)),
                  pl.BlockSpec((None, QBLOCK, LANES), lambda b, i: (b, i, 0)),
                  pl.BlockSpec((1, n_cmp, D_KV), per_b), pl.BlockSpec((1, n_cmp, KV_COLS), per_b),
                  pl.BlockSpec((None, 1, seq, D_KV), branch(0)), pl.BlockSpec((None, 1, seq, KV_COLS), branch(0)),
                  pl.BlockSpec((None, 1, seq, D_KV), branch(1)), pl.BlockSpec((None, 1, seq, KV_COLS), branch(1)),
                  pl.BlockSpec(smap.shape, const), pl.BlockSpec(expand.shape, const)],
        out_specs=pl.BlockSpec((1, QBLOCK, D_ATTN), lambda b, i: (b, i, 0)),
        out_shape=jax.ShapeDtypeStruct((batch, seq, D_ATTN), f32),
        scratch_shapes=[pltpu.VMEM((seq // KEY_CHUNK, QBLOCK, KEY_CHUNK), f32),
                        pltpu.VMEM((HEADS_PER_KV, QBLOCK, 1), f32),
                        pltpu.VMEM((HEADS_PER_KV, QBLOCK, 1), f32),
                        pltpu.VMEM((HEADS_PER_KV, QBLOCK, KV_COLS), f32)],
        compiler_params=pltpu.CompilerParams(
            dimension_semantics=("parallel", "arbitrary"), vmem_limit_bytes=VMEM_LIMIT),
        name="nsa_prompt",
    )(q_ext, gates3, kc, vvc, kb4, vvb4, kb4, vvb4, smap, expand)


RW_CHUNK = 64
N_PAIRS = N_HEADS_RWKV // 2
TN_DIMS = (((0,), (0,)), ((), ()))


def _mm(a, b):
    return jnp.dot(a.astype(bf16), b.astype(bf16), preferred_element_type=f32)


def _mm_nt(a, b):
    return lax.dot_general(a.astype(bf16), b.astype(bf16), NT_DIMS, preferred_element_type=f32)


def _mm_tn(a, b):
    return lax.dot_general(a.astype(bf16), b.astype(bf16), TN_DIMS, preferred_element_type=f32)


def _mm_split(a01, x):
    hi = x.astype(bf16)
    lo = (x - hi.astype(f32)).astype(bf16)
    return jnp.dot(a01, hi, preferred_element_type=f32) + jnp.dot(a01, lo, preferred_element_type=f32)


def _seg_sum(x, p_ref):
    hi = x.astype(bf16)
    lo = (x - hi.astype(f32)).astype(bf16)
    return (jnp.dot(hi, p_ref[...], preferred_element_type=f32)
            + jnp.dot(lo, p_ref[...], preferred_element_type=f32))


def _stack_heads(x):
    lane = lax.broadcasted_iota(jnp.int32, x.shape, 1)
    return jnp.concatenate([jnp.where(lane < HEAD_DIM, x, 0.0), jnp.where(lane >= HEAD_DIM, x, 0.0)], axis=0)


def _chunk_pair(r, cum, lw, alpha, beta, k, v, st):
    c = r.shape[0]
    eg, egi, egp = jnp.exp(cum), jnp.exp(-cum), jnp.exp(cum - lw)
    g_end = eg[c - 1:c, :]
    x_a, x_r = _stack_heads(alpha * egp), _stack_heads(r * eg)
    bt, kt = beta * egi, k * egi
    b_b, b_k = jnp.concatenate([bt, bt], axis=0), jnp.concatenate([kt, kt], axis=0)
    x_bg, x_kg, v_s = _stack_heads(bt * g_end), _stack_heads(kt * g_end), _stack_heads(v)
    row = lax.broadcasted_iota(jnp.int32, (2 * c, 2 * c), 0)
    col = lax.broadcasted_iota(jnp.int32, (2 * c, 2 * c), 1)
    same = (row // c) == (col // c)
    strict = same & (col < row)
    incl = same & (col <= row)
    n = jnp.where(strict, _mm_nt(x_a, b_b), 0.0)
    a_ak = jnp.where(strict, _mm_nt(x_a, b_k), 0.0)
    a_rb = jnp.where(incl, _mm_nt(x_r, b_b), 0.0)
    a_rk = jnp.where(incl, _mm_nt(x_r, b_k), 0.0)
    def level(k):
        return jnp.where(((row >> (k + 1)) == (col >> (k + 1))) & ((row >> k) != (col >> k)), n, 0.0)
    inv = jnp.where(row == col, 1.0, 0.0) + level(0)
    for k in range(1, int(np.log2(c))):
        inv = inv + _mm(_mm(inv, level(k)), inv)
    st_b = st.astype(bf16)
    p = _mm(x_a, st_b) + _mm(a_ak, v_s)
    e = _mm(inv, p)
    y_s = _mm(x_r, st_b) + _mm(a_rb, e) + _mm(a_rk, v_s)
    g_col = jnp.broadcast_to(g_end, (2 * c, 2 * c)).T
    st_new = st * g_col + _mm_tn(x_bg, e) + _mm_tn(x_kg, v_s)
    return y_s[:c] + y_s[c:], st_new


def _rwkv_prompt_kernel(*refs, first_layer):
    zr_ref, zk_ref, zv_ref, zl_ref, sp_ref, mu_ref, vec_ref, lr_ref, p_ref, tri_ref = refs[:10]
    if first_layer:
        o_ref, st_ref, vfo_ref, prev_ref = refs[10:]
    else:
        vf_ref, o_ref, st_ref, prev_ref = refs[10:]
    c = RW_CHUNK
    ci = pl.program_id(1)

    @pl.when(ci == 0)
    def _():
        prev_ref[0:1, :] = sp_ref[...]
        st_ref[...] = jnp.zeros(st_ref.shape, f32)

    def shifted(z_ref, lo, hi):
        z = z_ref[...]
        first = lax.broadcasted_iota(jnp.int32, z.shape, 0) == 0
        prev = jnp.where(first, prev_ref[0:1, lo:hi], pltpu.roll(z, 1, 0))
        xs = z + (prev - z) * mu_ref[:, lo:hi]
        prev_ref[0:1, lo:hi] = z[c - 1:c, :]
        return xs

    xr = shifted(zr_ref, 0, D_RWKV)
    xk = shifted(zk_ref, D_RWKV, 2 * D_RWKV)
    xv = shifted(zv_ref, 2 * D_RWKV, 3 * D_RWKV)
    xl = shifted(zl_ref, 3 * D_RWKV, SHIFT_COLS)
    w0, a0, vres0, k_k = vec_ref[0:1, :], vec_ref[1:2, :], vec_ref[2:3, :], vec_ref[3:4, :]
    k_a, r_k, lnx_w, lnx_b = vec_ref[4:5, :], vec_ref[5:6, :], vec_ref[6:7, :], vec_ref[7:8, :]

    xl_b = xl.astype(bf16)
    u = -(w0 + jnp.dot(jnp.tanh(xl).astype(bf16), lr_ref[0], preferred_element_type=f32))
    softplus = jnp.maximum(u, 0.0) + jnp.log(1.0 + jnp.exp(-jnp.abs(u)))
    lw = -jnp.exp(-softplus - 0.5)
    a = jax.nn.sigmoid(a0 + jnp.dot(xl_b, lr_ref[1], preferred_element_type=f32))
    if first_layer:
        v = xv
        vfo_ref[...] = xv
    else:
        v = xv + (vf_ref[...] - xv) * jax.nn.sigmoid(vres0 + jnp.dot(xl_b, lr_ref[2], preferred_element_type=f32))
    gate = jnp.dot(jax.nn.sigmoid(xl).astype(bf16), lr_ref[3], preferred_element_type=f32)
    kk = xk * k_k
    k2 = xk * (1.0 + (a - 1.0) * k_a)
    cum = _mm_split(tri_ref[...], lw)

    for pair in range(N_PAIRS):
        cols = slice(pair * LANES, (pair + 1) * LANES)
        kk_p = kk[:, cols]
        kk_p = kk_p * lax.rsqrt(jnp.maximum(_seg_sum(kk_p * kk_p, p_ref), 1e-24))
        r_p, k_p, v_p = xr[:, cols], k2[:, cols], v[:, cols]
        y, st_new = _chunk_pair(r_p, cum[:, cols], lw[:, cols], -kk_p, kk_p * a[:, cols], k_p, v_p, st_ref[pair])
        st_ref[pair] = st_new
        mean = _seg_sum(y, p_ref) * (1.0 / HEAD_DIM)
        d = y - mean
        var = _seg_sum(d * d, p_ref) * (1.0 / HEAD_DIM)
        yn = d * lax.rsqrt(var + GN_EPS) * lnx_w[:, cols] + lnx_b[:, cols]
        bonus = _seg_sum(r_p * k_p * r_k[:, cols], p_ref) * v_p
        o_ref[:, cols] = (yn + bonus) * gate[:, cols]


def rwkv_prompt(z, v_first, lp, consts, batch, seq):
    m = batch * seq
    c = RW_CHUNK
    n_chunks = seq // c
    first_layer = v_first is None
    rows = lambda j: (lambda b, i: (b * n_chunks + i, j))
    const2 = lambda b, i: (0, 0)
    tok = pl.BlockSpec((c, D_RWKV), rows(0))
    tok_shape = jax.ShapeDtypeStruct((m, D_RWKV), f32)
    out = pl.pallas_call(
        functools.partial(_rwkv_prompt_kernel, first_layer=first_layer),
        grid=(batch, n_chunks),
        in_specs=[pl.BlockSpec((c, D_RWKV), rows(Z_R // D_RWKV)),
                  pl.BlockSpec((c, D_RWKV), rows(Z_K // D_RWKV)),
                  pl.BlockSpec((c, D_RWKV), rows(Z_V // D_RWKV)),
                  pl.BlockSpec((c, LANES), rows(Z_LR // LANES)),
                  pl.BlockSpec((None, 1, SHIFT_COLS), lambda b, i: (b, 0, 0)),
                  pl.BlockSpec((1, SHIFT_COLS), const2),
                  pl.BlockSpec((8, D_RWKV), const2),
                  pl.BlockSpec((4, LANES, D_RWKV), lambda b, i: (0, 0, 0)),
                  pl.BlockSpec((LANES, LANES), const2),
                  pl.BlockSpec((c, c), const2)] + ([] if first_layer else [tok]),
        out_specs=[tok, pl.BlockSpec((None, N_PAIRS, LANES, LANES), lambda b, i: (b, 0, 0, 0))]
                  + ([tok] if first_layer else []),
        out_shape=[tok_shape, jax.ShapeDtypeStruct((batch, N_PAIRS, LANES, LANES), f32)]
                  + ([tok_shape] if first_layer else []),
        scratch_shapes=[pltpu.VMEM((8, SHIFT_COLS), f32)],
        compiler_params=pltpu.CompilerParams(
            dimension_semantics=("parallel", "arbitrary"), vmem_limit_bytes=VMEM_LIMIT),
        name="rwkv_prompt",
    )(z, z, z, z, consts['shift_zero'], lp['mu_row'], lp['rwkv_vecs'], lp['rwkv_lowrank'],
      consts['seg_ones'], consts['tri'], *([] if first_layer else [v_first]))
    return out[0], out[1], (out[2] if first_layer else v_first)


def state_from_pair_tiles(st):
    b = st.shape[0]
    t = st.reshape(b, N_PAIRS, 2, HEAD_DIM, 2, HEAD_DIM)
    diag = jnp.stack([t[:, :, 0, :, 0, :], t[:, :, 1, :, 1, :]], axis=2)
    return jnp.swapaxes(diag, -1, -2).reshape(b, N_HEADS_RWKV, HEAD_DIM, HEAD_DIM)


def rwkv_params(lp_raw, layer):
    zrow = jnp.zeros((D_RWKV,), f32)
    vres0 = lp_raw['vres0'][layer - 1] if layer > 0 else zrow
    vecs = jnp.stack([lp_raw['w0'][layer], lp_raw['a0'][layer], vres0, lp_raw['k_k'][layer], lp_raw['k_a'][layer],
                      lp_raw['r_k'][layer].reshape(D_RWKV), lp_raw['lnx_w'][layer], lp_raw['lnx_b'][layer]])
    def rows(w, lo):
        return jnp.zeros((LANES, D_RWKV), f32).at[lo:lo + w.shape[0]].set(w)
    vres2 = lp_raw['vres2'][layer - 1] if layer > 0 else jnp.zeros((RANK_A, D_RWKV), f32)
    lowrank = jnp.stack([rows(lp_raw['w2'][layer], 0), rows(lp_raw['a2'][layer], RANK_W),
                         rows(vres2, RANK_W), rows(lp_raw['g2'][layer], RANK_W + RANK_A)]).astype(bf16)
    return vecs, lowrank


CHUNKS_PER_PAGE = PAGE_SIZE // STRIDE_CMP
PAGES_PER_STEP = 8
BLOCKS_PER_STEP = PAGES_PER_STEP * PAGE_SIZE // BLOCK_SLC
N_BLK_PAD = 256


def _compress_pool_kernel(x_ref, pe_ref, w_ref, y_ref):
    x = x_ref[...]
    y_ref[:, :KV_COLS] = jnp.dot((x + pe_ref[0]).astype(bf16), w_ref[0], preferred_element_type=f32)
    y_ref[:, KV_COLS:] = jnp.dot((x + pe_ref[1]).astype(bf16), w_ref[1], preferred_element_type=f32)


def compress_pool(cache_cmp, wb, peb):
    depth, n_pool = cache_cmp.shape[:2]
    rows = n_pool * CHUNKS_PER_PAGE
    x = cache_cmp.reshape(depth, rows, CHUNK_COLS)
    tm = 512
    assert rows % tm == 0
    return pl.pallas_call(
        _compress_pool_kernel,
        grid=(depth, rows // tm),
        in_specs=[pl.BlockSpec((None, tm, CHUNK_COLS), lambda l, i: (l, i, 0)),
                  pl.BlockSpec((None, CMP_R, 1, CHUNK_COLS), lambda l, i: (l, 0, 0, 0)),
                  pl.BlockSpec((None, CMP_R, CHUNK_COLS, KV_COLS), lambda l, i: (l, 0, 0, 0))],
        out_specs=pl.BlockSpec((None, tm, 2 * KV_COLS), lambda l, i: (l, i, 0)),
        out_shape=jax.ShapeDtypeStruct((depth, rows, 2 * KV_COLS), f32),
        compiler_params=pltpu.CompilerParams(
            dimension_semantics=("parallel", "parallel"), vmem_limit_bytes=VMEM_LIMIT),
        name="compress_pool",
    )(x, peb, wb)


def _softmax_2d(s, mask):
    s = jnp.where(mask, s, NEG)
    m = jnp.max(s, axis=-1, keepdims=True)
    p = jnp.where(mask, jnp.exp(s - m), 0.0)
    l = jnp.sum(p, axis=-1, keepdims=True)
    return p * jnp.where(l > 0.0, 1.0 / l, 0.0)


def _pad_rows(x, rows):
    return jnp.concatenate([x, jnp.zeros((rows - x.shape[0], x.shape[1]), x.dtype)], axis=0)


def _sample_attn_kernel(pt_ref, q_ref, gate_ref, y_ref, *rest, past, n_new):
    pages = rest[:PAGES_PER_STEP]
    (ksn_ref, kwn_ref, cw_ref, smap_ref, o_ref,
     oc_ref, ow_ref, sel_ref, imp_ref, m_ref, l_ref, acc_ref) = rest[PAGES_PER_STEP:]
    del pt_ref
    c = pl.program_id(1)
    n_steps = pl.num_programs(1)
    rows = N_HEADS_ATTN * n_new
    q = q_ref[...]
    qpos = past + lax.broadcasted_iota(jnp.int32, (rows, 1), 0) % n_new
    n_blk = -(-(past + n_new) // BLOCK_SLC)

    @pl.when(c == 0)
    def _():
        y = y_ref[...]
        n_chunks = y.shape[0]
        comp = y[:, :KV_COLS] + pltpu.roll(y[:, KV_COLS:], n_chunks - 1, 0)
        s = lax.dot_general(q, comp[:, :D_KV].astype(bf16), NT_DIMS, preferred_element_type=f32)
        c_end = lax.broadcasted_iota(jnp.int32, (1, n_chunks), 1) * STRIDE_CMP + (BLOCK_CMP - 1)
        valid = (c_end <= qpos) & (lax.broadcasted_iota(jnp.int32, (1, n_chunks), 1) < n_chunks - 1)
        p_c = _softmax_2d(s, valid)
        oc_ref[...] = jnp.dot(p_c.astype(bf16), comp[:, D_KV:].astype(bf16), preferred_element_type=f32)

        hq = HEADS_PER_KV * n_new
        p_sum = [sum(p_c[g * hq + hh * n_new:g * hq + (hh + 1) * n_new] for hh in range(HEADS_PER_KV))
                 for g in range(N_KV_HEADS)]
        p_sum = _pad_rows(jnp.concatenate(p_sum, axis=0), LANES).astype(bf16)
        imp = lax.dot_general(smap_ref[...], p_sum, NT_DIMS, preferred_element_type=f32)
        jj = lax.broadcasted_iota(jnp.int32, (N_BLK_PAD, LANES), 0)
        qp = past + lax.broadcasted_iota(jnp.int32, (N_BLK_PAD, LANES), 1) % n_new
        cur = qp // BLOCK_SLC
        forced = (jj == 0) | ((jj <= cur) & (jj > cur - N_LOCAL))
        imp = jnp.where(jj * BLOCK_SLC <= qp, jnp.where(forced, FORCED_SCORE, imp), NEG)
        imp_ref[...] = imp

        def rank_step(j, rank):
            row = imp_ref[pl.ds(j, 1), :]
            return rank + jnp.where((row > imp) | ((row == imp) & (jj > j)), 1.0, 0.0)

        rank = lax.fori_loop(0, n_blk, rank_step, jnp.zeros((N_BLK_PAD, LANES), f32))
        sel = jnp.where(rank < float(min(N_SELECT, n_blk)), 1.0, 0.0).T
        sel_ref[...] = jnp.concatenate(
            [sel[g * n_new:(g + 1) * n_new] for g in range(N_KV_HEADS) for _ in range(HEADS_PER_KV)], axis=0)

        w_buf = cw_ref.shape[0]
        cw = cw_ref[...]
        kwn = _pad_rows(kwn_ref[...], LANES)
        s1 = lax.dot_general(q, cw[:, :D_KV].astype(bf16), NT_DIMS, preferred_element_type=f32)
        s2 = lax.dot_general(q, kwn[:, :D_KV].astype(bf16), NT_DIMS, preferred_element_type=f32)
        wpos = jnp.concatenate([past - w_buf + lax.broadcasted_iota(jnp.int32, (1, w_buf), 1),
                                past + lax.broadcasted_iota(jnp.int32, (1, LANES), 1)], axis=1)
        dist = qpos - wpos
        p_w = _softmax_2d(jnp.concatenate([s1, s2], axis=1), (dist >= 0) & (dist < WINDOW)).astype(bf16)
        ow_ref[...] = (jnp.dot(p_w[:, :w_buf], cw[:, D_KV:].astype(bf16), preferred_element_type=f32)
                       + jnp.dot(p_w[:, w_buf:], kwn[:, D_KV:].astype(bf16), preferred_element_type=f32))

        m_ref[...] = jnp.full(m_ref.shape, NEG, f32)
        l_ref[...] = jnp.zeros(l_ref.shape, f32)
        acc_ref[...] = jnp.zeros(acc_ref.shape, f32)

    def online_update(k, v, msk):
        s = jnp.where(msk, lax.dot_general(q, k.astype(bf16), NT_DIMS, preferred_element_type=f32), NEG)
        m_old = m_ref[...]
        m_new = jnp.maximum(m_old, jnp.max(s, axis=-1, keepdims=True))
        alpha = jnp.exp(m_old - m_new)
        p = jnp.where(msk, jnp.exp(s - m_new), 0.0)
        l_ref[...] = alpha * l_ref[...] + jnp.sum(p, axis=-1, keepdims=True)
        acc_ref[...] = alpha * acc_ref[...] + jnp.dot(p.astype(bf16), v.astype(bf16), preferred_element_type=f32)
        m_ref[...] = m_new

    kv = jnp.concatenate([pg[...] for pg in pages], axis=0)
    n_keys = kv.shape[0]
    blk = lax.broadcasted_iota(jnp.int32, (N_BLK_PAD, n_keys), 0)
    key_blk = c * BLOCKS_PER_STEP + lax.broadcasted_iota(jnp.int32, (N_BLK_PAD, n_keys), 1) // BLOCK_SLC
    expand = jnp.where(blk == key_blk, 1.0, 0.0).astype(bf16)
    msk = jnp.dot(sel_ref[...].astype(bf16), expand, preferred_element_type=f32) > 0.5
    online_update(kv[:, :D_KV], kv[:, D_KV:], msk)

    @pl.when(c == n_steps - 1)
    def _():
        ksn = _pad_rows(ksn_ref[...], LANES)
        kpos = past + lax.broadcasted_iota(jnp.int32, (1, LANES), 1)
        last_blk = past // BLOCK_SLC
        msk_new = (sel_ref[:, last_blk:last_blk + 1] > 0.5) & (kpos <= qpos)
        online_update(ksn[:, :D_KV], ksn[:, D_KV:], msk_new)
        o_s = acc_ref[...] / l_ref[...]
        o_c, o_w = oc_ref[...], ow_ref[...]
        lane = lax.broadcasted_iota(jnp.int32, (n_new, LANES), 1)
        heads = []
        for h in range(N_HEADS_ATTN):
            r = slice(h * n_new, (h + 1) * n_new)
            gc = gate_ref[:, N_BRANCH * h:N_BRANCH * h + 1]
            gs = gate_ref[:, N_BRANCH * h + 1:N_BRANCH * h + 2]
            gw = gate_ref[:, N_BRANCH * h + 2:N_BRANCH * h + 3]
            o_h = gc * o_c[r] + gs * o_s[r] + gw * o_w[r]
            heads.append(o_h if (h // HEADS_PER_KV) == (h % 2) else pltpu.roll(o_h, HEAD_DIM, 1))
        for pair in range(N_HEADS_ATTN // 2):
            o_ref[:, pair * LANES:(pair + 1) * LANES] = jnp.where(lane < HEAD_DIM, heads[2 * pair], heads[2 * pair + 1])


def sample_tables(past, n_new):
    n_chunks = past // STRIDE_CMP
    start = np.arange(n_chunks) * STRIDE_CMP
    first, last = start // BLOCK_SLC, (start + BLOCK_CMP - 1) // BLOCK_SLC
    j = np.arange(N_BLK_PAD)
    smap = (first[None, :] <= j[:, None]) & (last[None, :] >= j[:, None]) & (np.arange(n_chunks)[None, :] < n_chunks - 1)
    return jnp.asarray(smap, bf16)


def sample_attn(q_rows, gates, y_req, cache_slc, layer, page_table, kv_s_new, kv_w_new, cache_win, smap):
    bd, n_pages = page_table.shape
    n_new = kv_s_new.shape[1]
    past = n_pages * PAGE_SIZE
    rows = N_HEADS_ATTN * n_new
    w_buf = cache_win.shape[2]
    assert n_pages % PAGES_PER_STEP == 0 and -(-(past + n_new) // BLOCK_SLC) <= N_BLK_PAD and n_new <= LANES
    per_b = lambda b, c, pt: (b, 0, 0)
    page = lambda j: (lambda b, c, pt: (layer, pt[b, c * PAGES_PER_STEP + j], 0, 0))
    grid_spec = pltpu.PrefetchScalarGridSpec(
        num_scalar_prefetch=1,
        grid=(bd, n_pages // PAGES_PER_STEP),
        in_specs=[pl.BlockSpec((None, rows, LANES), per_b),
                  pl.BlockSpec((None, n_new, LANES), per_b),
                  pl.BlockSpec((None, past // STRIDE_CMP, 2 * KV_COLS), per_b)]
                 + [pl.BlockSpec((None, None, PAGE_SIZE, KV_COLS), page(j)) for j in range(PAGES_PER_STEP)]
                 + [pl.BlockSpec((None, n_new, KV_COLS), per_b),
                    pl.BlockSpec((None, n_new, KV_COLS), per_b),
                    pl.BlockSpec((None, None, w_buf, KV_COLS), lambda b, c, pt: (layer, b, 0, 0)),
                    pl.BlockSpec(smap.shape, lambda b, c, pt: (0, 0))],
        out_specs=pl.BlockSpec((None, n_new, D_ATTN), per_b),
        scratch_shapes=[pltpu.VMEM((rows, LANES), f32), pltpu.VMEM((rows, LANES), f32),
                        pltpu.VMEM((rows, N_BLK_PAD), f32), pltpu.VMEM((N_BLK_PAD, LANES), f32),
                        pltpu.VMEM((rows, 1), f32), pltpu.VMEM((rows, 1), f32), pltpu.VMEM((rows, LANES), f32)])
    return pl.pallas_call(
        functools.partial(_sample_attn_kernel, past=past, n_new=n_new),
        grid_spec=grid_spec,
        out_shape=jax.ShapeDtypeStruct((bd, n_new, D_ATTN), f32),
        compiler_params=pltpu.CompilerParams(
            dimension_semantics=("parallel", "arbitrary"), vmem_limit_bytes=VMEM_LIMIT),
        name="sample_attn",
    )(page_table, q_rows, gates, y_req, *([cache_slc] * PAGES_PER_STEP), kv_s_new, kv_w_new, cache_win, smap)


def _split(x, sizes):
    offs = [int(o) for o in np.cumsum(sizes)[:-1]]
    return jnp.split(x, offs, axis=-1)


def rms_norm(x, g):
    xf = x.astype(f32)
    y = xf * lax.rsqrt(jnp.mean(xf * xf, axis=-1, keepdims=True) + NORM_EPS)
    return (y * g.astype(f32)).astype(x.dtype)


def rope_tables(pos):
    inv = ROPE_THETA ** (-(jnp.arange(0, ROT_DIM, 2, dtype=f32) / ROT_DIM))
    ang = pos.astype(f32)[:, None] * inv[None, :]
    return jnp.cos(ang), jnp.sin(ang)


def apply_partial_rope(x, cos, sin):
    xr = x[..., :ROT_DIM].astype(f32)
    x1, x2 = xr[..., :ROT_DIM // 2], xr[..., ROT_DIM // 2:]
    c, s = cos[None, :, None, :], sin[None, :, None, :]
    rot = jnp.concatenate([x1 * c - x2 * s, x2 * c + x1 * s], axis=-1).astype(x.dtype)
    return jnp.concatenate([rot, x[..., ROT_DIM:]], axis=-1)


def masked_softmax(s, mask):
    p = jax.nn.softmax(jnp.where(mask, s, NEG), axis=-1)
    return jnp.where(mask, p, 0.0)


def gather_pages(pool, page_table):
    g = pool[page_table]
    return g.reshape(page_table.shape[0], -1, 2, N_KV_HEADS, HEAD_DIM)


def gather_blocks(blocks, idx):
    return jax.vmap(jax.vmap(lambda b, i: b[i]))(blocks, idx)


def nsa_heads(z_attn, pos, q_norm_g, k_norm_g):
    B, T = z_attn.shape[:2]
    parts = _split(z_attn, [D_ATTN] + [D_KV] * 6)
    cos, sin = rope_tables(pos)
    q = apply_partial_rope(rms_norm(parts[0].reshape(B, T, N_HEADS_ATTN, HEAD_DIM), q_norm_g), cos, sin)
    kvs = []
    for br in range(N_BRANCH):
        k = parts[1 + 2 * br].reshape(B, T, N_KV_HEADS, HEAD_DIM)
        v = parts[2 + 2 * br].reshape(B, T, N_KV_HEADS, HEAD_DIM)
        k = apply_partial_rope(rms_norm(k, k_norm_g[br]), cos, sin)
        kvs.append(jnp.stack([k, v], axis=2))
    return q, kvs


def compress_blocks(kv, cmp_pe, cmp_w):
    B, L = kv.shape[:2]
    n_cmp = (L - BLOCK_CMP) // STRIDE_CMP + 1
    r = BLOCK_CMP // STRIDE_CMP
    n_chunks = n_cmp + r - 1
    chunks = kv[:, :n_chunks * STRIDE_CMP].reshape(B, n_chunks, STRIDE_CMP, 2, N_KV_HEADS, HEAD_DIM)
    pe = cmp_pe.reshape(2, r, STRIDE_CMP, HEAD_DIM)
    w = cmp_w.reshape(2, r, STRIDE_CMP, HEAD_DIM, HEAD_DIM)
    comp = 0.0
    for i in range(r):
        part = chunks[:, i:i + n_cmp] + jnp.moveaxis(pe[:, i], 0, 1)[None, None, :, :, None, :]
        comp = comp + jnp.einsum('bnscgd,csde->bncge', part, w[:, i])
    c_end = jnp.arange(n_cmp, dtype=jnp.int32) * STRIDE_CMP + BLOCK_CMP - 1
    return comp[:, :, 0], comp[:, :, 1], c_end


def cmp_to_slc_map(n_cmp, n_slc):
    start = np.arange(n_cmp) * STRIDE_CMP
    first, last = start // BLOCK_SLC, (start + BLOCK_CMP - 1) // BLOCK_SLC
    j = np.arange(n_slc)
    return jnp.asarray(((first[:, None] <= j[None, :]) & (last[:, None] >= j[None, :])).astype(np.float32))


def to_slc_blocks(kv):
    B, L = kv.shape[:2]
    n_slc = -(-L // BLOCK_SLC)
    kv = jnp.pad(kv, ((0, 0), (0, n_slc * BLOCK_SLC - L), (0, 0), (0, 0), (0, 0)))
    return kv.reshape(B, n_slc, BLOCK_SLC, 2, N_KV_HEADS, HEAD_DIM).transpose(0, 4, 1, 2, 3, 5)


def nsa_query_block(q, q_pos, kc, vc, c_end, sel_map, slc_blocks, win_kv, w_pos, gates):
    B, Tq = q.shape[:2]
    qg = q.astype(f32).reshape(B, Tq, N_KV_HEADS, HEADS_PER_KV, HEAD_DIM) * (HEAD_DIM ** -0.5)
    s_c = jnp.einsum('bqghd,bngd->bghqn', qg, kc.astype(f32))
    p_c = masked_softmax(s_c, c_end[None, :] <= q_pos[:, None])
    o_c = jnp.einsum('bghqn,bngd->bqghd', p_c, vc.astype(f32))
    n_slc = slc_blocks.shape[2]
    imp = jnp.einsum('bghqn,ns->bgqs', p_c, sel_map)
    j = jnp.arange(n_slc, dtype=jnp.int32)[None, :]
    cur = (q_pos // BLOCK_SLC)[:, None]
    forced = (j == 0) | ((j <= cur) & (j > cur - N_LOCAL))
    causal = j * BLOCK_SLC <= q_pos[:, None]
    imp = jnp.where(causal, jnp.where(forced, FORCED_SCORE, imp), NEG)
    n_sel = min(N_SELECT, n_slc)
    _, idx = lax.top_k(imp, n_sel)
    g_kv = gather_blocks(slc_blocks, idx.reshape(B, N_KV_HEADS, Tq * n_sel))
    g_kv = g_kv.reshape(B, N_KV_HEADS, Tq, n_sel * BLOCK_SLC, 2, HEAD_DIM).astype(f32)
    k_pos = (idx[..., None] * BLOCK_SLC + jnp.arange(BLOCK_SLC, dtype=jnp.int32)).reshape(B, N_KV_HEADS, Tq, n_sel * BLOCK_SLC)
    s_s = jnp.einsum('bqghd,bgqkd->bghqk', qg, g_kv[..., 0, :])
    p_s = masked_softmax(s_s, (k_pos <= q_pos[:, None])[:, :, None])
    o_s = jnp.einsum('bghqk,bgqkd->bqghd', p_s, g_kv[..., 1, :])
    s_w = jnp.einsum('bqghd,bkgd->bghqk', qg, win_kv[:, :, 0].astype(f32))
    dist = q_pos[:, None] - w_pos[None, :]
    p_w = masked_softmax(s_w, (dist >= 0) & (dist < WINDOW) & (w_pos[None, :] >= 0))
    o_w = jnp.einsum('bghqk,bkgd->bqghd', p_w, win_kv[:, :, 1].astype(f32))
    g = gates.astype(f32).reshape(B, Tq, N_KV_HEADS, HEADS_PER_KV, N_BRANCH)
    o = g[..., 0:1] * o_c + g[..., 1:2] * o_s + g[..., 2:3] * o_w
    return o.reshape(B, Tq, D_ATTN).astype(q.dtype)


def prompt_attention(q, kvs, gates, lp):
    B, T = q.shape[:2]
    kc, vc, c_end = compress_blocks(kvs[0], lp['cmp_pe'], lp['cmp_w'])
    slc_blocks = to_slc_blocks(kvs[1])
    sel_map = cmp_to_slc_map(kc.shape[1], slc_blocks.shape[2])
    win_pad = jnp.pad(kvs[2], ((0, 0), (WINDOW, 0), (0, 0), (0, 0), (0, 0)))

    def one_query_block(q0):
        q_pos = q0 + jnp.arange(QBLOCK, dtype=jnp.int32)
        w_pos = q0 - WINDOW + jnp.arange(WINDOW + QBLOCK, dtype=jnp.int32)
        return nsa_query_block(
            lax.dynamic_slice_in_dim(q, q0, QBLOCK, axis=1), q_pos, kc, vc, c_end, sel_map, slc_blocks,
            lax.dynamic_slice_in_dim(win_pad, q0, WINDOW + QBLOCK, axis=1), w_pos,
            lax.dynamic_slice_in_dim(gates, q0, QBLOCK, axis=1))

    out = lax.map(one_query_block, jnp.arange(0, T, QBLOCK, dtype=jnp.int32))
    out = jnp.moveaxis(out, 0, 1).reshape(B, T, D_ATTN)
    cmp_pages = kvs[0].reshape(B, T // PAGE_SIZE, PAGE_SIZE, 2, N_KV_HEADS, HEAD_DIM)
    slc_pages = kvs[1].reshape(B, T // PAGE_SIZE, PAGE_SIZE, 2, N_KV_HEADS, HEAD_DIM)
    return out, (cmp_pages, slc_pages, kvs[2][:, -min(WINDOW, T):])


def sample_attention(q, kvs, gates, lp, cache_cmp, cache_slc, cache_win, page_table):
    Tn = q.shape[1]
    past = page_table.shape[1] * PAGE_SIZE
    w_buf = cache_win.shape[1]
    full_cmp = jnp.concatenate([gather_pages(cache_cmp, page_table), kvs[0]], axis=1)
    full_slc = jnp.concatenate([gather_pages(cache_slc, page_table), kvs[1]], axis=1)
    win = jnp.concatenate([cache_win, kvs[2]], axis=1)
    kc, vc, c_end = compress_blocks(full_cmp, lp['cmp_pe'], lp['cmp_w'])
    slc_blocks = to_slc_blocks(full_slc)
    sel_map = cmp_to_slc_map(kc.shape[1], slc_blocks.shape[2])
    q_pos = past + jnp.arange(Tn, dtype=jnp.int32)
    w_pos = past - w_buf + jnp.arange(w_buf + Tn, dtype=jnp.int32)
    out = nsa_query_block(q, q_pos, kc, vc, c_end, sel_map, slc_blocks, win, w_pos, gates)
    return out, (kvs[0], kvs[1], win[:, -w_buf:])


def rwkv7_time_mix(zs, shift_prev, wkv0, v_first, lp):
    B, T = zs.shape[:2]
    zf = zs.astype(f32)
    prev = jnp.concatenate([shift_prev.astype(f32)[:, None], zf[:, :-1]], axis=1)
    xs = zf + (prev - zf) * lp['mu']
    r, k, v, xw, xa, xg = _split(xs, [D_RWKV] * 3 + [RANK_W, RANK_A, RANK_G])
    w_raw = -jax.nn.softplus(-(lp['w0'] + jnp.tanh(xw) @ lp['w2'])) - 0.5
    decay = jnp.exp(-jnp.exp(w_raw))
    a = jax.nn.sigmoid(lp['a0'] + xa @ lp['a2'])
    if lp['vres'] is None:
        v_first = v
    else:
        vres0, vres2 = lp['vres']
        v = v + (v_first - v) * jax.nn.sigmoid(vres0 + xa @ vres2)
    g = jax.nn.sigmoid(xg) @ lp['g2']
    heads = lambda t: t.reshape(B, T, N_HEADS_RWKV, HEAD_DIM)
    kk = heads(k * lp['k_k'])
    kk = kk * lax.rsqrt(jnp.maximum(jnp.sum(kk * kk, axis=-1, keepdims=True), 1e-24))
    k = k * (1.0 + (a - 1.0) * lp['k_a'])
    r_h, k_h, v_h, a_h, d_h = heads(r), heads(k), heads(v), heads(a), heads(decay)

    def step(S, inp):
        r_t, d_t, kk_t, a_t, k_t, v_t = inp
        sa = jnp.einsum('bhvk,bhk->bhv', S, -kk_t)
        S = S * d_t[:, :, None, :] + sa[..., None] * (kk_t * a_t)[:, :, None, :] + v_t[..., None] * k_t[:, :, None, :]
        return S, jnp.einsum('bhvk,bhk->bhv', S, r_t)

    tm = lambda t: jnp.moveaxis(t, 1, 0)
    S_T, ys = lax.scan(step, wkv0.astype(f32), (tm(r_h), tm(d_h), tm(kk), tm(a_h), tm(k_h), tm(v_h)))
    y = jnp.moveaxis(ys, 0, 1)
    mean = jnp.mean(y, axis=-1, keepdims=True)
    var = jnp.mean(jnp.square(y - mean), axis=-1, keepdims=True)
    y = (y - mean) * lax.rsqrt(var + GN_EPS) * lp['lnx_w'].reshape(N_HEADS_RWKV, HEAD_DIM) + lp['lnx_b'].reshape(N_HEADS_RWKV, HEAD_DIM)
    y = y + jnp.sum(r_h * k_h * lp['r_k'], axis=-1, keepdims=True) * v_h
    y = y.reshape(B, T, D_RWKV) * g
    return y, S_T, zs[:, -1], v_first


def trunk_layer(x, pos, attn_fn, shift_prev, wkv0, v_first, lp):
    B, T = x.shape[:2]
    x2 = x.reshape(B * T, D_MODEL)
    z = norm_proj(x2, lp['norm1_g'], lp['w_in']).reshape(B, T, IN_COLS_PAD)
    z_attn, z_gate, z_rwkv = split_z(z)
    q, kvs = nsa_heads(z_attn, pos, lp['q_norm_g'], lp['k_norm_g'])
    gates = jax.nn.sigmoid((z_gate + lp['gate_b']).astype(f32))
    o_attn, attn_state = attn_fn(q, kvs, gates, lp)
    o_rwkv, wkv_t, shift_t, v_first = rwkv7_time_mix(z_rwkv, shift_prev, wkv0, v_first, lp)
    h = out_proj(x2, o_attn.reshape(B * T, D_ATTN), o_rwkv.reshape(B * T, D_RWKV), lp['w_out'])
    y = ffn(h, lp['norm2_g'], lp['w_up'], lp['w_down']).reshape(B, T, D_MODEL)
    return y, attn_state, wkv_t, shift_t, v_first


def sample_trunk_layer(x, layer, shift_prev, wkv0, v_first, lp, consts, caches):
    B, Tn = x.shape[:2]
    rows = B * Tn
    page_table = caches['page_table']
    x2 = x.reshape(rows, D_MODEL)
    z = norm_proj(x2, lp['norm1_g'], lp['w_in'])
    q_ext, kv_c, kv_s, kv_w, _, _, gates = attn_prep(
        z, consts['rope_s'], consts['seg_ones'], lp['q_g_lanes'], lp['k_g_lanes'], lp['gate_b_lanes'], 1, rows)
    q_rows = q_ext.reshape(N_HEADS_ATTN, B, Tn, LANES).transpose(1, 0, 2, 3).reshape(B, N_HEADS_ATTN * Tn, LANES)
    n_pool = caches['slc'].shape[1]
    y_req = caches['y_pool'][layer].reshape(n_pool, CHUNKS_PER_PAGE, 2 * KV_COLS)[page_table]
    y_req = y_req.reshape(B, page_table.shape[1] * CHUNKS_PER_PAGE, 2 * KV_COLS)
    new_rows = lambda kv: kv.reshape(B, Tn, KV_COLS)
    o_attn = sample_attn(q_rows, gates.reshape(B, Tn, LANES), y_req, caches['slc'], layer, page_table,
                         new_rows(kv_s), new_rows(kv_w), caches['win'], consts['smap_s'])
    tok = lambda kv: kv.reshape(B, Tn, 2, N_KV_HEADS, HEAD_DIM)
    win = jnp.concatenate([caches['win_raw'][layer][:, Tn:], tok(kv_w)], axis=1)
    attn_state = (tok(kv_c), tok(kv_s), win)
    z_rwkv = split_z(z)[2].reshape(B, Tn, SHIFT_COLS)
    o_rwkv, wkv_t, shift_t, v_first = rwkv7_time_mix(z_rwkv, shift_prev, wkv0, v_first, lp)
    h = out_proj(x2, o_attn.reshape(rows, D_ATTN), o_rwkv.reshape(rows, D_RWKV), lp['w_out'])
    y = ffn(h, lp['norm2_g'], lp['w_up'], lp['w_down']).reshape(B, Tn, D_MODEL)
    return y, attn_state, wkv_t, shift_t, v_first


def prompt_trunk_layer(x, v_first, lp, consts):
    B, T = x.shape[:2]
    x2 = x.reshape(B * T, D_MODEL)
    z = norm_proj(x2, lp['norm1_g'], lp['w_in'])
    q_ext, kv_c, kv_s, kv_w, kb, vvb, gates = attn_prep(
        z, consts['rope'], consts['seg_ones'], lp['q_g_lanes'], lp['k_g_lanes'], lp['gate_b_lanes'], B, T)
    kc, vvc = compress(kv_c, lp['cmp_wb'], lp['cmp_peb'], B)
    o_attn = nsa_prompt(q_ext, gates, kc, vvc, kb, vvb, consts['nsa'], B, T)
    pages = lambda kv: kv.reshape(B, T // PAGE_SIZE, PAGE_SIZE, 2, N_KV_HEADS, HEAD_DIM)
    attn_state = (pages(kv_c), pages(kv_s),
                  kv_w.reshape(B, T, 2, N_KV_HEADS, HEAD_DIM)[:, -min(WINDOW, T):])
    o_rwkv, st, v_first = rwkv_prompt(z, v_first, lp, consts, B, T)
    z_last = z.reshape(B, T, IN_COLS_PAD)[:, -1]
    shift_t = jnp.concatenate([z_last[:, Z_R:Z_KV], z_last[:, Z_LR:]], axis=-1)
    h = out_proj(x2, o_attn.reshape(B * T, D_ATTN), o_rwkv, lp['w_out'])
    y = ffn(h, lp['norm2_g'], lp['w_up'], lp['w_down']).reshape(B, T, D_MODEL)
    return y, attn_state, state_from_pair_tiles(st), shift_t, v_first


def kernel(x_prompt, x_sample, cache_cmp_kv, cache_slc_kv, cache_win_kv, state_wkv, state_shift, page_table,
           norm1_g, w_in, q_norm_g, k_norm_g, gate_b, cmp_pe, cmp_w, shift_mu, w0, w2, a0, a2, vres0, vres2,
           g2, k_k, k_a, r_k, lnx_w, lnx_b, w_out, norm2_g, w_up, w_down):
    B, T = x_prompt.shape[:2]
    Tn = x_sample.shape[1]
    past = page_table.shape[1] * PAGE_SIZE
    pos_p = jnp.arange(T, dtype=jnp.int32)
    pos_s = past + jnp.arange(Tn, dtype=jnp.int32)
    w_in_b = permute_w_in(w_in).astype(bf16)
    w_out_b, w_up_b, w_down_b = w_out.astype(bf16), w_up.astype(bf16), w_down.astype(bf16)
    seg = np.arange(LANES) // HEAD_DIM
    consts = {'rope': rope_lane_tables(pos_p), 'nsa': nsa_tables(T),
              'seg_ones': jnp.asarray(seg[:, None] == seg[None, :], bf16),
              'tri': jnp.asarray(np.tril(np.ones((RW_CHUNK, RW_CHUNK))), bf16),
              'shift_zero': jnp.zeros((B, 1, SHIFT_COLS), f32)}
    rwkv_raw = {'w0': w0, 'w2': w2, 'a0': a0, 'a2': a2, 'vres0': vres0, 'vres2': vres2, 'g2': g2, 'k_k': k_k,
                'k_a': k_a, 'r_k': r_k, 'lnx_w': lnx_w, 'lnx_b': lnx_b}
    Bd = x_sample.shape[0]
    n_pool, w_buf = cache_slc_kv.shape[1], cache_win_kv.shape[2]
    consts['rope_s'] = tuple(jnp.tile(t, (Bd, 1)) for t in rope_lane_tables(pos_s))
    consts['smap_s'] = sample_tables(past, Tn)
    cmp_wb, cmp_peb = jax.vmap(compress_weights)(cmp_pe, cmp_w)
    caches = {'page_table': page_table,
              'slc': cache_slc_kv.reshape(DEPTH, n_pool, PAGE_SIZE, KV_COLS),
              'win': cache_win_kv.reshape(DEPTH, Bd, w_buf, KV_COLS), 'win_raw': cache_win_kv,
              'y_pool': compress_pool(cache_cmp_kv, cmp_wb, cmp_peb)}
    xp, xs = x_prompt, x_sample
    vf_p, vf_s = None, None
    p_cmp, p_slc, p_win, p_wkv, p_shift = [], [], [], [], []
    s_cmp, s_slc, s_win, s_wkv, s_shift = [], [], [], [], []
    for l in range(DEPTH):
        lp = {'norm1_g': norm1_g[l], 'w_in': w_in_b[l], 'q_norm_g': q_norm_g[l], 'k_norm_g': k_norm_g[l],
              'gate_b': gate_b[l], 'cmp_pe': cmp_pe[l], 'cmp_w': cmp_w[l], 'mu': shift_mu[l], 'w0': w0[l],
              'w2': w2[l], 'a0': a0[l], 'a2': a2[l],
              'vres': None if l == 0 else (vres0[l - 1], vres2[l - 1]),
              'g2': g2[l], 'k_k': k_k[l], 'k_a': k_a[l], 'r_k': r_k[l], 'lnx_w': lnx_w[l], 'lnx_b': lnx_b[l],
              'w_out': w_out_b[l], 'norm2_g': norm2_g[l], 'w_up': w_up_b[l], 'w_down': w_down_b[l]}
        lp['cmp_wb'], lp['cmp_peb'] = cmp_wb[l], cmp_peb[l]
        lp['q_g_lanes'] = jnp.tile(q_norm_g[l], 2).reshape(1, LANES)
        lp['k_g_lanes'] = jnp.tile(k_norm_g[l], (1, 2))
        lp['gate_b_lanes'] = jnp.pad(gate_b[l], (0, LANES - N_BRANCH * N_HEADS_ATTN)).reshape(1, LANES)
        lp['mu_row'] = shift_mu[l].reshape(1, SHIFT_COLS)
        lp['rwkv_vecs'], lp['rwkv_lowrank'] = rwkv_params(rwkv_raw, l)
        xp, (c_kv, sl_kv, w_kv), wkv_t, sh_t, vf_p = prompt_trunk_layer(xp, vf_p, lp, consts)
        p_cmp.append(c_kv); p_slc.append(sl_kv); p_win.append(w_kv); p_wkv.append(wkv_t); p_shift.append(sh_t)
        xs, (c_kv, sl_kv, w_kv), wkv_t, sh_t, vf_s = sample_trunk_layer(
            xs, l, state_shift[l], state_wkv[l], vf_s, lp, consts, caches)
        s_cmp.append(c_kv); s_slc.append(sl_kv); s_win.append(w_kv); s_wkv.append(wkv_t); s_shift.append(sh_t)
    return (xp, xs, jnp.stack(p_cmp), jnp.stack(p_slc), jnp.stack(p_win), jnp.stack(p_wkv), jnp.stack(p_shift),
            jnp.stack(s_cmp), jnp.stack(s_slc), jnp.stack(s_win), jnp.stack(s_wkv), jnp.stack(s_shift))
```

```python
import functools
import jax, jax.numpy as jnp
from jax import lax
import numpy as np
from jax.experimental import pallas as pl
from jax.experimental.pallas import tpu as pltpu

D_MODEL = 1024
DEPTH = 4
PAGE_SIZE = 128
HEAD_DIM = 64
N_HEADS_ATTN = 8
N_KV_HEADS = 2
HEADS_PER_KV = N_HEADS_ATTN // N_KV_HEADS
D_ATTN = N_HEADS_ATTN * HEAD_DIM
D_KV = N_KV_HEADS * HEAD_DIM
N_BRANCH = 3
N_HEADS_RWKV = 8
D_RWKV = N_HEADS_RWKV * HEAD_DIM
ROT_DIM = HEAD_DIM // 4
ROPE_THETA = 500000.0
BLOCK_CMP = 32
STRIDE_CMP = 16
BLOCK_SLC = 64
N_SELECT = 16
N_LOCAL = 2
WINDOW = 512
QBLOCK = 128
RANK_W = 32
RANK_A = 32
RANK_G = 64
D_FF = 4 * D_MODEL
SHIFT_COLS = 3 * D_RWKV + RANK_W + RANK_A + RANK_G
IN_COLS = D_ATTN + 6 * D_KV + N_BRANCH * N_HEADS_ATTN + SHIFT_COLS
IN_COLS_PAD = 3072
NORM_EPS = 1e-6
GN_EPS = 64e-5
NEG = -1e30
FORCED_SCORE = 1e9

VMEM_LIMIT = 48 * 1024 * 1024
bf16 = jnp.bfloat16
f32 = jnp.float32


def _row_tile(m):
    return min(m, 1024)


def _norm_proj_kernel(x_ref, g_ref, w_ref, o_ref):
    x = x_ref[...]
    n = x * lax.rsqrt(jnp.mean(x * x, axis=-1, keepdims=True) + NORM_EPS) * g_ref[...]
    o_ref[...] = jnp.dot(n.astype(bf16), w_ref[...], preferred_element_type=f32)


def norm_proj(x, g, w):
    m, n = x.shape[0], w.shape[1]
    tm, tn = _row_tile(m), 512
    return pl.pallas_call(
        _norm_proj_kernel,
        grid=(m // tm, n // tn),
        in_specs=[pl.BlockSpec((tm, D_MODEL), lambda i, j: (i, 0)),
                  pl.BlockSpec((1, D_MODEL), lambda i, j: (0, 0)),
                  pl.BlockSpec((D_MODEL, tn), lambda i, j: (0, j))],
        out_specs=pl.BlockSpec((tm, tn), lambda i, j: (i, j)),
        out_shape=jax.ShapeDtypeStruct((m, n), f32),
        compiler_params=pltpu.CompilerParams(
            dimension_semantics=("parallel", "arbitrary"), vmem_limit_bytes=VMEM_LIMIT),
        name="norm_proj",
    )(x, g.reshape(1, D_MODEL), w)


def _out_proj_kernel(x_ref, a_ref, r_ref, w_ref, o_ref):
    o_ref[...] = (x_ref[...]
                  + jnp.dot(a_ref[...].astype(bf16), w_ref[:D_ATTN, :], preferred_element_type=f32)
                  + jnp.dot(r_ref[...].astype(bf16), w_ref[D_ATTN:, :], preferred_element_type=f32))


def out_proj(x, o_attn, o_rwkv, w):
    m = x.shape[0]
    tm = _row_tile(m)
    return pl.pallas_call(
        _out_proj_kernel,
        grid=(m // tm,),
        in_specs=[pl.BlockSpec((tm, D_MODEL), lambda i: (i, 0)),
                  pl.BlockSpec((tm, D_ATTN), lambda i: (i, 0)),
                  pl.BlockSpec((tm, D_RWKV), lambda i: (i, 0)),
                  pl.BlockSpec((D_MODEL, D_MODEL), lambda i: (0, 0))],
        out_specs=pl.BlockSpec((tm, D_MODEL), lambda i: (i, 0)),
        out_shape=jax.ShapeDtypeStruct((m, D_MODEL), f32),
        compiler_params=pltpu.CompilerParams(
            dimension_semantics=("parallel",), vmem_limit_bytes=VMEM_LIMIT),
        name="out_proj",
    )(x, o_attn, o_rwkv, w)


def _ffn_kernel(h_ref, g_ref, wu_ref, wd_ref, o_ref, n_ref):
    j = pl.program_id(1)

    @pl.when(j == 0)
    def _():
        h = h_ref[...]
        n = h * lax.rsqrt(jnp.mean(h * h, axis=-1, keepdims=True) + NORM_EPS) * g_ref[...]
        n_ref[...] = n.astype(bf16)
        o_ref[...] = h

    u = jnp.dot(n_ref[...], wu_ref[...], preferred_element_type=f32)
    u = jnp.square(jnp.maximum(u, 0.0))
    o_ref[...] += jnp.dot(u.astype(bf16), wd_ref[...], preferred_element_type=f32)


def ffn(h, g, w_up, w_down):
    m = h.shape[0]
    tm, tf = _row_tile(m), 512
    return pl.pallas_call(
        _ffn_kernel,
        grid=(m // tm, D_FF // tf),
        in_specs=[pl.BlockSpec((tm, D_MODEL), lambda i, j: (i, 0)),
                  pl.BlockSpec((1, D_MODEL), lambda i, j: (0, 0)),
                  pl.BlockSpec((D_MODEL, tf), lambda i, j: (0, j)),
                  pl.BlockSpec((tf, D_MODEL), lambda i, j: (j, 0))],
        out_specs=pl.BlockSpec((tm, D_MODEL), lambda i, j: (i, 0)),
        out_shape=jax.ShapeDtypeStruct((m, D_MODEL), f32),
        scratch_shapes=[pltpu.VMEM((tm, D_MODEL), bf16)],
        compiler_params=pltpu.CompilerParams(
            dimension_semantics=("parallel", "arbitrary"), vmem_limit_bytes=VMEM_LIMIT),
        name="ffn",
    )(h, g.reshape(1, D_MODEL), w_up, w_down)


LANES = 128
KV_COLS = 2 * D_KV
N_GATES = N_BRANCH * N_HEADS_ATTN
N_LOWRANK = RANK_W + RANK_A + RANK_G
Z_Q, Z_R, Z_K, Z_V = 0, D_ATTN, D_ATTN + D_RWKV, D_ATTN + 2 * D_RWKV
Z_KV = D_ATTN + 3 * D_RWKV
Z_GATE = Z_KV + N_BRANCH * KV_COLS
Z_LR = Z_GATE + LANES
assert N_LOWRANK == LANES and Z_LR + N_LOWRANK == IN_COLS_PAD and N_GATES <= LANES


def permute_w_in(w_in):
    o_kv, o_gate, o_rwkv = D_ATTN, D_ATTN + 6 * D_KV, D_ATTN + 6 * D_KV + N_GATES
    pad = jnp.zeros(w_in.shape[:-1] + (LANES - N_GATES,), w_in.dtype)
    return jnp.concatenate([w_in[..., :o_kv], w_in[..., o_rwkv:o_rwkv + 3 * D_RWKV], w_in[..., o_kv:o_gate],
                            w_in[..., o_gate:o_rwkv], pad, w_in[..., o_rwkv + 3 * D_RWKV:]], axis=-1)


def split_z(z):
    z_attn = jnp.concatenate([z[..., :D_ATTN], z[..., Z_KV:Z_GATE]], axis=-1)
    z_rwkv = jnp.concatenate([z[..., Z_R:Z_KV], z[..., Z_LR:]], axis=-1)
    return z_attn, z[..., Z_GATE:Z_GATE + N_GATES], z_rwkv


def _seg_sumsq(x, p_ref):
    x2 = x * x
    hi = x2.astype(bf16)
    lo = (x2 - hi.astype(f32)).astype(bf16)
    return (jnp.dot(hi, p_ref[...], preferred_element_type=f32)
            + jnp.dot(lo, p_ref[...], preferred_element_type=f32))


def _norm_rope(x, g, p_ref, c, s1, s2):
    y = x * lax.rsqrt(_seg_sumsq(x, p_ref) * (1.0 / HEAD_DIM) + NORM_EPS) * g
    return y * c + pltpu.roll(y, LANES - ROT_DIM // 2, 1) * s1 + pltpu.roll(y, ROT_DIM // 2, 1) * s2


def _attn_prep_kernel(zq_ref, zc_ref, zs_ref, zw_ref, zg_ref, c_ref, s1_ref, s2_ref, p_ref, qg_ref, kg_ref, gb_ref,
                      q_ref, kvc_ref, kvs_ref, kvw_ref, kb_ref, vvb_ref, gate_ref):
    c, s1, s2 = c_ref[...], s1_ref[...], s2_ref[...]
    lane = lax.broadcasted_iota(jnp.int32, (zq_ref.shape[0], LANES), 1)
    for pair in range(N_HEADS_ATTN // 2):
        y = _norm_rope(zq_ref[:, pair * LANES:(pair + 1) * LANES], qg_ref[...], p_ref, c, s1, s2)
        y = y * (HEAD_DIM ** -0.5)
        y_sw = pltpu.roll(y, HEAD_DIM, 1)
        for half in range(2):
            h = 2 * pair + half
            grp = h // HEADS_PER_KV
            src = y if half == grp else y_sw
            keep = (lane >= HEAD_DIM) if grp == 1 else (lane < HEAD_DIM)
            q_ref[0, h] = jnp.where(keep, src, 0.0).astype(bf16)
    ins = (zc_ref, zs_ref, zw_ref)
    outs = (kvc_ref, kvs_ref, kvw_ref)
    for br in range(N_BRANCH):
        k = _norm_rope(ins[br][:, :D_KV], kg_ref[br:br + 1, :], p_ref, c, s1, s2)
        v = ins[br][:, D_KV:]
        outs[br][:, :D_KV] = k
        outs[br][:, D_KV:] = v
        if br > 0:
            kb_ref[br - 1] = k.astype(bf16)
            vvb_ref[br - 1, :, :D_KV] = v.astype(bf16)
            vvb_ref[br - 1, :, D_KV:] = pltpu.roll(v, HEAD_DIM, 1).astype(bf16)
    gate_ref[...] = jax.nn.sigmoid(zg_ref[...] + gb_ref[...])


def rope_lane_tables(pos):
    half = ROT_DIM // 2
    inv = ROPE_THETA ** (-(jnp.arange(0, ROT_DIM, 2, dtype=f32) / ROT_DIM))
    ang = pos.astype(f32)[:, None] * inv[None, :]
    cos, sin = jnp.cos(ang), jnp.sin(ang)
    t = pos.shape[0]
    one, zero = jnp.ones((t, HEAD_DIM - ROT_DIM), f32), jnp.zeros((t, HEAD_DIM - ROT_DIM), f32)
    zh = jnp.zeros((t, half), f32)
    c = jnp.concatenate([cos, cos, one], axis=1)
    s1 = jnp.concatenate([-sin, zh, zero], axis=1)
    s2 = jnp.concatenate([zh, sin, zero], axis=1)
    return tuple(jnp.tile(a, (1, 2)) for a in (c, s1, s2))


def attn_prep(z, rope_tabs, seg_ones, q_g, k_g, gate_b, batch, seq):
    m = batch * seq
    tm = min(512, seq)
    tiles = seq // tm
    c, s1, s2 = rope_tabs
    row = lambda i: (i, 0)
    tab = lambda i: (i % tiles, 0)
    const = lambda i: (0, 0)
    kv_shape = jax.ShapeDtypeStruct((m, KV_COLS), f32)
    return pl.pallas_call(
        _attn_prep_kernel,
        grid=(m // tm,),
        in_specs=[pl.BlockSpec((tm, D_ATTN), row),
                  pl.BlockSpec((tm, KV_COLS), lambda i: (i, Z_KV // KV_COLS)),
                  pl.BlockSpec((tm, KV_COLS), lambda i: (i, Z_KV // KV_COLS + 1)),
                  pl.BlockSpec((tm, KV_COLS), lambda i: (i, Z_KV // KV_COLS + 2)),
                  pl.BlockSpec((tm, LANES), lambda i: (i, Z_GATE // LANES)),
                  pl.BlockSpec((tm, LANES), tab), pl.BlockSpec((tm, LANES), tab), pl.BlockSpec((tm, LANES), tab),
                  pl.BlockSpec((LANES, LANES), const),
                  pl.BlockSpec((1, LANES), const), pl.BlockSpec((N_BRANCH, LANES), const),
                  pl.BlockSpec((1, LANES), const)],
        out_specs=[pl.BlockSpec((1, N_HEADS_ATTN, tm, LANES), lambda i: (i // tiles, 0, i % tiles, 0)),
                   pl.BlockSpec((tm, KV_COLS), row), pl.BlockSpec((tm, KV_COLS), row),
                   pl.BlockSpec((tm, KV_COLS), row),
                   pl.BlockSpec((2, tm, D_KV), lambda i: (0, i, 0)),
                   pl.BlockSpec((2, tm, KV_COLS), lambda i: (0, i, 0)),
                   pl.BlockSpec((tm, LANES), row)],
        out_shape=[jax.ShapeDtypeStruct((batch, N_HEADS_ATTN, seq, LANES), bf16),
                   kv_shape, kv_shape, kv_shape,
                   jax.ShapeDtypeStruct((2, m, D_KV), bf16),
                   jax.ShapeDtypeStruct((2, m, KV_COLS), bf16),
                   jax.ShapeDtypeStruct((m, LANES), f32)],
        compiler_params=pltpu.CompilerParams(
            dimension_semantics=("parallel",), vmem_limit_bytes=VMEM_LIMIT),
        name="attn_prep",
    )(z, z, z, z, z, c, s1, s2, seg_ones, q_g, k_g, gate_b)


CHUNK_COLS = STRIDE_CMP * KV_COLS
CMP_R = BLOCK_CMP // STRIDE_CMP


def compress_weights(cmp_pe, cmp_w):
    pe = cmp_pe.reshape(2, CMP_R, STRIDE_CMP, HEAD_DIM)
    w = cmp_w.reshape(2, CMP_R, STRIDE_CMP, HEAD_DIM, HEAD_DIM)
    eye_c, eye_g = jnp.eye(2, dtype=f32), jnp.eye(N_KV_HEADS, dtype=f32)
    wb = jnp.einsum('cisde,cx,gy->iscgdxye', w, eye_c, eye_g).reshape(CMP_R, CHUNK_COLS, KV_COLS)
    peb = jnp.broadcast_to(jnp.transpose(pe, (1, 2, 0, 3))[:, :, :, None, :],
                           (CMP_R, STRIDE_CMP, 2, N_KV_HEADS, HEAD_DIM)).reshape(CMP_R, 1, CHUNK_COLS)
    return wb.astype(bf16), peb


def _compress_kernel(x_ref, pe_ref, w_ref, kc_ref, vvc_ref):
    x = x_ref[0]
    n = x.shape[0]
    y0 = jnp.dot((x + pe_ref[0]).astype(bf16), w_ref[0], preferred_element_type=f32)
    y1 = jnp.dot((x + pe_ref[1]).astype(bf16), w_ref[1], preferred_element_type=f32)
    comp = y0 + pltpu.roll(y1, n - 1, 0)
    row = lax.broadcasted_iota(jnp.int32, comp.shape, 0)
    comp = jnp.where(row < n - 1, comp, 0.0)
    kc_ref[0] = comp[:, :D_KV].astype(bf16)
    vc = comp[:, D_KV:]
    vvc_ref[0, :, :D_KV] = vc.astype(bf16)
    vvc_ref[0, :, D_KV:] = pltpu.roll(vc, HEAD_DIM, 1).astype(bf16)


def compress(kv, wb, peb, batch):
    n_chunks = kv.shape[0] // batch // STRIDE_CMP
    x = kv.reshape(batch, n_chunks, CHUNK_COLS)
    return pl.pallas_call(
        _compress_kernel,
        grid=(batch,),
        in_specs=[pl.BlockSpec((1, n_chunks, CHUNK_COLS), lambda b: (b, 0, 0)),
                  pl.BlockSpec((CMP_R, 1, CHUNK_COLS), lambda b: (0, 0, 0)),
                  pl.BlockSpec((CMP_R, CHUNK_COLS, KV_COLS), lambda b: (0, 0, 0))],
        out_specs=[pl.BlockSpec((1, n_chunks, D_KV), lambda b: (b, 0, 0)),
                   pl.BlockSpec((1, n_chunks, KV_COLS), lambda b: (b, 0, 0))],
        out_shape=[jax.ShapeDtypeStruct((batch, n_chunks, D_KV), bf16),
                   jax.ShapeDtypeStruct((batch, n_chunks, KV_COLS), bf16)],
        compiler_params=pltpu.CompilerParams(
            dimension_semantics=("parallel",), vmem_limit_bytes=VMEM_LIMIT),
        name="compress",
    )(x, peb, wb)


KEY_CHUNK = 512
NT_DIMS = (((1,), (1,)), ((), ()))


def _softmax_rows(s3, mask):
    s3 = jnp.where(mask[None], s3, NEG)
    m = jnp.max(s3, axis=-1, keepdims=True)
    p = jnp.where(mask[None], jnp.exp(s3 - m), 0.0)
    l = jnp.sum(p, axis=-1, keepdims=True)
    return p * jnp.where(l > 0.0, 1.0 / l, 0.0)


def _nsa_prompt_kernel(q_ref, gate_ref, kc_ref, vvc_ref, ks_ref, vvs_ref, kw_ref, vvw_ref, smap_ref, e_ref,
                       o_ref, mask_ref, m_ref, l_ref, acc_ref, *, seq):
    n_slc = seq // BLOCK_SLC
    n_cmp = seq // STRIDE_CMP
    hq = HEADS_PER_KV * QBLOCK
    q0 = pl.program_id(1) * QBLOCK
    qpos_col = q0 + lax.broadcasted_iota(jnp.int32, (QBLOCK, 1), 0)

    heads_out = []
    for g in range(N_KV_HEADS):
        q = q_ref[0, g * HEADS_PER_KV:(g + 1) * HEADS_PER_KV].reshape(hq, LANES)

        s = lax.dot_general(q, kc_ref[0], NT_DIMS, preferred_element_type=f32)
        c_end = lax.broadcasted_iota(jnp.int32, (1, n_cmp), 1) * STRIDE_CMP + (BLOCK_CMP - 1)
        p_c = _softmax_rows(s.reshape(HEADS_PER_KV, QBLOCK, n_cmp), c_end <= qpos_col)
        p_c = p_c.astype(bf16).reshape(hq, n_cmp)
        o_c = jnp.dot(p_c, vvc_ref[0], preferred_element_type=f32)

        imp = jnp.zeros((n_slc, QBLOCK), f32)
        for h in range(HEADS_PER_KV):
            imp = imp + lax.dot_general(smap_ref[...], p_c[h * QBLOCK:(h + 1) * QBLOCK], NT_DIMS,
                                        preferred_element_type=f32)
        jj = lax.broadcasted_iota(jnp.int32, (n_slc, QBLOCK), 0)
        qp = q0 + lax.broadcasted_iota(jnp.int32, (n_slc, QBLOCK), 1)
        cur = qp // BLOCK_SLC
        forced = (jj == 0) | ((jj <= cur) & (jj > cur - N_LOCAL))
        imp = jnp.where(jj * BLOCK_SLC <= qp, jnp.where(forced, FORCED_SCORE, imp), NEG)
        rank = jnp.zeros((n_slc, QBLOCK), f32)
        for j in range(n_slc):
            row = imp[j:j + 1, :]
            rank = rank + jnp.where((row > imp) | ((row == imp) & (jj > j)), 1.0, 0.0)
        sel_t = jnp.where(rank < float(min(N_SELECT, n_slc)), 1.0, 0.0)
        if n_slc < LANES:
            sel_t = jnp.concatenate([sel_t, jnp.zeros((LANES - n_slc, QBLOCK), f32)], axis=0)
        sel = sel_t.T.astype(bf16)
        for c in range(seq // KEY_CHUNK):
            mask_ref[c] = jnp.dot(sel, e_ref[:, c * KEY_CHUNK:(c + 1) * KEY_CHUNK], preferred_element_type=f32)

        m_ref[...] = jnp.full(m_ref.shape, NEG, f32)
        l_ref[...] = jnp.zeros(l_ref.shape, f32)
        acc_ref[...] = jnp.zeros(acc_ref.shape, f32)

        def sel_step(c, carry):
            k0 = pl.multiple_of(c * KEY_CHUNK, KEY_CHUNK)
            sc = lax.dot_general(q, ks_ref[0, pl.ds(k0, KEY_CHUNK), :], NT_DIMS, preferred_element_type=f32)
            kpos = k0 + lax.broadcasted_iota(jnp.int32, (1, KEY_CHUNK), 1)
            msk = ((mask_ref[c] > 0.5) & (kpos <= qpos_col))[None]
            s3 = jnp.where(msk, sc.reshape(HEADS_PER_KV, QBLOCK, KEY_CHUNK), NEG)
            m_old = m_ref[...]
            m_new = jnp.maximum(m_old, jnp.max(s3, axis=-1, keepdims=True))
            alpha = jnp.exp(m_old - m_new)
            p = jnp.where(msk, jnp.exp(s3 - m_new), 0.0)
            l_ref[...] = alpha * l_ref[...] + jnp.sum(p, axis=-1, keepdims=True)
            pv = jnp.dot(p.astype(bf16).reshape(hq, KEY_CHUNK), vvs_ref[0, pl.ds(k0, KEY_CHUNK), :],
                         preferred_element_type=f32)
            acc_ref[...] = alpha * acc_ref[...] + pv.reshape(HEADS_PER_KV, QBLOCK, KV_COLS)
            m_ref[...] = m_new
            return carry

        lax.fori_loop(0, q0 // KEY_CHUNK + 1, sel_step, 0)
        o_s = (acc_ref[...] / l_ref[...]).reshape(hq, KV_COLS)

        n_win = WINDOW + QBLOCK
        w0 = pl.multiple_of(jnp.maximum(q0 - WINDOW, 0), QBLOCK)
        sw = lax.dot_general(q, kw_ref[0, pl.ds(w0, n_win), :], NT_DIMS, preferred_element_type=f32)
        dist = qpos_col - (w0 + lax.broadcasted_iota(jnp.int32, (1, n_win), 1))
        p_w = _softmax_rows(sw.reshape(HEADS_PER_KV, QBLOCK, n_win), (dist >= 0) & (dist < WINDOW))
        o_w = jnp.dot(p_w.astype(bf16).reshape(hq, n_win), vvw_ref[0, pl.ds(w0, n_win), :],
                      preferred_element_type=f32)

        for hh in range(HEADS_PER_KV):
            h = g * HEADS_PER_KV + hh
            half = 0 if (h % 2) == g else 1
            rows = slice(hh * QBLOCK, (hh + 1) * QBLOCK)
            cols = slice(half * LANES, (half + 1) * LANES)
            gc = gate_ref[:, N_BRANCH * h:N_BRANCH * h + 1]
            gs = gate_ref[:, N_BRANCH * h + 1:N_BRANCH * h + 2]
            gw = gate_ref[:, N_BRANCH * h + 2:N_BRANCH * h + 3]
            heads_out.append(gc * o_c[rows, cols] + gs * o_s[rows, cols] + gw * o_w[rows, cols])

    lane = lax.broadcasted_iota(jnp.int32, (QBLOCK, LANES), 1)
    for pair in range(N_HEADS_ATTN // 2):
        o_ref[0, :, pair * LANES:(pair + 1) * LANES] = jnp.where(
            lane < HEAD_DIM, heads_out[2 * pair], heads_out[2 * pair + 1])


def nsa_tables(seq):
    n_slc, n_cmp = seq // BLOCK_SLC, seq // STRIDE_CMP
    start = np.arange(n_cmp) * STRIDE_CMP
    first, last = start // BLOCK_SLC, (start + BLOCK_CMP - 1) // BLOCK_SLC
    j = np.arange(n_slc)
    smap = (first[None, :] <= j[:, None]) & (last[None, :] >= j[:, None]) & (np.arange(n_cmp)[None, :] < n_cmp - 1)
    expand = (np.arange(seq)[None, :] // BLOCK_SLC) == np.arange(max(n_slc, LANES))[:, None]
    return jnp.asarray(smap, bf16), jnp.asarray(expand, bf16)


def nsa_prompt(q_ext, gates, kc, vvc, kb, vvb, tables, batch, seq):
    smap, expand = tables
    n_cmp = seq // STRIDE_CMP
    hq = HEADS_PER_KV * QBLOCK
    per_b = lambda b, i: (b, 0, 0)
    const = lambda b, i: (0, 0)
    gates3 = gates.reshape(batch, seq, LANES)
    kb4 = kb.reshape(2, batch, seq, D_KV)
    vvb4 = vvb.reshape(2, batch, seq, KV_COLS)
    branch = lambda br: (lambda b, i: (br, b, 0, 0))
    return pl.pallas_call(
        functools.partial(_nsa_prompt_kernel, seq=seq),
        grid=(batch, seq // QBLOCK),
        in_specs=[pl.BlockSpec((1, N_HEADS_ATTN, QBLOCK, LANES), lambda b, i: (b, 0, i, 0)),
                  pl.BlockSpec((None, QBLOCK, LANES), lambda b, i: (b, i, 0)),
                  pl.BlockSpec((1, n_cmp, D_KV), per_b), pl.BlockSpec((1, n_cmp, KV_COLS), per_b),
                  pl.BlockSpec((None, 1, seq, D_KV), branch(0)), pl.BlockSpec((None, 1, seq, KV_COLS), branch(0)),
                  pl.BlockSpec((None, 1, seq, D_KV), branch(1)), pl.BlockSpec((None, 1, seq, KV_COLS), branch(1)),
                  pl.BlockSpec(smap.shape, const), pl.BlockSpec(expand.shape, const)],
        out_specs=pl.BlockSpec((1, QBLOCK, D_ATTN), lambda b, i: (b, i, 0)),
        out_shape=jax.ShapeDtypeStruct((batch, seq, D_ATTN), f32),
        scratch_shapes=[pltpu.VMEM((seq // KEY_CHUNK, QBLOCK, KEY_CHUNK), f32),
                        pltpu.VMEM((HEADS_PER_KV, QBLOCK, 1), f32),
                        pltpu.VMEM((HEADS_PER_KV, QBLOCK, 1), f32),
                        pltpu.VMEM((HEADS_PER_KV, QBLOCK, KV_COLS), f32)],
        compiler_params=pltpu.CompilerParams(
            dimension_semantics=("parallel", "arbitrary"), vmem_limit_bytes=VMEM_LIMIT),
        name="nsa_prompt",
    )(q_ext, gates3, kc, vvc, kb4, vvb4, kb4, vvb4, smap, expand)


RW_CHUNK = 64
N_PAIRS = N_HEADS_RWKV // 2
TN_DIMS = (((0,), (0,)), ((), ()))


def _mm(a, b):
    return jnp.dot(a.astype(bf16), b.astype(bf16), preferred_element_type=f32)


def _mm_nt(a, b):
    return lax.dot_general(a.astype(bf16), b.astype(bf16), NT_DIMS, preferred_element_type=f32)


def _mm_tn(a, b):
    return lax.dot_general(a.astype(bf16), b.astype(bf16), TN_DIMS, preferred_element_type=f32)


def _mm_split(a01, x):
    hi = x.astype(bf16)
    lo = (x - hi.astype(f32)).astype(bf16)
    return jnp.dot(a01, hi, preferred_element_type=f32) + jnp.dot(a01, lo, preferred_element_type=f32)


def _seg_sum(x, p_ref):
    hi = x.astype(bf16)
    lo = (x - hi.astype(f32)).astype(bf16)
    return (jnp.dot(hi, p_ref[...], preferred_element_type=f32)
            + jnp.dot(lo, p_ref[...], preferred_element_type=f32))


def _stack_heads(x):
    lane = lax.broadcasted_iota(jnp.int32, x.shape, 1)
    return jnp.concatenate([jnp.where(lane < HEAD_DIM, x, 0.0), jnp.where(lane >= HEAD_DIM, x, 0.0)], axis=0)


def _chunk_pair(r, cum, lw, alpha, beta, k, v, st):
    c = r.shape[0]
    eg, egi, egp = jnp.exp(cum), jnp.exp(-cum), jnp.exp(cum - lw)
    g_end = eg[c - 1:c, :]
    x_a, x_r = _stack_heads(alpha * egp), _stack_heads(r * eg)
    bt, kt = beta * egi, k * egi
    b_b, b_k = jnp.concatenate([bt, bt], axis=0), jnp.concatenate([kt, kt], axis=0)
    x_bg, x_kg, v_s = _stack_heads(bt * g_end), _stack_heads(kt * g_end), _stack_heads(v)
    row = lax.broadcasted_iota(jnp.int32, (2 * c, 2 * c), 0)
    col = lax.broadcasted_iota(jnp.int32, (2 * c, 2 * c), 1)
    same = (row // c) == (col // c)
    strict = same & (col < row)
    incl = same & (col <= row)
    n = jnp.where(strict, _mm_nt(x_a, b_b), 0.0)
    a_ak = jnp.where(strict, _mm_nt(x_a, b_k), 0.0)
    a_rb = jnp.where(incl, _mm_nt(x_r, b_b), 0.0)
    a_rk = jnp.where(incl, _mm_nt(x_r, b_k), 0.0)
    def level(k):
        return jnp.where(((row >> (k + 1)) == (col >> (k + 1))) & ((row >> k) != (col >> k)), n, 0.0)
    inv = jnp.where(row == col, 1.0, 0.0) + level(0)
    for k in range(1, int(np.log2(c))):
        inv = inv + _mm(_mm(inv, level(k)), inv)
    st_b = st.astype(bf16)
    p = _mm(x_a, st_b) + _mm(a_ak, v_s)
    e = _mm(inv, p)
    y_s = _mm(x_r, st_b) + _mm(a_rb, e) + _mm(a_rk, v_s)
    g_col = jnp.broadcast_to(g_end, (2 * c, 2 * c)).T
    st_new = st * g_col + _mm_tn(x_bg, e) + _mm_tn(x_kg, v_s)
    return y_s[:c] + y_s[c:], st_new


def _rwkv_prompt_kernel(*refs, first_layer):
    zr_ref, zk_ref, zv_ref, zl_ref, sp_ref, mu_ref, vec_ref, lr_ref, p_ref, tri_ref = refs[:10]
    if first_layer:
        o_ref, st_ref, vfo_ref, prev_ref = refs[10:]
    else:
        vf_ref, o_ref, st_ref, prev_ref = refs[10:]
    c = RW_CHUNK
    ci = pl.program_id(1)

    @pl.when(ci == 0)
    def _():
        prev_ref[0:1, :] = sp_ref[...]
        st_ref[...] = jnp.zeros(st_ref.shape, f32)

    def shifted(z_ref, lo, hi):
        z = z_ref[...]
        first = lax.broadcasted_iota(jnp.int32, z.shape, 0) == 0
        prev = jnp.where(first, prev_ref[0:1, lo:hi], pltpu.roll(z, 1, 0))
        xs = z + (prev - z) * mu_ref[:, lo:hi]
        prev_ref[0:1, lo:hi] = z[c - 1:c, :]
        return xs

    xr = shifted(zr_ref, 0, D_RWKV)
    xk = shifted(zk_ref, D_RWKV, 2 * D_RWKV)
    xv = shifted(zv_ref, 2 * D_RWKV, 3 * D_RWKV)
    xl = shifted(zl_ref, 3 * D_RWKV, SHIFT_COLS)
    w0, a0, vres0, k_k = vec_ref[0:1, :], vec_ref[1:2, :], vec_ref[2:3, :], vec_ref[3:4, :]
    k_a, r_k, lnx_w, lnx_b = vec_ref[4:5, :], vec_ref[5:6, :], vec_ref[6:7, :], vec_ref[7:8, :]

    xl_b = xl.astype(bf16)
    u = -(w0 + jnp.dot(jnp.tanh(xl).astype(bf16), lr_ref[0], preferred_element_type=f32))
    softplus = jnp.maximum(u, 0.0) + jnp.log(1.0 + jnp.exp(-jnp.abs(u)))
    lw = -jnp.exp(-softplus - 0.5)
    a = jax.nn.sigmoid(a0 + jnp.dot(xl_b, lr_ref[1], preferred_element_type=f32))
    if first_layer:
        v = xv
        vfo_ref[...] = xv
    else:
        v = xv + (vf_ref[...] - xv) * jax.nn.sigmoid(vres0 + jnp.dot(xl_b, lr_ref[2], preferred_element_type=f32))
    gate = jnp.dot(jax.nn.sigmoid(xl).astype(bf16), lr_ref[3], preferred_element_type=f32)
    kk = xk * k_k
    k2 = xk * (1.0 + (a - 1.0) * k_a)
    cum = _mm_split(tri_ref[...], lw)

    for pair in range(N_PAIRS):
        cols = slice(pair * LANES, (pair + 1) * LANES)
        kk_p = kk[:, cols]
        kk_p = kk_p * lax.rsqrt(jnp.maximum(_seg_sum(kk_p * kk_p, p_ref), 1e-24))
        r_p, k_p, v_p = xr[:, cols], k2[:, cols], v[:, cols]
        y, st_new = _chunk_pair(r_p, cum[:, cols], lw[:, cols], -kk_p, kk_p * a[:, cols], k_p, v_p, st_ref[pair])
        st_ref[pair] = st_new
        mean = _seg_sum(y, p_ref) * (1.0 / HEAD_DIM)
        d = y - mean
        var = _seg_sum(d * d, p_ref) * (1.0 / HEAD_DIM)
        yn = d * lax.rsqrt(var + GN_EPS) * lnx_w[:, cols] + lnx_b[:, cols]
        bonus = _seg_sum(r_p * k_p * r_k[:, cols], p_ref) * v_p
        o_ref[:, cols] = (yn + bonus) * gate[:, cols]


def rwkv_prompt(z, v_first, lp, consts, batch, seq):
    m = batch * seq
    c = RW_CHUNK
    n_chunks = seq // c
    first_layer = v_first is None
    rows = lambda j: (lambda b, i: (b * n_chunks + i, j))
    const2 = lambda b, i: (0, 0)
    tok = pl.BlockSpec((c, D_RWKV), rows(0))
    tok_shape = jax.ShapeDtypeStruct((m, D_RWKV), f32)
    out = pl.pallas_call(
        functools.partial(_rwkv_prompt_kernel, first_layer=first_layer),
        grid=(batch, n_chunks),
        in_specs=[pl.BlockSpec((c, D_RWKV), rows(Z_R // D_RWKV)),
                  pl.BlockSpec((c, D_RWKV), rows(Z_K // D_RWKV)),
                  pl.BlockSpec((c, D_RWKV), rows(Z_V // D_RWKV)),
                  pl.BlockSpec((c, LANES), rows(Z_LR // LANES)),
                  pl.BlockSpec((None, 1, SHIFT_COLS), lambda b, i: (b, 0, 0)),
                  pl.BlockSpec((1, SHIFT_COLS), const2),
                  pl.BlockSpec((8, D_RWKV), const2),
                  pl.BlockSpec((4, LANES, D_RWKV), lambda b, i: (0, 0, 0)),
                  pl.BlockSpec((LANES, LANES), const2),
                  pl.BlockSpec((c, c), const2)] + ([] if first_layer else [tok]),
        out_specs=[tok, pl.BlockSpec((None, N_PAIRS, LANES, LANES), lambda b, i: (b, 0, 0, 0))]
                  + ([tok] if first_layer else []),
        out_shape=[tok_shape, jax.ShapeDtypeStruct((batch, N_PAIRS, LANES, LANES), f32)]
                  + ([tok_shape] if first_layer else []),
        scratch_shapes=[pltpu.VMEM((8, SHIFT_COLS), f32)],
        compiler_params=pltpu.CompilerParams(
            dimension_semantics=("parallel", "arbitrary"), vmem_limit_bytes=VMEM_LIMIT),
        name="rwkv_prompt",
    )(z, z, z, z, consts['shift_zero'], lp['mu_row'], lp['rwkv_vecs'], lp['rwkv_lowrank'],
      consts['seg_ones'], consts['tri'], *([] if first_layer else [v_first]))
    return out[0], out[1], (out[2] if first_layer else v_first)


def state_from_pair_tiles(st):
    b = st.shape[0]
    t = st.reshape(b, N_PAIRS, 2, HEAD_DIM, 2, HEAD_DIM)
    diag = jnp.stack([t[:, :, 0, :, 0, :], t[:, :, 1, :, 1, :]], axis=2)
    return jnp.swapaxes(diag, -1, -2).reshape(b, N_HEADS_RWKV, HEAD_DIM, HEAD_DIM)


def rwkv_params(lp_raw, layer):
    zrow = jnp.zeros((D_RWKV,), f32)
    vres0 = lp_raw['vres0'][layer - 1] if layer > 0 else zrow
    vecs = jnp.stack([lp_raw['w0'][layer], lp_raw['a0'][layer], vres0, lp_raw['k_k'][layer], lp_raw['k_a'][layer],
                      lp_raw['r_k'][layer].reshape(D_RWKV), lp_raw['lnx_w'][layer], lp_raw['lnx_b'][layer]])
    def rows(w, lo):
        return jnp.zeros((LANES, D_RWKV), f32).at[lo:lo + w.shape[0]].set(w)
    vres2 = lp_raw['vres2'][layer - 1] if layer > 0 else jnp.zeros((RANK_A, D_RWKV), f32)
    lowrank = jnp.stack([rows(lp_raw['w2'][layer], 0), rows(lp_raw['a2'][layer], RANK_W),
                         rows(vres2, RANK_W), rows(lp_raw['g2'][layer], RANK_W + RANK_A)]).astype(bf16)
    return vecs, lowrank


CHUNKS_PER_PAGE = PAGE_SIZE // STRIDE_CMP
PAGES_PER_STEP = 8
BLOCKS_PER_STEP = PAGES_PER_STEP * PAGE_SIZE // BLOCK_SLC
N_BLK_PAD = 256


def _compress_pool_kernel(x_ref, pe_ref, w_ref, y_ref):
    x = x_ref[...]
    y_ref[:, :KV_COLS] = jnp.dot((x + pe_ref[0]).astype(bf16), w_ref[0], preferred_element_type=f32)
    y_ref[:, KV_COLS:] = jnp.dot((x + pe_ref[1]).astype(bf16), w_ref[1], preferred_element_type=f32)


def compress_pool(cache_cmp, wb, peb):
    depth, n_pool = cache_cmp.shape[:2]
    rows = n_pool * CHUNKS_PER_PAGE
    x = cache_cmp.reshape(depth, rows, CHUNK_COLS)
    tm = 512
    assert rows % tm == 0
    return pl.pallas_call(
        _compress_pool_kernel,
        grid=(depth, rows // tm),
        in_specs=[pl.BlockSpec((None, tm, CHUNK_COLS), lambda l, i: (l, i, 0)),
                  pl.BlockSpec((None, CMP_R, 1, CHUNK_COLS), lambda l, i: (l, 0, 0, 0)),
                  pl.BlockSpec((None, CMP_R, CHUNK_COLS, KV_COLS), lambda l, i: (l, 0, 0, 0))],
        out_specs=pl.BlockSpec((None, tm, 2 * KV_COLS), lambda l, i: (l, i, 0)),
        out_shape=jax.ShapeDtypeStruct((depth, rows, 2 * KV_COLS), f32),
        compiler_params=pltpu.CompilerParams(
            dimension_semantics=("parallel", "parallel"), vmem_limit_bytes=VMEM_LIMIT),
        name="compress_pool",
    )(x, peb, wb)


def _softmax_2d(s, mask):
    s = jnp.where(mask, s, NEG)
    m = jnp.max(s, axis=-1, keepdims=True)
    p = jnp.where(mask, jnp.exp(s - m), 0.0)
    l = jnp.sum(p, axis=-1, keepdims=True)
    return p * jnp.where(l > 0.0, 1.0 / l, 0.0)


def _pad_rows(x, rows):
    return jnp.concatenate([x, jnp.zeros((rows - x.shape[0], x.shape[1]), x.dtype)], axis=0)


def head_dim_major(tile_ref, which):
    return jnp.concatenate([tile_ref[which, g] for g in range(N_KV_HEADS)], axis=0).astype(bf16)


def cache_tiles(cache):
    nd = cache.ndim
    return jnp.transpose(cache, tuple(range(nd - 4)) + (nd - 3, nd - 2, nd - 1, nd - 4))


def _sample_attn_kernel(pt_ref, q_ref, gate_ref, y_ref, *rest, past, n_new):
    pages = rest[:PAGES_PER_STEP]
    (ksn_ref, kwn_ref, cw_ref, smap_ref, o_ref,
     oc_ref, ow_ref, sel_ref, imp_ref, m_ref, l_ref, acc_ref) = rest[PAGES_PER_STEP:]
    del pt_ref
    c = pl.program_id(1)
    n_steps = pl.num_programs(1)
    rows = N_HEADS_ATTN * n_new
    q = q_ref[...]
    qpos = past + lax.broadcasted_iota(jnp.int32, (rows, 1), 0) % n_new
    n_blk = -(-(past + n_new) // BLOCK_SLC)

    @pl.when(c == 0)
    def _():
        y = y_ref[...]
        n_chunks = y.shape[0]
        comp = y[:, :KV_COLS] + pltpu.roll(y[:, KV_COLS:], n_chunks - 1, 0)
        s = lax.dot_general(q, comp[:, :D_KV].astype(bf16), NT_DIMS, preferred_element_type=f32)
        c_end = lax.broadcasted_iota(jnp.int32, (1, n_chunks), 1) * STRIDE_CMP + (BLOCK_CMP - 1)
        valid = (c_end <= qpos) & (lax.broadcasted_iota(jnp.int32, (1, n_chunks), 1) < n_chunks - 1)
        p_c = _softmax_2d(s, valid)
        oc_ref[...] = jnp.dot(p_c.astype(bf16), comp[:, D_KV:].astype(bf16), preferred_element_type=f32)

        hq = HEADS_PER_KV * n_new
        p_sum = [sum(p_c[g * hq + hh * n_new:g * hq + (hh + 1) * n_new] for hh in range(HEADS_PER_KV))
                 for g in range(N_KV_HEADS)]
        p_sum = _pad_rows(jnp.concatenate(p_sum, axis=0), LANES).astype(bf16)
        imp = lax.dot_general(smap_ref[...], p_sum, NT_DIMS, preferred_element_type=f32)
        jj = lax.broadcasted_iota(jnp.int32, (N_BLK_PAD, LANES), 0)
        qp = past + lax.broadcasted_iota(jnp.int32, (N_BLK_PAD, LANES), 1) % n_new
        cur = qp // BLOCK_SLC
        forced = (jj == 0) | ((jj <= cur) & (jj > cur - N_LOCAL))
        imp = jnp.where(jj * BLOCK_SLC <= qp, jnp.where(forced, FORCED_SCORE, imp), NEG)
        imp_ref[...] = imp

        def rank_step(j, rank):
            row = imp_ref[pl.ds(j, 1), :]
            return rank + jnp.where((row > imp) | ((row == imp) & (jj > j)), 1.0, 0.0)

        rank = lax.fori_loop(0, n_blk, rank_step, jnp.zeros((N_BLK_PAD, LANES), f32))
        sel = jnp.where(rank < float(min(N_SELECT, n_blk)), 1.0, 0.0).T
        sel_ref[...] = jnp.concatenate(
            [sel[g * n_new:(g + 1) * n_new] for g in range(N_KV_HEADS) for _ in range(HEADS_PER_KV)], axis=0)

        w_buf = cw_ref.shape[-1]
        kt_w, vt_w = head_dim_major(cw_ref, 0), head_dim_major(cw_ref, 1)
        kwn = _pad_rows(kwn_ref[...], LANES)
        s1 = jnp.dot(q, kt_w, preferred_element_type=f32)
        s2 = lax.dot_general(q, kwn[:, :D_KV].astype(bf16), NT_DIMS, preferred_element_type=f32)
        wpos = jnp.concatenate([past - w_buf + lax.broadcasted_iota(jnp.int32, (1, w_buf), 1),
                                past + lax.broadcasted_iota(jnp.int32, (1, LANES), 1)], axis=1)
        dist = qpos - wpos
        p_w = _softmax_2d(jnp.concatenate([s1, s2], axis=1), (dist >= 0) & (dist < WINDOW)).astype(bf16)
        ow_ref[...] = (lax.dot_general(p_w[:, :w_buf], vt_w, NT_DIMS, preferred_element_type=f32)
                       + jnp.dot(p_w[:, w_buf:], kwn[:, D_KV:].astype(bf16), preferred_element_type=f32))

        m_ref[...] = jnp.full(m_ref.shape, NEG, f32)
        l_ref[...] = jnp.zeros(l_ref.shape, f32)
        acc_ref[...] = jnp.zeros(acc_ref.shape, f32)

    def online_update(s, msk, pv):
        s = jnp.where(msk, s, NEG)
        m_old = m_ref[...]
        m_new = jnp.maximum(m_old, jnp.max(s, axis=-1, keepdims=True))
        alpha = jnp.exp(m_old - m_new)
        p = jnp.where(msk, jnp.exp(s - m_new), 0.0)
        l_ref[...] = alpha * l_ref[...] + jnp.sum(p, axis=-1, keepdims=True)
        acc_ref[...] = alpha * acc_ref[...] + pv(p.astype(bf16))
        m_ref[...] = m_new

    kt = jnp.concatenate([head_dim_major(pg, 0) for pg in pages], axis=1)
    vt = jnp.concatenate([head_dim_major(pg, 1) for pg in pages], axis=1)
    n_keys = kt.shape[1]
    blk = lax.broadcasted_iota(jnp.int32, (N_BLK_PAD, n_keys), 0)
    key_blk = c * BLOCKS_PER_STEP + lax.broadcasted_iota(jnp.int32, (N_BLK_PAD, n_keys), 1) // BLOCK_SLC
    expand = jnp.where(blk == key_blk, 1.0, 0.0).astype(bf16)
    msk = jnp.dot(sel_ref[...].astype(bf16), expand, preferred_element_type=f32) > 0.5
    online_update(jnp.dot(q, kt, preferred_element_type=f32), msk,
                  lambda p: lax.dot_general(p, vt, NT_DIMS, preferred_element_type=f32))

    @pl.when(c == n_steps - 1)
    def _():
        ksn = _pad_rows(ksn_ref[...], LANES)
        kpos = past + lax.broadcasted_iota(jnp.int32, (1, LANES), 1)
        last_blk = past // BLOCK_SLC
        msk_new = (sel_ref[:, last_blk:last_blk + 1] > 0.5) & (kpos <= qpos)
        online_update(lax.dot_general(q, ksn[:, :D_KV].astype(bf16), NT_DIMS, preferred_element_type=f32), msk_new,
                      lambda p: jnp.dot(p, ksn[:, D_KV:].astype(bf16), preferred_element_type=f32))
        o_s = acc_ref[...] / l_ref[...]
        o_c, o_w = oc_ref[...], ow_ref[...]
        lane = lax.broadcasted_iota(jnp.int32, (n_new, LANES), 1)
        heads = []
        for h in range(N_HEADS_ATTN):
            r = slice(h * n_new, (h + 1) * n_new)
            gc = gate_ref[:, N_BRANCH * h:N_BRANCH * h + 1]
            gs = gate_ref[:, N_BRANCH * h + 1:N_BRANCH * h + 2]
            gw = gate_ref[:, N_BRANCH * h + 2:N_BRANCH * h + 3]
            o_h = gc * o_c[r] + gs * o_s[r] + gw * o_w[r]
            heads.append(o_h if (h // HEADS_PER_KV) == (h % 2) else pltpu.roll(o_h, HEAD_DIM, 1))
        for pair in range(N_HEADS_ATTN // 2):
            o_ref[:, pair * LANES:(pair + 1) * LANES] = jnp.where(lane < HEAD_DIM, heads[2 * pair], heads[2 * pair + 1])


def sample_tables(past, n_new):
    n_chunks = past // STRIDE_CMP
    start = np.arange(n_chunks) * STRIDE_CMP
    first, last = start // BLOCK_SLC, (start + BLOCK_CMP - 1) // BLOCK_SLC
    j = np.arange(N_BLK_PAD)
    smap = (first[None, :] <= j[:, None]) & (last[None, :] >= j[:, None]) & (np.arange(n_chunks)[None, :] < n_chunks - 1)
    return jnp.asarray(smap, bf16)


def sample_attn(q_rows, gates, y_req, cache_slc, layer, page_table, kv_s_new, kv_w_new, cache_win, smap):
    bd, n_pages = page_table.shape
    n_new = kv_s_new.shape[1]
    past = n_pages * PAGE_SIZE
    rows = N_HEADS_ATTN * n_new
    w_buf = cache_win.shape[-1]
    tile = (None, None, 2, N_KV_HEADS, HEAD_DIM)
    assert n_pages % PAGES_PER_STEP == 0 and -(-(past + n_new) // BLOCK_SLC) <= N_BLK_PAD and n_new <= LANES
    per_b = lambda b, c, pt: (b, 0, 0)
    page = lambda j: (lambda b, c, pt: (layer, pt[b, c * PAGES_PER_STEP + j], 0, 0, 0, 0))
    grid_spec = pltpu.PrefetchScalarGridSpec(
        num_scalar_prefetch=1,
        grid=(bd, n_pages // PAGES_PER_STEP),
        in_specs=[pl.BlockSpec((None, rows, LANES), per_b),
                  pl.BlockSpec((None, n_new, LANES), per_b),
                  pl.BlockSpec((None, past // STRIDE_CMP, 2 * KV_COLS), per_b)]
                 + [pl.BlockSpec(tile + (PAGE_SIZE,), page(j)) for j in range(PAGES_PER_STEP)]
                 + [pl.BlockSpec((None, n_new, KV_COLS), per_b),
                    pl.BlockSpec((None, n_new, KV_COLS), per_b),
                    pl.BlockSpec(tile + (w_buf,), lambda b, c, pt: (layer, b, 0, 0, 0, 0)),
                    pl.BlockSpec(smap.shape, lambda b, c, pt: (0, 0))],
        out_specs=pl.BlockSpec((None, n_new, D_ATTN), per_b),
        scratch_shapes=[pltpu.VMEM((rows, LANES), f32), pltpu.VMEM((rows, LANES), f32),
                        pltpu.VMEM((rows, N_BLK_PAD), f32), pltpu.VMEM((N_BLK_PAD, LANES), f32),
                        pltpu.VMEM((rows, 1), f32), pltpu.VMEM((rows, 1), f32), pltpu.VMEM((rows, LANES), f32)])
    return pl.pallas_call(
        functools.partial(_sample_attn_kernel, past=past, n_new=n_new),
        grid_spec=grid_spec,
        out_shape=jax.ShapeDtypeStruct((bd, n_new, D_ATTN), f32),
        compiler_params=pltpu.CompilerParams(
            dimension_semantics=("parallel", "arbitrary"), vmem_limit_bytes=VMEM_LIMIT),
        name="sample_attn",
    )(page_table, q_rows, gates, y_req, *([cache_slc] * PAGES_PER_STEP), kv_s_new, kv_w_new, cache_win, smap)


def _split(x, sizes):
    offs = [int(o) for o in np.cumsum(sizes)[:-1]]
    return jnp.split(x, offs, axis=-1)


def rwkv7_time_mix(zs, shift_prev, wkv0, v_first, lp):
    B, T = zs.shape[:2]
    zf = zs.astype(f32)
    prev = jnp.concatenate([shift_prev.astype(f32)[:, None], zf[:, :-1]], axis=1)
    xs = zf + (prev - zf) * lp['mu']
    r, k, v, xw, xa, xg = _split(xs, [D_RWKV] * 3 + [RANK_W, RANK_A, RANK_G])
    w_raw = -jax.nn.softplus(-(lp['w0'] + jnp.tanh(xw) @ lp['w2'])) - 0.5
    decay = jnp.exp(-jnp.exp(w_raw))
    a = jax.nn.sigmoid(lp['a0'] + xa @ lp['a2'])
    if lp['vres'] is None:
        v_first = v
    else:
        vres0, vres2 = lp['vres']
        v = v + (v_first - v) * jax.nn.sigmoid(vres0 + xa @ vres2)
    g = jax.nn.sigmoid(xg) @ lp['g2']
    heads = lambda t: t.reshape(B, T, N_HEADS_RWKV, HEAD_DIM)
    kk = heads(k * lp['k_k'])
    kk = kk * lax.rsqrt(jnp.maximum(jnp.sum(kk * kk, axis=-1, keepdims=True), 1e-24))
    k = k * (1.0 + (a - 1.0) * lp['k_a'])
    r_h, k_h, v_h, a_h, d_h = heads(r), heads(k), heads(v), heads(a), heads(decay)

    def step(S, inp):
        r_t, d_t, kk_t, a_t, k_t, v_t = inp
        sa = jnp.einsum('bhvk,bhk->bhv', S, -kk_t)
        S = S * d_t[:, :, None, :] + sa[..., None] * (kk_t * a_t)[:, :, None, :] + v_t[..., None] * k_t[:, :, None, :]
        return S, jnp.einsum('bhvk,bhk->bhv', S, r_t)

    tm = lambda t: jnp.moveaxis(t, 1, 0)
    S_T, ys = lax.scan(step, wkv0.astype(f32), (tm(r_h), tm(d_h), tm(kk), tm(a_h), tm(k_h), tm(v_h)))
    y = jnp.moveaxis(ys, 0, 1)
    mean = jnp.mean(y, axis=-1, keepdims=True)
    var = jnp.mean(jnp.square(y - mean), axis=-1, keepdims=True)
    y = (y - mean) * lax.rsqrt(var + GN_EPS) * lp['lnx_w'].reshape(N_HEADS_RWKV, HEAD_DIM) + lp['lnx_b'].reshape(N_HEADS_RWKV, HEAD_DIM)
    y = y + jnp.sum(r_h * k_h * lp['r_k'], axis=-1, keepdims=True) * v_h
    y = y.reshape(B, T, D_RWKV) * g
    return y, S_T, zs[:, -1], v_first


def sample_trunk_layer(x, layer, shift_prev, wkv0, v_first, lp, consts, caches):
    B, Tn = x.shape[:2]
    rows = B * Tn
    page_table = caches['page_table']
    x2 = x.reshape(rows, D_MODEL)
    z = norm_proj(x2, lp['norm1_g'], lp['w_in'])
    q_ext, kv_c, kv_s, kv_w, _, _, gates = attn_prep(
        z, consts['rope_s'], consts['seg_ones'], lp['q_g_lanes'], lp['k_g_lanes'], lp['gate_b_lanes'], 1, rows)
    q_rows = q_ext.reshape(N_HEADS_ATTN, B, Tn, LANES).transpose(1, 0, 2, 3).reshape(B, N_HEADS_ATTN * Tn, LANES)
    n_pool = caches['slc'].shape[1]
    y_req = caches['y_pool'][layer].reshape(n_pool, CHUNKS_PER_PAGE, 2 * KV_COLS)[page_table]
    y_req = y_req.reshape(B, page_table.shape[1] * CHUNKS_PER_PAGE, 2 * KV_COLS)
    new_rows = lambda kv: kv.reshape(B, Tn, KV_COLS)
    o_attn = sample_attn(q_rows, gates.reshape(B, Tn, LANES), y_req, caches['slc'], layer, page_table,
                         new_rows(kv_s), new_rows(kv_w), caches['win'], consts['smap_s'])
    tok = lambda kv: kv.reshape(B, Tn, 2, N_KV_HEADS, HEAD_DIM)
    win = jnp.concatenate([caches['win_raw'][layer][:, Tn:], tok(kv_w)], axis=1)
    attn_state = (tok(kv_c), tok(kv_s), win)
    z_rwkv = split_z(z)[2].reshape(B, Tn, SHIFT_COLS)
    o_rwkv, wkv_t, shift_t, v_first = rwkv7_time_mix(z_rwkv, shift_prev, wkv0, v_first, lp)
    h = out_proj(x2, o_attn.reshape(rows, D_ATTN), o_rwkv.reshape(rows, D_RWKV), lp['w_out'])
    y = ffn(h, lp['norm2_g'], lp['w_up'], lp['w_down']).reshape(B, Tn, D_MODEL)
    return y, attn_state, wkv_t, shift_t, v_first


def prompt_trunk_layer(x, v_first, lp, consts):
    B, T = x.shape[:2]
    x2 = x.reshape(B * T, D_MODEL)
    z = norm_proj(x2, lp['norm1_g'], lp['w_in'])
    q_ext, kv_c, kv_s, kv_w, kb, vvb, gates = attn_prep(
        z, consts['rope'], consts['seg_ones'], lp['q_g_lanes'], lp['k_g_lanes'], lp['gate_b_lanes'], B, T)
    kc, vvc = compress(kv_c, lp['cmp_wb'], lp['cmp_peb'], B)
    o_attn = nsa_prompt(q_ext, gates, kc, vvc, kb, vvb, consts['nsa'], B, T)
    pages = lambda kv: kv.reshape(B, T // PAGE_SIZE, PAGE_SIZE, 2, N_KV_HEADS, HEAD_DIM)
    attn_state = (pages(kv_c), pages(kv_s),
                  kv_w.reshape(B, T, 2, N_KV_HEADS, HEAD_DIM)[:, -min(WINDOW, T):])
    o_rwkv, st, v_first = rwkv_prompt(z, v_first, lp, consts, B, T)
    z_last = z.reshape(B, T, IN_COLS_PAD)[:, -1]
    shift_t = jnp.concatenate([z_last[:, Z_R:Z_KV], z_last[:, Z_LR:]], axis=-1)
    h = out_proj(x2, o_attn.reshape(B * T, D_ATTN), o_rwkv, lp['w_out'])
    y = ffn(h, lp['norm2_g'], lp['w_up'], lp['w_down']).reshape(B, T, D_MODEL)
    return y, attn_state, state_from_pair_tiles(st), shift_t, v_first


def kernel(x_prompt, x_sample, cache_cmp_kv, cache_slc_kv, cache_win_kv, state_wkv, state_shift, page_table,
           norm1_g, w_in, q_norm_g, k_norm_g, gate_b, cmp_pe, cmp_w, shift_mu, w0, w2, a0, a2, vres0, vres2,
           g2, k_k, k_a, r_k, lnx_w, lnx_b, w_out, norm2_g, w_up, w_down):
    B, T = x_prompt.shape[:2]
    Tn = x_sample.shape[1]
    past = page_table.shape[1] * PAGE_SIZE
    pos_p = jnp.arange(T, dtype=jnp.int32)
    pos_s = past + jnp.arange(Tn, dtype=jnp.int32)
    w_in_b = permute_w_in(w_in).astype(bf16)
    w_out_b, w_up_b, w_down_b = w_out.astype(bf16), w_up.astype(bf16), w_down.astype(bf16)
    seg = np.arange(LANES) // HEAD_DIM
    consts = {'rope': rope_lane_tables(pos_p), 'nsa': nsa_tables(T),
              'seg_ones': jnp.asarray(seg[:, None] == seg[None, :], bf16),
              'tri': jnp.asarray(np.tril(np.ones((RW_CHUNK, RW_CHUNK))), bf16),
              'shift_zero': jnp.zeros((B, 1, SHIFT_COLS), f32)}
    rwkv_raw = {'w0': w0, 'w2': w2, 'a0': a0, 'a2': a2, 'vres0': vres0, 'vres2': vres2, 'g2': g2, 'k_k': k_k,
                'k_a': k_a, 'r_k': r_k, 'lnx_w': lnx_w, 'lnx_b': lnx_b}
    Bd = x_sample.shape[0]
    n_pool, w_buf = cache_slc_kv.shape[1], cache_win_kv.shape[2]
    consts['rope_s'] = tuple(jnp.tile(t, (Bd, 1)) for t in rope_lane_tables(pos_s))
    consts['smap_s'] = sample_tables(past, Tn)
    cmp_wb, cmp_peb = jax.vmap(compress_weights)(cmp_pe, cmp_w)
    caches = {'page_table': page_table,
              'slc': cache_tiles(cache_slc_kv), 'win': cache_tiles(cache_win_kv), 'win_raw': cache_win_kv,
              'y_pool': compress_pool(cache_cmp_kv, cmp_wb, cmp_peb)}
    xp, xs = x_prompt, x_sample
    vf_p, vf_s = None, None
    p_cmp, p_slc, p_win, p_wkv, p_shift = [], [], [], [], []
    s_cmp, s_slc, s_win, s_wkv, s_shift = [], [], [], [], []
    for l in range(DEPTH):
        lp = {'norm1_g': norm1_g[l], 'w_in': w_in_b[l], 'q_norm_g': q_norm_g[l], 'k_norm_g': k_norm_g[l],
              'gate_b': gate_b[l], 'cmp_pe': cmp_pe[l], 'cmp_w': cmp_w[l], 'mu': shift_mu[l], 'w0': w0[l],
              'w2': w2[l], 'a0': a0[l], 'a2': a2[l],
              'vres': None if l == 0 else (vres0[l - 1], vres2[l - 1]),
              'g2': g2[l], 'k_k': k_k[l], 'k_a': k_a[l], 'r_k': r_k[l], 'lnx_w': lnx_w[l], 'lnx_b': lnx_b[l],
              'w_out': w_out_b[l], 'norm2_g': norm2_g[l], 'w_up': w_up_b[l], 'w_down': w_down_b[l]}
        lp['cmp_wb'], lp['cmp_peb'] = cmp_wb[l], cmp_peb[l]
        lp['q_g_lanes'] = jnp.tile(q_norm_g[l], 2).reshape(1, LANES)
        lp['k_g_lanes'] = jnp.tile(k_norm_g[l], (1, 2))
        lp['gate_b_lanes'] = jnp.pad(gate_b[l], (0, LANES - N_BRANCH * N_HEADS_ATTN)).reshape(1, LANES)
        lp['mu_row'] = shift_mu[l].reshape(1, SHIFT_COLS)
        lp['rwkv_vecs'], lp['rwkv_lowrank'] = rwkv_params(rwkv_raw, l)
        xp, (c_kv, sl_kv, w_kv), wkv_t, sh_t, vf_p = prompt_trunk_layer(xp, vf_p, lp, consts)
        p_cmp.append(c_kv); p_slc.append(sl_kv); p_win.append(w_kv); p_wkv.append(wkv_t); p_shift.append(sh_t)
        xs, (c_kv, sl_kv, w_kv), wkv_t, sh_t, vf_s = sample_trunk_layer(
            xs, l, state_shift[l], state_wkv[l], vf_s, lp, consts, caches)
        s_cmp.append(c_kv); s_slc.append(sl_kv); s_win.append(w_kv); s_wkv.append(wkv_t); s_shift.append(sh_t)
    return (xp, xs, jnp.stack(p_cmp), jnp.stack(p_slc), jnp.stack(p_win), jnp.stack(p_wkv), jnp.stack(p_shift),
            jnp.stack(s_cmp), jnp.stack(s_slc), jnp.stack(s_win), jnp.stack(s_wkv), jnp.stack(s_shift))
```

```python
import functools
import jax, jax.numpy as jnp
from jax import lax
import numpy as np
from jax.experimental import pallas as pl
from jax.experimental.pallas import tpu as pltpu

D_MODEL = 1024
DEPTH = 4
PAGE_SIZE = 128
HEAD_DIM = 64
N_HEADS_ATTN = 8
N_KV_HEADS = 2
HEADS_PER_KV = N_HEADS_ATTN // N_KV_HEADS
D_ATTN = N_HEADS_ATTN * HEAD_DIM
D_KV = N_KV_HEADS * HEAD_DIM
N_BRANCH = 3
N_HEADS_RWKV = 8
D_RWKV = N_HEADS_RWKV * HEAD_DIM
ROT_DIM = HEAD_DIM // 4
ROPE_THETA = 500000.0
BLOCK_CMP = 32
STRIDE_CMP = 16
BLOCK_SLC = 64
N_SELECT = 16
N_LOCAL = 2
WINDOW = 512
QBLOCK = 128
RANK_W = 32
RANK_A = 32
RANK_G = 64
D_FF = 4 * D_MODEL
SHIFT_COLS = 3 * D_RWKV + RANK_W + RANK_A + RANK_G
IN_COLS = D_ATTN + 6 * D_KV + N_BRANCH * N_HEADS_ATTN + SHIFT_COLS
IN_COLS_PAD = 3072
NORM_EPS = 1e-6
GN_EPS = 64e-5
NEG = -1e30
FORCED_SCORE = 1e9

VMEM_LIMIT = 48 * 1024 * 1024
bf16 = jnp.bfloat16
f32 = jnp.float32


def _row_tile(m):
    return min(m, 1024)


def _norm_proj_kernel(x_ref, g_ref, w_ref, o_ref):
    x = x_ref[...]
    n = x * lax.rsqrt(jnp.mean(x * x, axis=-1, keepdims=True) + NORM_EPS) * g_ref[...]
    o_ref[...] = jnp.dot(n.astype(bf16), w_ref[...], preferred_element_type=f32)


def norm_proj(x, g, w):
    m, n = x.shape[0], w.shape[1]
    tm, tn = _row_tile(m), 512
    return pl.pallas_call(
        _norm_proj_kernel,
        grid=(m // tm, n // tn),
        in_specs=[pl.BlockSpec((tm, D_MODEL), lambda i, j: (i, 0)),
                  pl.BlockSpec((1, D_MODEL), lambda i, j: (0, 0)),
                  pl.BlockSpec((D_MODEL, tn), lambda i, j: (0, j))],
        out_specs=pl.BlockSpec((tm, tn), lambda i, j: (i, j)),
        out_shape=jax.ShapeDtypeStruct((m, n), f32),
        compiler_params=pltpu.CompilerParams(
            dimension_semantics=("parallel", "arbitrary"), vmem_limit_bytes=VMEM_LIMIT),
        name="norm_proj",
    )(x, g.reshape(1, D_MODEL), w)


def _out_proj_kernel(x_ref, a_ref, r_ref, w_ref, o_ref):
    o_ref[...] = (x_ref[...]
                  + jnp.dot(a_ref[...].astype(bf16), w_ref[:D_ATTN, :], preferred_element_type=f32)
                  + jnp.dot(r_ref[...].astype(bf16), w_ref[D_ATTN:, :], preferred_element_type=f32))


def out_proj(x, o_attn, o_rwkv, w):
    m = x.shape[0]
    tm = _row_tile(m)
    return pl.pallas_call(
        _out_proj_kernel,
        grid=(m // tm,),
        in_specs=[pl.BlockSpec((tm, D_MODEL), lambda i: (i, 0)),
                  pl.BlockSpec((tm, D_ATTN), lambda i: (i, 0)),
                  pl.BlockSpec((tm, D_RWKV), lambda i: (i, 0)),
                  pl.BlockSpec((D_MODEL, D_MODEL), lambda i: (0, 0))],
        out_specs=pl.BlockSpec((tm, D_MODEL), lambda i: (i, 0)),
        out_shape=jax.ShapeDtypeStruct((m, D_MODEL), f32),
        compiler_params=pltpu.CompilerParams(
            dimension_semantics=("parallel",), vmem_limit_bytes=VMEM_LIMIT),
        name="out_proj",
    )(x, o_attn, o_rwkv, w)


def _ffn_kernel(h_ref, g_ref, wu_ref, wd_ref, o_ref, n_ref):
    j = pl.program_id(1)

    @pl.when(j == 0)
    def _():
        h = h_ref[...]
        n = h * lax.rsqrt(jnp.mean(h * h, axis=-1, keepdims=True) + NORM_EPS) * g_ref[...]
        n_ref[...] = n.astype(bf16)
        o_ref[...] = h

    u = jnp.dot(n_ref[...], wu_ref[...], preferred_element_type=f32)
    u = jnp.square(jnp.maximum(u, 0.0))
    o_ref[...] += jnp.dot(u.astype(bf16), wd_ref[...], preferred_element_type=f32)


def ffn(h, g, w_up, w_down):
    m = h.shape[0]
    tm, tf = _row_tile(m), 512
    return pl.pallas_call(
        _ffn_kernel,
        grid=(m // tm, D_FF // tf),
        in_specs=[pl.BlockSpec((tm, D_MODEL), lambda i, j: (i, 0)),
                  pl.BlockSpec((1, D_MODEL), lambda i, j: (0, 0)),
                  pl.BlockSpec((D_MODEL, tf), lambda i, j: (0, j)),
                  pl.BlockSpec((tf, D_MODEL), lambda i, j: (j, 0))],
        out_specs=pl.BlockSpec((tm, D_MODEL), lambda i, j: (i, 0)),
        out_shape=jax.ShapeDtypeStruct((m, D_MODEL), f32),
        scratch_shapes=[pltpu.VMEM((tm, D_MODEL), bf16)],
        compiler_params=pltpu.CompilerParams(
            dimension_semantics=("parallel", "arbitrary"), vmem_limit_bytes=VMEM_LIMIT),
        name="ffn",
    )(h, g.reshape(1, D_MODEL), w_up, w_down)


LANES = 128
KV_COLS = 2 * D_KV
N_GATES = N_BRANCH * N_HEADS_ATTN
N_LOWRANK = RANK_W + RANK_A + RANK_G
Z_Q, Z_R, Z_K, Z_V = 0, D_ATTN, D_ATTN + D_RWKV, D_ATTN + 2 * D_RWKV
Z_KV = D_ATTN + 3 * D_RWKV
Z_GATE = Z_KV + N_BRANCH * KV_COLS
Z_LR = Z_GATE + LANES
assert N_LOWRANK == LANES and Z_LR + N_LOWRANK == IN_COLS_PAD and N_GATES <= LANES


def permute_w_in(w_in):
    o_kv, o_gate, o_rwkv = D_ATTN, D_ATTN + 6 * D_KV, D_ATTN + 6 * D_KV + N_GATES
    pad = jnp.zeros(w_in.shape[:-1] + (LANES - N_GATES,), w_in.dtype)
    return jnp.concatenate([w_in[..., :o_kv], w_in[..., o_rwkv:o_rwkv + 3 * D_RWKV], w_in[..., o_kv:o_gate],
                            w_in[..., o_gate:o_rwkv], pad, w_in[..., o_rwkv + 3 * D_RWKV:]], axis=-1)


def split_z(z):
    z_attn = jnp.concatenate([z[..., :D_ATTN], z[..., Z_KV:Z_GATE]], axis=-1)
    z_rwkv = jnp.concatenate([z[..., Z_R:Z_KV], z[..., Z_LR:]], axis=-1)
    return z_attn, z[..., Z_GATE:Z_GATE + N_GATES], z_rwkv


def _seg_sumsq(x, p_ref):
    x2 = x * x
    hi = x2.astype(bf16)
    lo = (x2 - hi.astype(f32)).astype(bf16)
    return (jnp.dot(hi, p_ref[...], preferred_element_type=f32)
            + jnp.dot(lo, p_ref[...], preferred_element_type=f32))


def _norm_rope(x, g, p_ref, c, s1, s2):
    y = x * lax.rsqrt(_seg_sumsq(x, p_ref) * (1.0 / HEAD_DIM) + NORM_EPS) * g
    return y * c + pltpu.roll(y, LANES - ROT_DIM // 2, 1) * s1 + pltpu.roll(y, ROT_DIM // 2, 1) * s2


def _attn_prep_kernel(zq_ref, zc_ref, zs_ref, zw_ref, zg_ref, c_ref, s1_ref, s2_ref, p_ref, qg_ref, kg_ref, gb_ref,
                      q_ref, kvc_ref, kvs_ref, kvw_ref, kb_ref, vvb_ref, gate_ref):
    c, s1, s2 = c_ref[...], s1_ref[...], s2_ref[...]
    lane = lax.broadcasted_iota(jnp.int32, (zq_ref.shape[0], LANES), 1)
    for pair in range(N_HEADS_ATTN // 2):
        y = _norm_rope(zq_ref[:, pair * LANES:(pair + 1) * LANES], qg_ref[...], p_ref, c, s1, s2)
        y = y * (HEAD_DIM ** -0.5)
        y_sw = pltpu.roll(y, HEAD_DIM, 1)
        for half in range(2):
            h = 2 * pair + half
            grp = h // HEADS_PER_KV
            src = y if half == grp else y_sw
            keep = (lane >= HEAD_DIM) if grp == 1 else (lane < HEAD_DIM)
            q_ref[0, h] = jnp.where(keep, src, 0.0).astype(bf16)
    ins = (zc_ref, zs_ref, zw_ref)
    outs = (kvc_ref, kvs_ref, kvw_ref)
    for br in range(N_BRANCH):
        k = _norm_rope(ins[br][:, :D_KV], kg_ref[br:br + 1, :], p_ref, c, s1, s2)
        v = ins[br][:, D_KV:]
        outs[br][:, :D_KV] = k
        outs[br][:, D_KV:] = v
        if br > 0:
            kb_ref[br - 1] = k.astype(bf16)
            vvb_ref[br - 1, :, :D_KV] = v.astype(bf16)
            vvb_ref[br - 1, :, D_KV:] = pltpu.roll(v, HEAD_DIM, 1).astype(bf16)
    gate_ref[...] = jax.nn.sigmoid(zg_ref[...] + gb_ref[...])


def rope_lane_tables(pos):
    half = ROT_DIM // 2
    inv = ROPE_THETA ** (-(jnp.arange(0, ROT_DIM, 2, dtype=f32) / ROT_DIM))
    ang = pos.astype(f32)[:, None] * inv[None, :]
    cos, sin = jnp.cos(ang), jnp.sin(ang)
    t = pos.shape[0]
    one, zero = jnp.ones((t, HEAD_DIM - ROT_DIM), f32), jnp.zeros((t, HEAD_DIM - ROT_DIM), f32)
    zh = jnp.zeros((t, half), f32)
    c = jnp.concatenate([cos, cos, one], axis=1)
    s1 = jnp.concatenate([-sin, zh, zero], axis=1)
    s2 = jnp.concatenate([zh, sin, zero], axis=1)
    return tuple(jnp.tile(a, (1, 2)) for a in (c, s1, s2))


def attn_prep(z, rope_tabs, seg_ones, q_g, k_g, gate_b, batch, seq):
    m = batch * seq
    tm = min(512, seq)
    tiles = seq // tm
    c, s1, s2 = rope_tabs
    row = lambda i: (i, 0)
    tab = lambda i: (i % tiles, 0)
    const = lambda i: (0, 0)
    kv_shape = jax.ShapeDtypeStruct((m, KV_COLS), f32)
    return pl.pallas_call(
        _attn_prep_kernel,
        grid=(m // tm,),
        in_specs=[pl.BlockSpec((tm, D_ATTN), row),
                  pl.BlockSpec((tm, KV_COLS), lambda i: (i, Z_KV // KV_COLS)),
                  pl.BlockSpec((tm, KV_COLS), lambda i: (i, Z_KV // KV_COLS + 1)),
                  pl.BlockSpec((tm, KV_COLS), lambda i: (i, Z_KV // KV_COLS + 2)),
                  pl.BlockSpec((tm, LANES), lambda i: (i, Z_GATE // LANES)),
                  pl.BlockSpec((tm, LANES), tab), pl.BlockSpec((tm, LANES), tab), pl.BlockSpec((tm, LANES), tab),
                  pl.BlockSpec((LANES, LANES), const),
                  pl.BlockSpec((1, LANES), const), pl.BlockSpec((N_BRANCH, LANES), const),
                  pl.BlockSpec((1, LANES), const)],
        out_specs=[pl.BlockSpec((1, N_HEADS_ATTN, tm, LANES), lambda i: (i // tiles, 0, i % tiles, 0)),
                   pl.BlockSpec((tm, KV_COLS), row), pl.BlockSpec((tm, KV_COLS), row),
                   pl.BlockSpec((tm, KV_COLS), row),
                   pl.BlockSpec((2, tm, D_KV), lambda i: (0, i, 0)),
                   pl.BlockSpec((2, tm, KV_COLS), lambda i: (0, i, 0)),
                   pl.BlockSpec((tm, LANES), row)],
        out_shape=[jax.ShapeDtypeStruct((batch, N_HEADS_ATTN, seq, LANES), bf16),
                   kv_shape, kv_shape, kv_shape,
                   jax.ShapeDtypeStruct((2, m, D_KV), bf16),
                   jax.ShapeDtypeStruct((2, m, KV_COLS), bf16),
                   jax.ShapeDtypeStruct((m, LANES), f32)],
        compiler_params=pltpu.CompilerParams(
            dimension_semantics=("parallel",), vmem_limit_bytes=VMEM_LIMIT),
        name="attn_prep",
    )(z, z, z, z, z, c, s1, s2, seg_ones, q_g, k_g, gate_b)


CHUNK_COLS = STRIDE_CMP * KV_COLS
CMP_R = BLOCK_CMP // STRIDE_CMP


def compress_weights(cmp_pe, cmp_w):
    pe = cmp_pe.reshape(2, CMP_R, STRIDE_CMP, HEAD_DIM)
    w = cmp_w.reshape(2, CMP_R, STRIDE_CMP, HEAD_DIM, HEAD_DIM)
    eye_c, eye_g = jnp.eye(2, dtype=f32), jnp.eye(N_KV_HEADS, dtype=f32)
    wb = jnp.einsum('cisde,cx,gy->iscgdxye', w, eye_c, eye_g).reshape(CMP_R, CHUNK_COLS, KV_COLS)
    peb = jnp.broadcast_to(jnp.transpose(pe, (1, 2, 0, 3))[:, :, :, None, :],
                           (CMP_R, STRIDE_CMP, 2, N_KV_HEADS, HEAD_DIM)).reshape(CMP_R, 1, CHUNK_COLS)
    return wb.astype(bf16), peb


def _compress_kernel(x_ref, pe_ref, w_ref, kc_ref, vvc_ref):
    x = x_ref[0]
    n = x.shape[0]
    y0 = jnp.dot((x + pe_ref[0]).astype(bf16), w_ref[0], preferred_element_type=f32)
    y1 = jnp.dot((x + pe_ref[1]).astype(bf16), w_ref[1], preferred_element_type=f32)
    comp = y0 + pltpu.roll(y1, n - 1, 0)
    row = lax.broadcasted_iota(jnp.int32, comp.shape, 0)
    comp = jnp.where(row < n - 1, comp, 0.0)
    kc_ref[0] = comp[:, :D_KV].astype(bf16)
    vc = comp[:, D_KV:]
    vvc_ref[0, :, :D_KV] = vc.astype(bf16)
    vvc_ref[0, :, D_KV:] = pltpu.roll(vc, HEAD_DIM, 1).astype(bf16)


def compress(kv, wb, peb, batch):
    n_chunks = kv.shape[0] // batch // STRIDE_CMP
    x = kv.reshape(batch, n_chunks, CHUNK_COLS)
    return pl.pallas_call(
        _compress_kernel,
        grid=(batch,),
        in_specs=[pl.BlockSpec((1, n_chunks, CHUNK_COLS), lambda b: (b, 0, 0)),
                  pl.BlockSpec((CMP_R, 1, CHUNK_COLS), lambda b: (0, 0, 0)),
                  pl.BlockSpec((CMP_R, CHUNK_COLS, KV_COLS), lambda b: (0, 0, 0))],
        out_specs=[pl.BlockSpec((1, n_chunks, D_KV), lambda b: (b, 0, 0)),
                   pl.BlockSpec((1, n_chunks, KV_COLS), lambda b: (b, 0, 0))],
        out_shape=[jax.ShapeDtypeStruct((batch, n_chunks, D_KV), bf16),
                   jax.ShapeDtypeStruct((batch, n_chunks, KV_COLS), bf16)],
        compiler_params=pltpu.CompilerParams(
            dimension_semantics=("parallel",), vmem_limit_bytes=VMEM_LIMIT),
        name="compress",
    )(x, peb, wb)


KEY_CHUNK = 512
NT_DIMS = (((1,), (1,)), ((), ()))


def _softmax_rows(s3, mask):
    s3 = jnp.where(mask[None], s3, NEG)
    m = jnp.max(s3, axis=-1, keepdims=True)
    p = jnp.where(mask[None], jnp.exp(s3 - m), 0.0)
    l = jnp.sum(p, axis=-1, keepdims=True)
    return p * jnp.where(l > 0.0, 1.0 / l, 0.0)


def _nsa_prompt_kernel(q_ref, gate_ref, kc_ref, vvc_ref, ks_ref, vvs_ref, kw_ref, vvw_ref, smap_ref, e_ref,
                       o_ref, mask_ref, m_ref, l_ref, acc_ref, *, seq):
    n_slc = seq // BLOCK_SLC
    n_cmp = seq // STRIDE_CMP
    hq = HEADS_PER_KV * QBLOCK
    q0 = pl.program_id(1) * QBLOCK
    qpos_col = q0 + lax.broadcasted_iota(jnp.int32, (QBLOCK, 1), 0)

    heads_out = []
    for g in range(N_KV_HEADS):
        q = q_ref[0, g * HEADS_PER_KV:(g + 1) * HEADS_PER_KV].reshape(hq, LANES)

        s = lax.dot_general(q, kc_ref[0], NT_DIMS, preferred_element_type=f32)
        c_end = lax.broadcasted_iota(jnp.int32, (1, n_cmp), 1) * STRIDE_CMP + (BLOCK_CMP - 1)
        p_c = _softmax_rows(s.reshape(HEADS_PER_KV, QBLOCK, n_cmp), c_end <= qpos_col)
        p_c = p_c.astype(bf16).reshape(hq, n_cmp)
        o_c = jnp.dot(p_c, vvc_ref[0], preferred_element_type=f32)

        imp = jnp.zeros((n_slc, QBLOCK), f32)
        for h in range(HEADS_PER_KV):
            imp = imp + lax.dot_general(smap_ref[...], p_c[h * QBLOCK:(h + 1) * QBLOCK], NT_DIMS,
                                        preferred_element_type=f32)
        jj = lax.broadcasted_iota(jnp.int32, (n_slc, QBLOCK), 0)
        qp = q0 + lax.broadcasted_iota(jnp.int32, (n_slc, QBLOCK), 1)
        cur = qp // BLOCK_SLC
        forced = (jj == 0) | ((jj <= cur) & (jj > cur - N_LOCAL))
        imp = jnp.where(jj * BLOCK_SLC <= qp, jnp.where(forced, FORCED_SCORE, imp), NEG)
        rank = jnp.zeros((n_slc, QBLOCK), f32)
        for j in range(n_slc):
            row = imp[j:j + 1, :]
            rank = rank + jnp.where((row > imp) | ((row == imp) & (jj > j)), 1.0, 0.0)
        sel_t = jnp.where(rank < float(min(N_SELECT, n_slc)), 1.0, 0.0)
        if n_slc < LANES:
            sel_t = jnp.concatenate([sel_t, jnp.zeros((LANES - n_slc, QBLOCK), f32)], axis=0)
        sel = sel_t.T.astype(bf16)
        for c in range(seq // KEY_CHUNK):
            mask_ref[c] = jnp.dot(sel, e_ref[:, c * KEY_CHUNK:(c + 1) * KEY_CHUNK], preferred_element_type=f32)

        m_ref[...] = jnp.full(m_ref.shape, NEG, f32)
        l_ref[...] = jnp.zeros(l_ref.shape, f32)
        acc_ref[...] = jnp.zeros(acc_ref.shape, f32)

        def sel_step(c, carry):
            k0 = pl.multiple_of(c * KEY_CHUNK, KEY_CHUNK)
            sc = lax.dot_general(q, ks_ref[0, pl.ds(k0, KEY_CHUNK), :], NT_DIMS, preferred_element_type=f32)
            kpos = k0 + lax.broadcasted_iota(jnp.int32, (1, KEY_CHUNK), 1)
            msk = ((mask_ref[c] > 0.5) & (kpos <= qpos_col))[None]
            s3 = jnp.where(msk, sc.reshape(HEADS_PER_KV, QBLOCK, KEY_CHUNK), NEG)
            m_old = m_ref[...]
            m_new = jnp.maximum(m_old, jnp.max(s3, axis=-1, keepdims=True))
            alpha = jnp.exp(m_old - m_new)
            p = jnp.where(msk, jnp.exp(s3 - m_new), 0.0)
            l_ref[...] = alpha * l_ref[...] + jnp.sum(p, axis=-1, keepdims=True)
            pv = jnp.dot(p.astype(bf16).reshape(hq, KEY_CHUNK), vvs_ref[0, pl.ds(k0, KEY_CHUNK), :],
                         preferred_element_type=f32)
            acc_ref[...] = alpha * acc_ref[...] + pv.reshape(HEADS_PER_KV, QBLOCK, KV_COLS)
            m_ref[...] = m_new
            return carry

        lax.fori_loop(0, q0 // KEY_CHUNK + 1, sel_step, 0)
        o_s = (acc_ref[...] / l_ref[...]).reshape(hq, KV_COLS)

        n_win = WINDOW + QBLOCK
        w0 = pl.multiple_of(jnp.maximum(q0 - WINDOW, 0), QBLOCK)
        sw = lax.dot_general(q, kw_ref[0, pl.ds(w0, n_win), :], NT_DIMS, preferred_element_type=f32)
        dist = qpos_col - (w0 + lax.broadcasted_iota(jnp.int32, (1, n_win), 1))
        p_w = _softmax_rows(sw.reshape(HEADS_PER_KV, QBLOCK, n_win), (dist >= 0) & (dist < WINDOW))
        o_w = jnp.dot(p_w.astype(bf16).reshape(hq, n_win), vvw_ref[0, pl.ds(w0, n_win), :],
                      preferred_element_type=f32)

        for hh in range(HEADS_PER_KV):
            h = g * HEADS_PER_KV + hh
            half = 0 if (h % 2) == g else 1
            rows = slice(hh * QBLOCK, (hh + 1) * QBLOCK)
            cols = slice(half * LANES, (half + 1) * LANES)
            gc = gate_ref[:, N_BRANCH * h:N_BRANCH * h + 1]
            gs = gate_ref[:, N_BRANCH * h + 1:N_BRANCH * h + 2]
            gw = gate_ref[:, N_BRANCH * h + 2:N_BRANCH * h + 3]
            heads_out.append(gc * o_c[rows, cols] + gs * o_s[rows, cols] + gw * o_w[rows, cols])

    lane = lax.broadcasted_iota(jnp.int32, (QBLOCK, LANES), 1)
    for pair in range(N_HEADS_ATTN // 2):
        o_ref[0, :, pair * LANES:(pair + 1) * LANES] = jnp.where(
            lane < HEAD_DIM, heads_out[2 * pair], heads_out[2 * pair + 1])


def nsa_tables(seq):
    n_slc, n_cmp = seq // BLOCK_SLC, seq // STRIDE_CMP
    start = np.arange(n_cmp) * STRIDE_CMP
    first, last = start // BLOCK_SLC, (start + BLOCK_CMP - 1) // BLOCK_SLC
    j = np.arange(n_slc)
    smap = (first[None, :] <= j[:, None]) & (last[None, :] >= j[:, None]) & (np.arange(n_cmp)[None, :] < n_cmp - 1)
    expand = (np.arange(seq)[None, :] // BLOCK_SLC) == np.arange(max(n_slc, LANES))[:, None]
    return jnp.asarray(smap, bf16), jnp.asarray(expand, bf16)


def nsa_prompt(q_ext, gates, kc, vvc, kb, vvb, tables, batch, seq):
    smap, expand = tables
    n_cmp = seq // STRIDE_CMP
    hq = HEADS_PER_KV * QBLOCK
    per_b = lambda b, i: (b, 0, 0)
    const = lambda b, i: (0, 0)
    gates3 = gates.reshape(batch, seq, LANES)
    kb4 = kb.reshape(2, batch, seq, D_KV)
    vvb4 = vvb.reshape(2, batch, seq, KV_COLS)
    branch = lambda br: (lambda b, i: (br, b, 0, 0))
    return pl.pallas_call(
        functools.partial(_nsa_prompt_kernel, seq=seq),
        grid=(batch, seq // QBLOCK),
        in_specs=[pl.BlockSpec((1, N_HEADS_ATTN, QBLOCK, LANES), lambda b, i: (b, 0, i, 0)),
                  pl.BlockSpec((None, QBLOCK, LANES), lambda b, i: (b, i, 0)),
                  pl.BlockSpec((1, n_cmp, D_KV), per_b), pl.BlockSpec((1, n_cmp, KV_COLS), per_b),
                  pl.BlockSpec((None, 1, seq, D_KV), branch(0)), pl.BlockSpec((None, 1, seq, KV_COLS), branch(0)),
                  pl.BlockSpec((None, 1, seq, D_KV), branch(1)), pl.BlockSpec((None, 1, seq, KV_COLS), branch(1)),
                  pl.BlockSpec(smap.shape, const), pl.BlockSpec(expand.shape, const)],
        out_specs=pl.BlockSpec((1, QBLOCK, D_ATTN), lambda b, i: (b, i, 0)),
        out_shape=jax.ShapeDtypeStruct((batch, seq, D_ATTN), f32),
        scratch_shapes=[pltpu.VMEM((seq // KEY_CHUNK, QBLOCK, KEY_CHUNK), f32),
                        pltpu.VMEM((HEADS_PER_KV, QBLOCK, 1), f32),
                        pltpu.VMEM((HEADS_PER_KV, QBLOCK, 1), f32),
                        pltpu.VMEM((HEADS_PER_KV, QBLOCK, KV_COLS), f32)],
        compiler_params=pltpu.CompilerParams(
            dimension_semantics=("parallel", "arbitrary"), vmem_limit_bytes=VMEM_LIMIT),
        name="nsa_prompt",
    )(q_ext, gates3, kc, vvc, kb4, vvb4, kb4, vvb4, smap, expand)


RW_CHUNK = 64
N_PAIRS = N_HEADS_RWKV // 2
TN_DIMS = (((0,), (0,)), ((), ()))


def _mm(a, b):
    return jnp.dot(a.astype(bf16), b.astype(bf16), preferred_element_type=f32)


def _mm_nt(a, b):
    return lax.dot_general(a.astype(bf16), b.astype(bf16), NT_DIMS, preferred_element_type=f32)


def _mm_tn(a, b):
    return lax.dot_general(a.astype(bf16), b.astype(bf16), TN_DIMS, preferred_element_type=f32)


def _mm_split(a01, x):
    hi = x.astype(bf16)
    lo = (x - hi.astype(f32)).astype(bf16)
    return jnp.dot(a01, hi, preferred_element_type=f32) + jnp.dot(a01, lo, preferred_element_type=f32)


def _seg_sum(x, p_ref):
    hi = x.astype(bf16)
    lo = (x - hi.astype(f32)).astype(bf16)
    return (jnp.dot(hi, p_ref[...], preferred_element_type=f32)
            + jnp.dot(lo, p_ref[...], preferred_element_type=f32))


def _stack_heads(x):
    lane = lax.broadcasted_iota(jnp.int32, x.shape, 1)
    return jnp.concatenate([jnp.where(lane < HEAD_DIM, x, 0.0), jnp.where(lane >= HEAD_DIM, x, 0.0)], axis=0)


def _chunk_pair(r, cum, lw, alpha, beta, k, v, st):
    c = r.shape[0]
    eg, egi, egp = jnp.exp(cum), jnp.exp(-cum), jnp.exp(cum - lw)
    g_end = eg[c - 1:c, :]
    x_a, x_r = _stack_heads(alpha * egp), _stack_heads(r * eg)
    bt, kt = beta * egi, k * egi
    b_b, b_k = jnp.concatenate([bt, bt], axis=0), jnp.concatenate([kt, kt], axis=0)
    x_bg, x_kg, v_s = _stack_heads(bt * g_end), _stack_heads(kt * g_end), _stack_heads(v)
    row = lax.broadcasted_iota(jnp.int32, (2 * c, 2 * c), 0)
    col = lax.broadcasted_iota(jnp.int32, (2 * c, 2 * c), 1)
    same = (row // c) == (col // c)
    strict = same & (col < row)
    incl = same & (col <= row)
    n = jnp.where(strict, _mm_nt(x_a, b_b), 0.0)
    a_ak = jnp.where(strict, _mm_nt(x_a, b_k), 0.0)
    a_rb = jnp.where(incl, _mm_nt(x_r, b_b), 0.0)
    a_rk = jnp.where(incl, _mm_nt(x_r, b_k), 0.0)
    def level(k):
        return jnp.where(((row >> (k + 1)) == (col >> (k + 1))) & ((row >> k) != (col >> k)), n, 0.0)
    inv = jnp.where(row == col, 1.0, 0.0) + level(0)
    for k in range(1, int(np.log2(c))):
        inv = inv + _mm(_mm(inv, level(k)), inv)
    st_b = st.astype(bf16)
    p = _mm(x_a, st_b) + _mm(a_ak, v_s)
    e = _mm(inv, p)
    y_s = _mm(x_r, st_b) + _mm(a_rb, e) + _mm(a_rk, v_s)
    g_col = jnp.broadcast_to(g_end, (2 * c, 2 * c)).T
    st_new = st * g_col + _mm_tn(x_bg, e) + _mm_tn(x_kg, v_s)
    return y_s[:c] + y_s[c:], st_new


def _rwkv_prompt_kernel(*refs, first_layer):
    zr_ref, zk_ref, zv_ref, zl_ref, sp_ref, mu_ref, vec_ref, lr_ref, p_ref, tri_ref = refs[:10]
    if first_layer:
        o_ref, st_ref, vfo_ref, prev_ref = refs[10:]
    else:
        vf_ref, o_ref, st_ref, prev_ref = refs[10:]
    c = RW_CHUNK
    ci = pl.program_id(1)

    @pl.when(ci == 0)
    def _():
        prev_ref[0:1, :] = sp_ref[...]
        st_ref[...] = jnp.zeros(st_ref.shape, f32)

    def shifted(z_ref, lo, hi):
        z = z_ref[...]
        first = lax.broadcasted_iota(jnp.int32, z.shape, 0) == 0
        prev = jnp.where(first, prev_ref[0:1, lo:hi], pltpu.roll(z, 1, 0))
        xs = z + (prev - z) * mu_ref[:, lo:hi]
        prev_ref[0:1, lo:hi] = z[c - 1:c, :]
        return xs

    xr = shifted(zr_ref, 0, D_RWKV)
    xk = shifted(zk_ref, D_RWKV, 2 * D_RWKV)
    xv = shifted(zv_ref, 2 * D_RWKV, 3 * D_RWKV)
    xl = shifted(zl_ref, 3 * D_RWKV, SHIFT_COLS)
    w0, a0, vres0, k_k = vec_ref[0:1, :], vec_ref[1:2, :], vec_ref[2:3, :], vec_ref[3:4, :]
    k_a, r_k, lnx_w, lnx_b = vec_ref[4:5, :], vec_ref[5:6, :], vec_ref[6:7, :], vec_ref[7:8, :]

    xl_b = xl.astype(bf16)
    u = -(w0 + jnp.dot(jnp.tanh(xl).astype(bf16), lr_ref[0], preferred_element_type=f32))
    softplus = jnp.maximum(u, 0.0) + jnp.log(1.0 + jnp.exp(-jnp.abs(u)))
    lw = -jnp.exp(-softplus - 0.5)
    a = jax.nn.sigmoid(a0 + jnp.dot(xl_b, lr_ref[1], preferred_element_type=f32))
    if first_layer:
        v = xv
        vfo_ref[...] = xv
    else:
        v = xv + (vf_ref[...] - xv) * jax.nn.sigmoid(vres0 + jnp.dot(xl_b, lr_ref[2], preferred_element_type=f32))
    gate = jnp.dot(jax.nn.sigmoid(xl).astype(bf16), lr_ref[3], preferred_element_type=f32)
    kk = xk * k_k
    k2 = xk * (1.0 + (a - 1.0) * k_a)
    cum = _mm_split(tri_ref[...], lw)

    for pair in range(N_PAIRS):
        cols = slice(pair * LANES, (pair + 1) * LANES)
        kk_p = kk[:, cols]
        kk_p = kk_p * lax.rsqrt(jnp.maximum(_seg_sum(kk_p * kk_p, p_ref), 1e-24))
        r_p, k_p, v_p = xr[:, cols], k2[:, cols], v[:, cols]
        y, st_new = _chunk_pair(r_p, cum[:, cols], lw[:, cols], -kk_p, kk_p * a[:, cols], k_p, v_p, st_ref[pair])
        st_ref[pair] = st_new
        mean = _seg_sum(y, p_ref) * (1.0 / HEAD_DIM)
        d = y - mean
        var = _seg_sum(d * d, p_ref) * (1.0 / HEAD_DIM)
        yn = d * lax.rsqrt(var + GN_EPS) * lnx_w[:, cols] + lnx_b[:, cols]
        bonus = _seg_sum(r_p * k_p * r_k[:, cols], p_ref) * v_p
        o_ref[:, cols] = (yn + bonus) * gate[:, cols]


def rwkv_prompt(z, v_first, lp, consts, batch, seq):
    m = batch * seq
    c = RW_CHUNK
    n_chunks = seq // c
    first_layer = v_first is None
    rows = lambda j: (lambda b, i: (b * n_chunks + i, j))
    const2 = lambda b, i: (0, 0)
    tok = pl.BlockSpec((c, D_RWKV), rows(0))
    tok_shape = jax.ShapeDtypeStruct((m, D_RWKV), f32)
    out = pl.pallas_call(
        functools.partial(_rwkv_prompt_kernel, first_layer=first_layer),
        grid=(batch, n_chunks),
        in_specs=[pl.BlockSpec((c, D_RWKV), rows(Z_R // D_RWKV)),
                  pl.BlockSpec((c, D_RWKV), rows(Z_K // D_RWKV)),
                  pl.BlockSpec((c, D_RWKV), rows(Z_V // D_RWKV)),
                  pl.BlockSpec((c, LANES), rows(Z_LR // LANES)),
                  pl.BlockSpec((None, 1, SHIFT_COLS), lambda b, i: (b, 0, 0)),
                  pl.BlockSpec((1, SHIFT_COLS), const2),
                  pl.BlockSpec((8, D_RWKV), const2),
                  pl.BlockSpec((4, LANES, D_RWKV), lambda b, i: (0, 0, 0)),
                  pl.BlockSpec((LANES, LANES), const2),
                  pl.BlockSpec((c, c), const2)] + ([] if first_layer else [tok]),
        out_specs=[tok, pl.BlockSpec((None, N_PAIRS, LANES, LANES), lambda b, i: (b, 0, 0, 0))]
                  + ([tok] if first_layer else []),
        out_shape=[tok_shape, jax.ShapeDtypeStruct((batch, N_PAIRS, LANES, LANES), f32)]
                  + ([tok_shape] if first_layer else []),
        scratch_shapes=[pltpu.VMEM((8, SHIFT_COLS), f32)],
        compiler_params=pltpu.CompilerParams(
            dimension_semantics=("parallel", "arbitrary"), vmem_limit_bytes=VMEM_LIMIT),
        name="rwkv_prompt",
    )(z, z, z, z, consts['shift_zero'], lp['mu_row'], lp['rwkv_vecs'], lp['rwkv_lowrank'],
      consts['seg_ones'], consts['tri'], *([] if first_layer else [v_first]))
    return out[0], out[1], (out[2] if first_layer else v_first)


def state_from_pair_tiles(st):
    b = st.shape[0]
    t = st.reshape(b, N_PAIRS, 2, HEAD_DIM, 2, HEAD_DIM)
    diag = jnp.stack([t[:, :, 0, :, 0, :], t[:, :, 1, :, 1, :]], axis=2)
    return jnp.swapaxes(diag, -1, -2).reshape(b, N_HEADS_RWKV, HEAD_DIM, HEAD_DIM)


def rwkv_params(lp_raw, layer):
    zrow = jnp.zeros((D_RWKV,), f32)
    vres0 = lp_raw['vres0'][layer - 1] if layer > 0 else zrow
    vecs = jnp.stack([lp_raw['w0'][layer], lp_raw['a0'][layer], vres0, lp_raw['k_k'][layer], lp_raw['k_a'][layer],
                      lp_raw['r_k'][layer].reshape(D_RWKV), lp_raw['lnx_w'][layer], lp_raw['lnx_b'][layer]])
    def rows(w, lo):
        return jnp.zeros((LANES, D_RWKV), f32).at[lo:lo + w.shape[0]].set(w)
    vres2 = lp_raw['vres2'][layer - 1] if layer > 0 else jnp.zeros((RANK_A, D_RWKV), f32)
    lowrank = jnp.stack([rows(lp_raw['w2'][layer], 0), rows(lp_raw['a2'][layer], RANK_W),
                         rows(vres2, RANK_W), rows(lp_raw['g2'][layer], RANK_W + RANK_A)]).astype(bf16)
    return vecs, lowrank


CHUNKS_PER_PAGE = PAGE_SIZE // STRIDE_CMP
PAGES_PER_STEP = 8
BLOCKS_PER_STEP = PAGES_PER_STEP * PAGE_SIZE // BLOCK_SLC
N_BLK_PAD = 256


POOL_PAGES_PER_STEP = 64


def _compress_pool_kernel(x_ref, pe_ref, w_ref, y_ref, xk_ref, xv_ref):
    xs_refs = (xk_ref, xv_ref)
    n_pages = x_ref.shape[0]
    n_chunks = n_pages * CHUNKS_PER_PAGE

    def to_token_major(p, carry):
        r0 = pl.multiple_of(p * PAGE_SIZE, PAGE_SIZE)
        page = x_ref[p]
        for c in range(2):
            tile = jnp.concatenate([page[c, g] for g in range(N_KV_HEADS)], axis=0)
            xs_refs[c][pl.ds(r0, PAGE_SIZE), :] = tile.T
        return carry

    lax.fori_loop(0, n_pages, to_token_major, 0)
    for c in range(2):
        acc = [jnp.zeros((n_chunks, D_KV), f32) for _ in range(CMP_R)]
        for s in range(STRIDE_CMP):
            xs = xs_refs[c][pl.ds(s, n_chunks, stride=STRIDE_CMP), :]
            rows = slice(s * KV_COLS + c * D_KV, s * KV_COLS + (c + 1) * D_KV)
            for i in range(CMP_R):
                acc[i] = acc[i] + jnp.dot((xs + pe_ref[i, :, rows]).astype(bf16),
                                          w_ref[i, rows, c * D_KV:(c + 1) * D_KV], preferred_element_type=f32)
        for i in range(CMP_R):
            y_ref[:, i * KV_COLS + c * D_KV:i * KV_COLS + (c + 1) * D_KV] = acc[i]


def compress_pool(cache_cmp, wb, peb):
    depth, n_pool = cache_cmp.shape[:2]
    rows = n_pool * CHUNKS_PER_PAGE
    pp = POOL_PAGES_PER_STEP
    tm = pp * CHUNKS_PER_PAGE
    assert n_pool % pp == 0
    return pl.pallas_call(
        _compress_pool_kernel,
        grid=(depth, n_pool // pp),
        in_specs=[pl.BlockSpec((None, pp, 2, N_KV_HEADS, HEAD_DIM, PAGE_SIZE), lambda l, i: (l, i, 0, 0, 0, 0)),
                  pl.BlockSpec((None, CMP_R, 1, CHUNK_COLS), lambda l, i: (l, 0, 0, 0)),
                  pl.BlockSpec((None, CMP_R, CHUNK_COLS, KV_COLS), lambda l, i: (l, 0, 0, 0))],
        out_specs=pl.BlockSpec((None, tm, 2 * KV_COLS), lambda l, i: (l, i, 0)),
        out_shape=jax.ShapeDtypeStruct((depth, rows, 2 * KV_COLS), f32),
        scratch_shapes=[pltpu.VMEM((pp * PAGE_SIZE, D_KV), f32), pltpu.VMEM((pp * PAGE_SIZE, D_KV), f32)],
        compiler_params=pltpu.CompilerParams(
            dimension_semantics=("parallel", "parallel"), vmem_limit_bytes=VMEM_LIMIT),
        name="compress_pool",
    )(cache_tiles(cache_cmp), peb, wb)


def _softmax_2d(s, mask):
    s = jnp.where(mask, s, NEG)
    m = jnp.max(s, axis=-1, keepdims=True)
    p = jnp.where(mask, jnp.exp(s - m), 0.0)
    l = jnp.sum(p, axis=-1, keepdims=True)
    return p * jnp.where(l > 0.0, 1.0 / l, 0.0)


def _pad_rows(x, rows):
    return jnp.concatenate([x, jnp.zeros((rows - x.shape[0], x.shape[1]), x.dtype)], axis=0)


def head_dim_major(tile_ref, which):
    return jnp.concatenate([tile_ref[which, g] for g in range(N_KV_HEADS)], axis=0).astype(bf16)


def cache_tiles(cache):
    nd = cache.ndim
    return jnp.transpose(cache, tuple(range(nd - 4)) + (nd - 3, nd - 2, nd - 1, nd - 4))


def _sample_attn_kernel(pt_ref, q_ref, gate_ref, y_ref, *rest, past, n_new):
    pages = rest[:PAGES_PER_STEP]
    (ksn_ref, kwn_ref, cw_ref, smap_ref, o_ref,
     oc_ref, ow_ref, sel_ref, imp_ref, m_ref, l_ref, acc_ref) = rest[PAGES_PER_STEP:]
    del pt_ref
    c = pl.program_id(1)
    n_steps = pl.num_programs(1)
    rows = N_HEADS_ATTN * n_new
    q = q_ref[...]
    qpos = past + lax.broadcasted_iota(jnp.int32, (rows, 1), 0) % n_new
    n_blk = -(-(past + n_new) // BLOCK_SLC)

    @pl.when(c == 0)
    def _():
        y = y_ref[...]
        n_chunks = y.shape[0]
        comp = y[:, :KV_COLS] + pltpu.roll(y[:, KV_COLS:], n_chunks - 1, 0)
        s = lax.dot_general(q, comp[:, :D_KV].astype(bf16), NT_DIMS, preferred_element_type=f32)
        c_end = lax.broadcasted_iota(jnp.int32, (1, n_chunks), 1) * STRIDE_CMP + (BLOCK_CMP - 1)
        valid = (c_end <= qpos) & (lax.broadcasted_iota(jnp.int32, (1, n_chunks), 1) < n_chunks - 1)
        p_c = _softmax_2d(s, valid)
        oc_ref[...] = jnp.dot(p_c.astype(bf16), comp[:, D_KV:].astype(bf16), preferred_element_type=f32)

        hq = HEADS_PER_KV * n_new
        p_sum = [sum(p_c[g * hq + hh * n_new:g * hq + (hh + 1) * n_new] for hh in range(HEADS_PER_KV))
                 for g in range(N_KV_HEADS)]
        p_sum = _pad_rows(jnp.concatenate(p_sum, axis=0), LANES).astype(bf16)
        imp = lax.dot_general(smap_ref[...], p_sum, NT_DIMS, preferred_element_type=f32)
        jj = lax.broadcasted_iota(jnp.int32, (N_BLK_PAD, LANES), 0)
        qp = past + lax.broadcasted_iota(jnp.int32, (N_BLK_PAD, LANES), 1) % n_new
        cur = qp // BLOCK_SLC
        forced = (jj == 0) | ((jj <= cur) & (jj > cur - N_LOCAL))
        imp = jnp.where(jj * BLOCK_SLC <= qp, jnp.where(forced, FORCED_SCORE, imp), NEG)
        imp_ref[...] = imp

        def rank_step(j, rank):
            row = imp_ref[pl.ds(j, 1), :]
            return rank + jnp.where((row > imp) | ((row == imp) & (jj > j)), 1.0, 0.0)

        rank = lax.fori_loop(0, n_blk, rank_step, jnp.zeros((N_BLK_PAD, LANES), f32))
        sel = jnp.where(rank < float(min(N_SELECT, n_blk)), 1.0, 0.0).T
        sel_ref[...] = jnp.concatenate(
            [sel[g * n_new:(g + 1) * n_new] for g in range(N_KV_HEADS) for _ in range(HEADS_PER_KV)], axis=0)

        w_buf = cw_ref.shape[-1]
        kt_w, vt_w = head_dim_major(cw_ref, 0), head_dim_major(cw_ref, 1)
        kwn = _pad_rows(kwn_ref[...], LANES)
        s1 = jnp.dot(q, kt_w, preferred_element_type=f32)
        s2 = lax.dot_general(q, kwn[:, :D_KV].astype(bf16), NT_DIMS, preferred_element_type=f32)
        wpos = jnp.concatenate([past - w_buf + lax.broadcasted_iota(jnp.int32, (1, w_buf), 1),
                                past + lax.broadcasted_iota(jnp.int32, (1, LANES), 1)], axis=1)
        dist = qpos - wpos
        p_w = _softmax_2d(jnp.concatenate([s1, s2], axis=1), (dist >= 0) & (dist < WINDOW)).astype(bf16)
        ow_ref[...] = (lax.dot_general(p_w[:, :w_buf], vt_w, NT_DIMS, preferred_element_type=f32)
                       + jnp.dot(p_w[:, w_buf:], kwn[:, D_KV:].astype(bf16), preferred_element_type=f32))

        m_ref[...] = jnp.full(m_ref.shape, NEG, f32)
        l_ref[...] = jnp.zeros(l_ref.shape, f32)
        acc_ref[...] = jnp.zeros(acc_ref.shape, f32)

    def online_update(s, msk, pv):
        s = jnp.where(msk, s, NEG)
        m_old = m_ref[...]
        m_new = jnp.maximum(m_old, jnp.max(s, axis=-1, keepdims=True))
        alpha = jnp.exp(m_old - m_new)
        p = jnp.where(msk, jnp.exp(s - m_new), 0.0)
        l_ref[...] = alpha * l_ref[...] + jnp.sum(p, axis=-1, keepdims=True)
        acc_ref[...] = alpha * acc_ref[...] + pv(p.astype(bf16))
        m_ref[...] = m_new

    kt = jnp.concatenate([head_dim_major(pg, 0) for pg in pages], axis=1)
    vt = jnp.concatenate([head_dim_major(pg, 1) for pg in pages], axis=1)
    n_keys = kt.shape[1]
    blk = lax.broadcasted_iota(jnp.int32, (N_BLK_PAD, n_keys), 0)
    key_blk = c * BLOCKS_PER_STEP + lax.broadcasted_iota(jnp.int32, (N_BLK_PAD, n_keys), 1) // BLOCK_SLC
    expand = jnp.where(blk == key_blk, 1.0, 0.0).astype(bf16)
    msk = jnp.dot(sel_ref[...].astype(bf16), expand, preferred_element_type=f32) > 0.5
    online_update(jnp.dot(q, kt, preferred_element_type=f32), msk,
                  lambda p: lax.dot_general(p, vt, NT_DIMS, preferred_element_type=f32))

    @pl.when(c == n_steps - 1)
    def _():
        ksn = _pad_rows(ksn_ref[...], LANES)
        kpos = past + lax.broadcasted_iota(jnp.int32, (1, LANES), 1)
        last_blk = past // BLOCK_SLC
        msk_new = (sel_ref[:, last_blk:last_blk + 1] > 0.5) & (kpos <= qpos)
        online_update(lax.dot_general(q, ksn[:, :D_KV].astype(bf16), NT_DIMS, preferred_element_type=f32), msk_new,
                      lambda p: jnp.dot(p, ksn[:, D_KV:].astype(bf16), preferred_element_type=f32))
        o_s = acc_ref[...] / l_ref[...]
        o_c, o_w = oc_ref[...], ow_ref[...]
        lane = lax.broadcasted_iota(jnp.int32, (n_new, LANES), 1)
        heads = []
        for h in range(N_HEADS_ATTN):
            r = slice(h * n_new, (h + 1) * n_new)
            gc = gate_ref[:, N_BRANCH * h:N_BRANCH * h + 1]
            gs = gate_ref[:, N_BRANCH * h + 1:N_BRANCH * h + 2]
            gw = gate_ref[:, N_BRANCH * h + 2:N_BRANCH * h + 3]
            o_h = gc * o_c[r] + gs * o_s[r] + gw * o_w[r]
            heads.append(o_h if (h // HEADS_PER_KV) == (h % 2) else pltpu.roll(o_h, HEAD_DIM, 1))
        for pair in range(N_HEADS_ATTN // 2):
            o_ref[:, pair * LANES:(pair + 1) * LANES] = jnp.where(lane < HEAD_DIM, heads[2 * pair], heads[2 * pair + 1])


def sample_tables(past, n_new):
    n_chunks = past // STRIDE_CMP
    start = np.arange(n_chunks) * STRIDE_CMP
    first, last = start // BLOCK_SLC, (start + BLOCK_CMP - 1) // BLOCK_SLC
    j = np.arange(N_BLK_PAD)
    smap = (first[None, :] <= j[:, None]) & (last[None, :] >= j[:, None]) & (np.arange(n_chunks)[None, :] < n_chunks - 1)
    return jnp.asarray(smap, bf16)


def sample_attn(q_rows, gates, y_req, cache_slc, layer, page_table, kv_s_new, kv_w_new, cache_win, smap):
    bd, n_pages = page_table.shape
    n_new = kv_s_new.shape[1]
    past = n_pages * PAGE_SIZE
    rows = N_HEADS_ATTN * n_new
    w_buf = cache_win.shape[-1]
    tile = (None, None, 2, N_KV_HEADS, HEAD_DIM)
    assert n_pages % PAGES_PER_STEP == 0 and -(-(past + n_new) // BLOCK_SLC) <= N_BLK_PAD and n_new <= LANES
    per_b = lambda b, c, pt: (b, 0, 0)
    page = lambda j: (lambda b, c, pt: (layer, pt[b, c * PAGES_PER_STEP + j], 0, 0, 0, 0))
    grid_spec = pltpu.PrefetchScalarGridSpec(
        num_scalar_prefetch=1,
        grid=(bd, n_pages // PAGES_PER_STEP),
        in_specs=[pl.BlockSpec((None, rows, LANES), per_b),
                  pl.BlockSpec((None, n_new, LANES), per_b),
                  pl.BlockSpec((None, past // STRIDE_CMP, 2 * KV_COLS), per_b)]
                 + [pl.BlockSpec(tile + (PAGE_SIZE,), page(j)) for j in range(PAGES_PER_STEP)]
                 + [pl.BlockSpec((None, n_new, KV_COLS), per_b),
                    pl.BlockSpec((None, n_new, KV_COLS), per_b),
                    pl.BlockSpec(tile + (w_buf,), lambda b, c, pt: (layer, b, 0, 0, 0, 0)),
                    pl.BlockSpec(smap.shape, lambda b, c, pt: (0, 0))],
        out_specs=pl.BlockSpec((None, n_new, D_ATTN), per_b),
        scratch_shapes=[pltpu.VMEM((rows, LANES), f32), pltpu.VMEM((rows, LANES), f32),
                        pltpu.VMEM((rows, N_BLK_PAD), f32), pltpu.VMEM((N_BLK_PAD, LANES), f32),
                        pltpu.VMEM((rows, 1), f32), pltpu.VMEM((rows, 1), f32), pltpu.VMEM((rows, LANES), f32)])
    return pl.pallas_call(
        functools.partial(_sample_attn_kernel, past=past, n_new=n_new),
        grid_spec=grid_spec,
        out_shape=jax.ShapeDtypeStruct((bd, n_new, D_ATTN), f32),
        compiler_params=pltpu.CompilerParams(
            dimension_semantics=("parallel", "arbitrary"), vmem_limit_bytes=VMEM_LIMIT),
        name="sample_attn",
    )(page_table, q_rows, gates, y_req, *([cache_slc] * PAGES_PER_STEP), kv_s_new, kv_w_new, cache_win, smap)


def _split(x, sizes):
    offs = [int(o) for o in np.cumsum(sizes)[:-1]]
    return jnp.split(x, offs, axis=-1)


def rwkv7_time_mix(zs, shift_prev, wkv0, v_first, lp):
    B, T = zs.shape[:2]
    zf = zs.astype(f32)
    prev = jnp.concatenate([shift_prev.astype(f32)[:, None], zf[:, :-1]], axis=1)
    xs = zf + (prev - zf) * lp['mu']
    r, k, v, xw, xa, xg = _split(xs, [D_RWKV] * 3 + [RANK_W, RANK_A, RANK_G])
    w_raw = -jax.nn.softplus(-(lp['w0'] + jnp.tanh(xw) @ lp['w2'])) - 0.5
    decay = jnp.exp(-jnp.exp(w_raw))
    a = jax.nn.sigmoid(lp['a0'] + xa @ lp['a2'])
    if lp['vres'] is None:
        v_first = v
    else:
        vres0, vres2 = lp['vres']
        v = v + (v_first - v) * jax.nn.sigmoid(vres0 + xa @ vres2)
    g = jax.nn.sigmoid(xg) @ lp['g2']
    heads = lambda t: t.reshape(B, T, N_HEADS_RWKV, HEAD_DIM)
    kk = heads(k * lp['k_k'])
    kk = kk * lax.rsqrt(jnp.maximum(jnp.sum(kk * kk, axis=-1, keepdims=True), 1e-24))
    k = k * (1.0 + (a - 1.0) * lp['k_a'])
    r_h, k_h, v_h, a_h, d_h = heads(r), heads(k), heads(v), heads(a), heads(decay)

    def step(S, inp):
        r_t, d_t, kk_t, a_t, k_t, v_t = inp
        sa = jnp.einsum('bhvk,bhk->bhv', S, -kk_t)
        S = S * d_t[:, :, None, :] + sa[..., None] * (kk_t * a_t)[:, :, None, :] + v_t[..., None] * k_t[:, :, None, :]
        return S, jnp.einsum('bhvk,bhk->bhv', S, r_t)

    tm = lambda t: jnp.moveaxis(t, 1, 0)
    S_T, ys = lax.scan(step, wkv0.astype(f32), (tm(r_h), tm(d_h), tm(kk), tm(a_h), tm(k_h), tm(v_h)))
    y = jnp.moveaxis(ys, 0, 1)
    mean = jnp.mean(y, axis=-1, keepdims=True)
    var = jnp.mean(jnp.square(y - mean), axis=-1, keepdims=True)
    y = (y - mean) * lax.rsqrt(var + GN_EPS) * lp['lnx_w'].reshape(N_HEADS_RWKV, HEAD_DIM) + lp['lnx_b'].reshape(N_HEADS_RWKV, HEAD_DIM)
    y = y + jnp.sum(r_h * k_h * lp['r_k'], axis=-1, keepdims=True) * v_h
    y = y.reshape(B, T, D_RWKV) * g
    return y, S_T, zs[:, -1], v_first


def sample_trunk_layer(x, layer, shift_prev, wkv0, v_first, lp, consts, caches):
    B, Tn = x.shape[:2]
    rows = B * Tn
    page_table = caches['page_table']
    x2 = x.reshape(rows, D_MODEL)
    z = norm_proj(x2, lp['norm1_g'], lp['w_in'])
    q_ext, kv_c, kv_s, kv_w, _, _, gates = attn_prep(
        z, consts['rope_s'], consts['seg_ones'], lp['q_g_lanes'], lp['k_g_lanes'], lp['gate_b_lanes'], 1, rows)
    q_rows = q_ext.reshape(N_HEADS_ATTN, B, Tn, LANES).transpose(1, 0, 2, 3).reshape(B, N_HEADS_ATTN * Tn, LANES)
    n_pool = caches['slc'].shape[1]
    y_req = caches['y_pool'][layer].reshape(n_pool, CHUNKS_PER_PAGE, 2 * KV_COLS)[page_table]
    y_req = y_req.reshape(B, page_table.shape[1] * CHUNKS_PER_PAGE, 2 * KV_COLS)
    new_rows = lambda kv: kv.reshape(B, Tn, KV_COLS)
    o_attn = sample_attn(q_rows, gates.reshape(B, Tn, LANES), y_req, caches['slc'], layer, page_table,
                         new_rows(kv_s), new_rows(kv_w), caches['win'], consts['smap_s'])
    tok = lambda kv: kv.reshape(B, Tn, 2, N_KV_HEADS, HEAD_DIM)
    win = jnp.concatenate([caches['win_raw'][layer][:, Tn:], tok(kv_w)], axis=1)
    attn_state = (tok(kv_c), tok(kv_s), win)
    z_rwkv = split_z(z)[2].reshape(B, Tn, SHIFT_COLS)
    o_rwkv, wkv_t, shift_t, v_first = rwkv7_time_mix(z_rwkv, shift_prev, wkv0, v_first, lp)
    h = out_proj(x2, o_attn.reshape(rows, D_ATTN), o_rwkv.reshape(rows, D_RWKV), lp['w_out'])
    y = ffn(h, lp['norm2_g'], lp['w_up'], lp['w_down']).reshape(B, Tn, D_MODEL)
    return y, attn_state, wkv_t, shift_t, v_first


def prompt_trunk_layer(x, v_first, lp, consts):
    B, T = x.shape[:2]
    x2 = x.reshape(B * T, D_MODEL)
    z = norm_proj(x2, lp['norm1_g'], lp['w_in'])
    q_ext, kv_c, kv_s, kv_w, kb, vvb, gates = attn_prep(
        z, consts['rope'], consts['seg_ones'], lp['q_g_lanes'], lp['k_g_lanes'], lp['gate_b_lanes'], B, T)
    kc, vvc = compress(kv_c, lp['cmp_wb'], lp['cmp_peb'], B)
    o_attn = nsa_prompt(q_ext, gates, kc, vvc, kb, vvb, consts['nsa'], B, T)
    pages = lambda kv: kv.reshape(B, T // PAGE_SIZE, PAGE_SIZE, 2, N_KV_HEADS, HEAD_DIM)
    attn_state = (pages(kv_c), pages(kv_s),
                  kv_w.reshape(B, T, 2, N_KV_HEADS, HEAD_DIM)[:, -min(WINDOW, T):])
    o_rwkv, st, v_first = rwkv_prompt(z, v_first, lp, consts, B, T)
    z_last = z.reshape(B, T, IN_COLS_PAD)[:, -1]
    shift_t = jnp.concatenate([z_last[:, Z_R:Z_KV], z_last[:, Z_LR:]], axis=-1)
    h = out_proj(x2, o_attn.reshape(B * T, D_ATTN), o_rwkv, lp['w_out'])
    y = ffn(h, lp['norm2_g'], lp['w_up'], lp['w_down']).reshape(B, T, D_MODEL)
    return y, attn_state, state_from_pair_tiles(st), shift_t, v_first


def kernel(x_prompt, x_sample, cache_cmp_kv, cache_slc_kv, cache_win_kv, state_wkv, state_shift, page_table,
           norm1_g, w_in, q_norm_g, k_norm_g, gate_b, cmp_pe, cmp_w, shift_mu, w0, w2, a0, a2, vres0, vres2,
           g2, k_k, k_a, r_k, lnx_w, lnx_b, w_out, norm2_g, w_up, w_down):
    B, T = x_prompt.shape[:2]
    Tn = x_sample.shape[1]
    past = page_table.shape[1] * PAGE_SIZE
    pos_p = jnp.arange(T, dtype=jnp.int32)
    pos_s = past + jnp.arange(Tn, dtype=jnp.int32)
    w_in_b = permute_w_in(w_in).astype(bf16)
    w_out_b, w_up_b, w_down_b = w_out.astype(bf16), w_up.astype(bf16), w_down.astype(bf16)
    seg = np.arange(LANES) // HEAD_DIM
    consts = {'rope': rope_lane_tables(pos_p), 'nsa': nsa_tables(T),
              'seg_ones': jnp.asarray(seg[:, None] == seg[None, :], bf16),
              'tri': jnp.asarray(np.tril(np.ones((RW_CHUNK, RW_CHUNK))), bf16),
              'shift_zero': jnp.zeros((B, 1, SHIFT_COLS), f32)}
    rwkv_raw = {'w0': w0, 'w2': w2, 'a0': a0, 'a2': a2, 'vres0': vres0, 'vres2': vres2, 'g2': g2, 'k_k': k_k,
                'k_a': k_a, 'r_k': r_k, 'lnx_w': lnx_w, 'lnx_b': lnx_b}
    Bd = x_sample.shape[0]
    n_pool, w_buf = cache_slc_kv.shape[1], cache_win_kv.shape[2]
    consts['rope_s'] = tuple(jnp.tile(t, (Bd, 1)) for t in rope_lane_tables(pos_s))
    consts['smap_s'] = sample_tables(past, Tn)
    cmp_wb, cmp_peb = jax.vmap(compress_weights)(cmp_pe, cmp_w)
    caches = {'page_table': page_table,
              'slc': cache_tiles(cache_slc_kv), 'win': cache_tiles(cache_win_kv), 'win_raw': cache_win_kv,
              'y_pool': compress_pool(cache_cmp_kv, cmp_wb, cmp_peb)}
    xp, xs = x_prompt, x_sample
    vf_p, vf_s = None, None
    p_cmp, p_slc, p_win, p_wkv, p_shift = [], [], [], [], []
    s_cmp, s_slc, s_win, s_wkv, s_shift = [], [], [], [], []
    for l in range(DEPTH):
        lp = {'norm1_g': norm1_g[l], 'w_in': w_in_b[l], 'q_norm_g': q_norm_g[l], 'k_norm_g': k_norm_g[l],
              'gate_b': gate_b[l], 'cmp_pe': cmp_pe[l], 'cmp_w': cmp_w[l], 'mu': shift_mu[l], 'w0': w0[l],
              'w2': w2[l], 'a0': a0[l], 'a2': a2[l],
              'vres': None if l == 0 else (vres0[l - 1], vres2[l - 1]),
              'g2': g2[l], 'k_k': k_k[l], 'k_a': k_a[l], 'r_k': r_k[l], 'lnx_w': lnx_w[l], 'lnx_b': lnx_b[l],
              'w_out': w_out_b[l], 'norm2_g': norm2_g[l], 'w_up': w_up_b[l], 'w_down': w_down_b[l]}
        lp['cmp_wb'], lp['cmp_peb'] = cmp_wb[l], cmp_peb[l]
        lp['q_g_lanes'] = jnp.tile(q_norm_g[l], 2).reshape(1, LANES)
        lp['k_g_lanes'] = jnp.tile(k_norm_g[l], (1, 2))
        lp['gate_b_lanes'] = jnp.pad(gate_b[l], (0, LANES - N_BRANCH * N_HEADS_ATTN)).reshape(1, LANES)
        lp['mu_row'] = shift_mu[l].reshape(1, SHIFT_COLS)
        lp['rwkv_vecs'], lp['rwkv_lowrank'] = rwkv_params(rwkv_raw, l)
        xp, (c_kv, sl_kv, w_kv), wkv_t, sh_t, vf_p = prompt_trunk_layer(xp, vf_p, lp, consts)
        p_cmp.append(c_kv); p_slc.append(sl_kv); p_win.append(w_kv); p_wkv.append(wkv_t); p_shift.append(sh_t)
        xs, (c_kv, sl_kv, w_kv), wkv_t, sh_t, vf_s = sample_trunk_layer(
            xs, l, state_shift[l], state_wkv[l], vf_s, lp, consts, caches)
        s_cmp.append(c_kv); s_slc.append(sl_kv); s_win.append(w_kv); s_wkv.append(wkv_t); s_shift.append(sh_t)
    return (xp, xs, jnp.stack(p_cmp), jnp.stack(p_slc), jnp.stack(p_win), jnp.stack(p_wkv), jnp.stack(p_shift),
            jnp.stack(s_cmp), jnp.stack(s_slc), jnp.stack(s_win), jnp.stack(s_wkv), jnp.stack(s_shift))
```

```python
import functools
import jax, jax.numpy as jnp
from jax import lax
import numpy as np
from jax.experimental import pallas as pl
from jax.experimental.pallas import tpu as pltpu

D_MODEL = 1024
DEPTH = 4
PAGE_SIZE = 128
HEAD_DIM = 64
N_HEADS_ATTN = 8
N_KV_HEADS = 2
HEADS_PER_KV = N_HEADS_ATTN // N_KV_HEADS
D_ATTN = N_HEADS_ATTN * HEAD_DIM
D_KV = N_KV_HEADS * HEAD_DIM
N_BRANCH = 3
N_HEADS_RWKV = 8
D_RWKV = N_HEADS_RWKV * HEAD_DIM
ROT_DIM = HEAD_DIM // 4
ROPE_THETA = 500000.0
BLOCK_CMP = 32
STRIDE_CMP = 16
BLOCK_SLC = 64
N_SELECT = 16
N_LOCAL = 2
WINDOW = 512
QBLOCK = 128
RANK_W = 32
RANK_A = 32
RANK_G = 64
D_FF = 4 * D_MODEL
SHIFT_COLS = 3 * D_RWKV + RANK_W + RANK_A + RANK_G
IN_COLS = D_ATTN + 6 * D_KV + N_BRANCH * N_HEADS_ATTN + SHIFT_COLS
IN_COLS_PAD = 3072
NORM_EPS = 1e-6
GN_EPS = 64e-5
NEG = -1e30
FORCED_SCORE = 1e9

VMEM_LIMIT = 48 * 1024 * 1024
bf16 = jnp.bfloat16
f32 = jnp.float32


def _row_tile(m):
    return min(m, 1024)


def _norm_proj_kernel(x_ref, g_ref, w_ref, o_ref):
    x = x_ref[...]
    n = x * lax.rsqrt(jnp.mean(x * x, axis=-1, keepdims=True) + NORM_EPS) * g_ref[...]
    o_ref[...] = jnp.dot(n.astype(bf16), w_ref[...], preferred_element_type=f32)


def norm_proj(x, g, w):
    m, n = x.shape[0], w.shape[1]
    tm, tn = _row_tile(m), 512
    return pl.pallas_call(
        _norm_proj_kernel,
        grid=(m // tm, n // tn),
        in_specs=[pl.BlockSpec((tm, D_MODEL), lambda i, j: (i, 0)),
                  pl.BlockSpec((1, D_MODEL), lambda i, j: (0, 0)),
                  pl.BlockSpec((D_MODEL, tn), lambda i, j: (0, j))],
        out_specs=pl.BlockSpec((tm, tn), lambda i, j: (i, j)),
        out_shape=jax.ShapeDtypeStruct((m, n), f32),
        compiler_params=pltpu.CompilerParams(
            dimension_semantics=("parallel", "arbitrary"), vmem_limit_bytes=VMEM_LIMIT),
        name="norm_proj",
    )(x, g.reshape(1, D_MODEL), w)


def _out_proj_kernel(x_ref, a_ref, r_ref, w_ref, o_ref):
    o_ref[...] = (x_ref[...]
                  + jnp.dot(a_ref[...].astype(bf16), w_ref[:D_ATTN, :], preferred_element_type=f32)
                  + jnp.dot(r_ref[...].astype(bf16), w_ref[D_ATTN:, :], preferred_element_type=f32))


def out_proj(x, o_attn, o_rwkv, w):
    m = x.shape[0]
    tm = _row_tile(m)
    return pl.pallas_call(
        _out_proj_kernel,
        grid=(m // tm,),
        in_specs=[pl.BlockSpec((tm, D_MODEL), lambda i: (i, 0)),
                  pl.BlockSpec((tm, D_ATTN), lambda i: (i, 0)),
                  pl.BlockSpec((tm, D_RWKV), lambda i: (i, 0)),
                  pl.BlockSpec((D_MODEL, D_MODEL), lambda i: (0, 0))],
        out_specs=pl.BlockSpec((tm, D_MODEL), lambda i: (i, 0)),
        out_shape=jax.ShapeDtypeStruct((m, D_MODEL), f32),
        compiler_params=pltpu.CompilerParams(
            dimension_semantics=("parallel",), vmem_limit_bytes=VMEM_LIMIT),
        name="out_proj",
    )(x, o_attn, o_rwkv, w)


def _ffn_kernel(h_ref, g_ref, wu_ref, wd_ref, o_ref, n_ref):
    j = pl.program_id(1)

    @pl.when(j == 0)
    def _():
        h = h_ref[...]
        n = h * lax.rsqrt(jnp.mean(h * h, axis=-1, keepdims=True) + NORM_EPS) * g_ref[...]
        n_ref[...] = n.astype(bf16)
        o_ref[...] = h

    u = jnp.dot(n_ref[...], wu_ref[...], preferred_element_type=f32)
    u = jnp.square(jnp.maximum(u, 0.0))
    o_ref[...] += jnp.dot(u.astype(bf16), wd_ref[...], preferred_element_type=f32)


def ffn(h, g, w_up, w_down):
    m = h.shape[0]
    tm, tf = _row_tile(m), 512
    return pl.pallas_call(
        _ffn_kernel,
        grid=(m // tm, D_FF // tf),
        in_specs=[pl.BlockSpec((tm, D_MODEL), lambda i, j: (i, 0)),
                  pl.BlockSpec((1, D_MODEL), lambda i, j: (0, 0)),
                  pl.BlockSpec((D_MODEL, tf), lambda i, j: (0, j)),
                  pl.BlockSpec((tf, D_MODEL), lambda i, j: (j, 0))],
        out_specs=pl.BlockSpec((tm, D_MODEL), lambda i, j: (i, 0)),
        out_shape=jax.ShapeDtypeStruct((m, D_MODEL), f32),
        scratch_shapes=[pltpu.VMEM((tm, D_MODEL), bf16)],
        compiler_params=pltpu.CompilerParams(
            dimension_semantics=("parallel", "arbitrary"), vmem_limit_bytes=VMEM_LIMIT),
        name="ffn",
    )(h, g.reshape(1, D_MODEL), w_up, w_down)


LANES = 128
KV_COLS = 2 * D_KV
N_GATES = N_BRANCH * N_HEADS_ATTN
N_LOWRANK = RANK_W + RANK_A + RANK_G
Z_Q, Z_R, Z_K, Z_V = 0, D_ATTN, D_ATTN + D_RWKV, D_ATTN + 2 * D_RWKV
Z_KV = D_ATTN + 3 * D_RWKV
Z_GATE = Z_KV + N_BRANCH * KV_COLS
Z_LR = Z_GATE + LANES
assert N_LOWRANK == LANES and Z_LR + N_LOWRANK == IN_COLS_PAD and N_GATES <= LANES


def permute_w_in(w_in):
    o_kv, o_gate, o_rwkv = D_ATTN, D_ATTN + 6 * D_KV, D_ATTN + 6 * D_KV + N_GATES
    pad = jnp.zeros(w_in.shape[:-1] + (LANES - N_GATES,), w_in.dtype)
    return jnp.concatenate([w_in[..., :o_kv], w_in[..., o_rwkv:o_rwkv + 3 * D_RWKV], w_in[..., o_kv:o_gate],
                            w_in[..., o_gate:o_rwkv], pad, w_in[..., o_rwkv + 3 * D_RWKV:]], axis=-1)


def split_z(z):
    z_attn = jnp.concatenate([z[..., :D_ATTN], z[..., Z_KV:Z_GATE]], axis=-1)
    z_rwkv = jnp.concatenate([z[..., Z_R:Z_KV], z[..., Z_LR:]], axis=-1)
    return z_attn, z[..., Z_GATE:Z_GATE + N_GATES], z_rwkv


def _seg_sumsq(x, p_ref):
    x2 = x * x
    hi = x2.astype(bf16)
    lo = (x2 - hi.astype(f32)).astype(bf16)
    return (jnp.dot(hi, p_ref[...], preferred_element_type=f32)
            + jnp.dot(lo, p_ref[...], preferred_element_type=f32))


def _norm_rope(x, g, p_ref, c, s1, s2):
    y = x * lax.rsqrt(_seg_sumsq(x, p_ref) * (1.0 / HEAD_DIM) + NORM_EPS) * g
    return y * c + pltpu.roll(y, LANES - ROT_DIM // 2, 1) * s1 + pltpu.roll(y, ROT_DIM // 2, 1) * s2


def _attn_prep_kernel(zq_ref, zc_ref, zs_ref, zw_ref, zg_ref, c_ref, s1_ref, s2_ref, p_ref, qg_ref, kg_ref, gb_ref,
                      q_ref, kvc_ref, kvs_ref, kvw_ref, kb_ref, vvb_ref, gate_ref):
    c, s1, s2 = c_ref[...], s1_ref[...], s2_ref[...]
    lane = lax.broadcasted_iota(jnp.int32, (zq_ref.shape[0], LANES), 1)
    for pair in range(N_HEADS_ATTN // 2):
        y = _norm_rope(zq_ref[:, pair * LANES:(pair + 1) * LANES], qg_ref[...], p_ref, c, s1, s2)
        y = y * (HEAD_DIM ** -0.5)
        y_sw = pltpu.roll(y, HEAD_DIM, 1)
        for half in range(2):
            h = 2 * pair + half
            grp = h // HEADS_PER_KV
            src = y if half == grp else y_sw
            keep = (lane >= HEAD_DIM) if grp == 1 else (lane < HEAD_DIM)
            q_ref[0, h] = jnp.where(keep, src, 0.0).astype(bf16)
    ins = (zc_ref, zs_ref, zw_ref)
    outs = (kvc_ref, kvs_ref, kvw_ref)
    for br in range(N_BRANCH):
        k = _norm_rope(ins[br][:, :D_KV], kg_ref[br:br + 1, :], p_ref, c, s1, s2)
        v = ins[br][:, D_KV:]
        outs[br][:, :D_KV] = k
        outs[br][:, D_KV:] = v
        if br > 0:
            kb_ref[br - 1] = k.astype(bf16)
            vvb_ref[br - 1, :, :D_KV] = v.astype(bf16)
            vvb_ref[br - 1, :, D_KV:] = pltpu.roll(v, HEAD_DIM, 1).astype(bf16)
    gate_ref[...] = jax.nn.sigmoid(zg_ref[...] + gb_ref[...])


def rope_lane_tables(pos):
    half = ROT_DIM // 2
    inv = ROPE_THETA ** (-(jnp.arange(0, ROT_DIM, 2, dtype=f32) / ROT_DIM))
    ang = pos.astype(f32)[:, None] * inv[None, :]
    cos, sin = jnp.cos(ang), jnp.sin(ang)
    t = pos.shape[0]
    one, zero = jnp.ones((t, HEAD_DIM - ROT_DIM), f32), jnp.zeros((t, HEAD_DIM - ROT_DIM), f32)
    zh = jnp.zeros((t, half), f32)
    c = jnp.concatenate([cos, cos, one], axis=1)
    s1 = jnp.concatenate([-sin, zh, zero], axis=1)
    s2 = jnp.concatenate([zh, sin, zero], axis=1)
    return tuple(jnp.tile(a, (1, 2)) for a in (c, s1, s2))


def attn_prep(z, rope_tabs, seg_ones, q_g, k_g, gate_b, batch, seq):
    m = batch * seq
    tm = min(512, seq)
    tiles = seq // tm
    c, s1, s2 = rope_tabs
    row = lambda i: (i, 0)
    tab = lambda i: (i % tiles, 0)
    const = lambda i: (0, 0)
    kv_shape = jax.ShapeDtypeStruct((m, KV_COLS), f32)
    return pl.pallas_call(
        _attn_prep_kernel,
        grid=(m // tm,),
        in_specs=[pl.BlockSpec((tm, D_ATTN), row),
                  pl.BlockSpec((tm, KV_COLS), lambda i: (i, Z_KV // KV_COLS)),
                  pl.BlockSpec((tm, KV_COLS), lambda i: (i, Z_KV // KV_COLS + 1)),
                  pl.BlockSpec((tm, KV_COLS), lambda i: (i, Z_KV // KV_COLS + 2)),
                  pl.BlockSpec((tm, LANES), lambda i: (i, Z_GATE // LANES)),
                  pl.BlockSpec((tm, LANES), tab), pl.BlockSpec((tm, LANES), tab), pl.BlockSpec((tm, LANES), tab),
                  pl.BlockSpec((LANES, LANES), const),
                  pl.BlockSpec((1, LANES), const), pl.BlockSpec((N_BRANCH, LANES), const),
                  pl.BlockSpec((1, LANES), const)],
        out_specs=[pl.BlockSpec((1, N_HEADS_ATTN, tm, LANES), lambda i: (i // tiles, 0, i % tiles, 0)),
                   pl.BlockSpec((tm, KV_COLS), row), pl.BlockSpec((tm, KV_COLS), row),
                   pl.BlockSpec((tm, KV_COLS), row),
                   pl.BlockSpec((2, tm, D_KV), lambda i: (0, i, 0)),
                   pl.BlockSpec((2, tm, KV_COLS), lambda i: (0, i, 0)),
                   pl.BlockSpec((tm, LANES), row)],
        out_shape=[jax.ShapeDtypeStruct((batch, N_HEADS_ATTN, seq, LANES), bf16),
                   kv_shape, kv_shape, kv_shape,
                   jax.ShapeDtypeStruct((2, m, D_KV), bf16),
                   jax.ShapeDtypeStruct((2, m, KV_COLS), bf16),
                   jax.ShapeDtypeStruct((m, LANES), f32)],
        compiler_params=pltpu.CompilerParams(
            dimension_semantics=("parallel",), vmem_limit_bytes=VMEM_LIMIT),
        name="attn_prep",
    )(z, z, z, z, z, c, s1, s2, seg_ones, q_g, k_g, gate_b)


CHUNK_COLS = STRIDE_CMP * KV_COLS
CMP_R = BLOCK_CMP // STRIDE_CMP


def compress_weights(cmp_pe, cmp_w):
    pe = cmp_pe.reshape(2, CMP_R, STRIDE_CMP, HEAD_DIM)
    w = cmp_w.reshape(2, CMP_R, STRIDE_CMP, HEAD_DIM, HEAD_DIM)
    eye_c, eye_g = jnp.eye(2, dtype=f32), jnp.eye(N_KV_HEADS, dtype=f32)
    wb = jnp.einsum('cisde,cx,gy->iscgdxye', w, eye_c, eye_g).reshape(CMP_R, CHUNK_COLS, KV_COLS)
    peb = jnp.broadcast_to(jnp.transpose(pe, (1, 2, 0, 3))[:, :, :, None, :],
                           (CMP_R, STRIDE_CMP, 2, N_KV_HEADS, HEAD_DIM)).reshape(CMP_R, 1, CHUNK_COLS)
    return wb.astype(bf16), peb


def _compress_kernel(x_ref, pe_ref, w_ref, kc_ref, vvc_ref):
    x = x_ref[0]
    n = x.shape[0]
    y0 = jnp.dot((x + pe_ref[0]).astype(bf16), w_ref[0], preferred_element_type=f32)
    y1 = jnp.dot((x + pe_ref[1]).astype(bf16), w_ref[1], preferred_element_type=f32)
    comp = y0 + pltpu.roll(y1, n - 1, 0)
    row = lax.broadcasted_iota(jnp.int32, comp.shape, 0)
    comp = jnp.where(row < n - 1, comp, 0.0)
    kc_ref[0] = comp[:, :D_KV].astype(bf16)
    vc = comp[:, D_KV:]
    vvc_ref[0, :, :D_KV] = vc.astype(bf16)
    vvc_ref[0, :, D_KV:] = pltpu.roll(vc, HEAD_DIM, 1).astype(bf16)


def compress(kv, wb, peb, batch):
    n_chunks = kv.shape[0] // batch // STRIDE_CMP
    x = kv.reshape(batch, n_chunks, CHUNK_COLS)
    return pl.pallas_call(
        _compress_kernel,
        grid=(batch,),
        in_specs=[pl.BlockSpec((1, n_chunks, CHUNK_COLS), lambda b: (b, 0, 0)),
                  pl.BlockSpec((CMP_R, 1, CHUNK_COLS), lambda b: (0, 0, 0)),
                  pl.BlockSpec((CMP_R, CHUNK_COLS, KV_COLS), lambda b: (0, 0, 0))],
        out_specs=[pl.BlockSpec((1, n_chunks, D_KV), lambda b: (b, 0, 0)),
                   pl.BlockSpec((1, n_chunks, KV_COLS), lambda b: (b, 0, 0))],
        out_shape=[jax.ShapeDtypeStruct((batch, n_chunks, D_KV), bf16),
                   jax.ShapeDtypeStruct((batch, n_chunks, KV_COLS), bf16)],
        compiler_params=pltpu.CompilerParams(
            dimension_semantics=("parallel",), vmem_limit_bytes=VMEM_LIMIT),
        name="compress",
    )(x, peb, wb)


KEY_CHUNK = 512
NT_DIMS = (((1,), (1,)), ((), ()))


def _softmax_rows(s3, mask):
    s3 = jnp.where(mask[None], s3, NEG)
    m = jnp.max(s3, axis=-1, keepdims=True)
    p = jnp.where(mask[None], jnp.exp(s3 - m), 0.0)
    l = jnp.sum(p, axis=-1, keepdims=True)
    return p * jnp.where(l > 0.0, 1.0 / l, 0.0)


def _nsa_prompt_kernel(q_ref, gate_ref, kc_ref, vvc_ref, ks_ref, vvs_ref, kw_ref, vvw_ref, smap_ref, e_ref,
                       o_ref, mask_ref, m_ref, l_ref, acc_ref, *, seq):
    n_slc = seq // BLOCK_SLC
    n_cmp = seq // STRIDE_CMP
    hq = HEADS_PER_KV * QBLOCK
    q0 = pl.program_id(1) * QBLOCK
    qpos_col = q0 + lax.broadcasted_iota(jnp.int32, (QBLOCK, 1), 0)

    n_key_chunks = seq // KEY_CHUNK
    qs, o_cs = [], []
    for g in range(N_KV_HEADS):
        q = q_ref[0, g * HEADS_PER_KV:(g + 1) * HEADS_PER_KV].reshape(hq, LANES)
        qs.append(q)

        s = lax.dot_general(q, kc_ref[0], NT_DIMS, preferred_element_type=f32)
        c_end = lax.broadcasted_iota(jnp.int32, (1, n_cmp), 1) * STRIDE_CMP + (BLOCK_CMP - 1)
        p_c = _softmax_rows(s.reshape(HEADS_PER_KV, QBLOCK, n_cmp), c_end <= qpos_col)
        p_c = p_c.astype(bf16).reshape(hq, n_cmp)
        o_cs.append(jnp.dot(p_c, vvc_ref[0], preferred_element_type=f32))

        imp = jnp.zeros((n_slc, QBLOCK), f32)
        for h in range(HEADS_PER_KV):
            imp = imp + lax.dot_general(smap_ref[...], p_c[h * QBLOCK:(h + 1) * QBLOCK], NT_DIMS,
                                        preferred_element_type=f32)
        jj = lax.broadcasted_iota(jnp.int32, (n_slc, QBLOCK), 0)
        qp = q0 + lax.broadcasted_iota(jnp.int32, (n_slc, QBLOCK), 1)
        cur = qp // BLOCK_SLC
        forced = (jj == 0) | ((jj <= cur) & (jj > cur - N_LOCAL))
        imp = jnp.where(jj * BLOCK_SLC <= qp, jnp.where(forced, FORCED_SCORE, imp), NEG)
        rank = jnp.zeros((n_slc, QBLOCK), f32)
        for j in range(n_slc):
            row = imp[j:j + 1, :]
            rank = rank + jnp.where((row > imp) | ((row == imp) & (jj > j)), 1.0, 0.0)
        sel_t = jnp.where(rank < float(min(N_SELECT, n_slc)), 1.0, 0.0)
        if n_slc < LANES:
            sel_t = jnp.concatenate([sel_t, jnp.zeros((LANES - n_slc, QBLOCK), f32)], axis=0)
        sel = sel_t.T.astype(bf16)
        for c in range(n_key_chunks):
            mask_ref[g * n_key_chunks + c] = jnp.dot(sel, e_ref[:, c * KEY_CHUNK:(c + 1) * KEY_CHUNK],
                                                     preferred_element_type=f32)

    m_ref[...] = jnp.full(m_ref.shape, NEG, f32)
    l_ref[...] = jnp.zeros(l_ref.shape, f32)
    acc_ref[...] = jnp.zeros(acc_ref.shape, f32)

    def sel_step(c, carry):
        k0 = pl.multiple_of(c * KEY_CHUNK, KEY_CHUNK)
        k_chunk = ks_ref[0, pl.ds(k0, KEY_CHUNK), :]
        vv_chunk = vvs_ref[0, pl.ds(k0, KEY_CHUNK), :]
        causal = (k0 + lax.broadcasted_iota(jnp.int32, (1, KEY_CHUNK), 1)) <= qpos_col
        for g in range(N_KV_HEADS):
            hs = slice(g * HEADS_PER_KV, (g + 1) * HEADS_PER_KV)
            sc = lax.dot_general(qs[g], k_chunk, NT_DIMS, preferred_element_type=f32)
            msk = ((mask_ref[g * n_key_chunks + c] > 0.5) & causal)[None]
            s3 = jnp.where(msk, sc.reshape(HEADS_PER_KV, QBLOCK, KEY_CHUNK), NEG)
            m_old = m_ref[hs]
            m_new = jnp.maximum(m_old, jnp.max(s3, axis=-1, keepdims=True))
            alpha = jnp.exp(m_old - m_new)
            p = jnp.where(msk, jnp.exp(s3 - m_new), 0.0)
            l_ref[hs] = alpha * l_ref[hs] + jnp.sum(p, axis=-1, keepdims=True)
            pv = jnp.dot(p.astype(bf16).reshape(hq, KEY_CHUNK), vv_chunk, preferred_element_type=f32)
            acc_ref[hs] = alpha * acc_ref[hs] + pv.reshape(HEADS_PER_KV, QBLOCK, KV_COLS)
            m_ref[hs] = m_new
        return carry

    lax.fori_loop(0, q0 // KEY_CHUNK + 1, sel_step, 0)

    heads_out = []
    for g in range(N_KV_HEADS):
        q, o_c = qs[g], o_cs[g]
        hs = slice(g * HEADS_PER_KV, (g + 1) * HEADS_PER_KV)
        o_s = (acc_ref[hs] / l_ref[hs]).reshape(hq, KV_COLS)

        n_win = WINDOW + QBLOCK
        w0 = pl.multiple_of(jnp.maximum(q0 - WINDOW, 0), QBLOCK)
        sw = lax.dot_general(q, kw_ref[0, pl.ds(w0, n_win), :], NT_DIMS, preferred_element_type=f32)
        dist = qpos_col - (w0 + lax.broadcasted_iota(jnp.int32, (1, n_win), 1))
        p_w = _softmax_rows(sw.reshape(HEADS_PER_KV, QBLOCK, n_win), (dist >= 0) & (dist < WINDOW))
        o_w = jnp.dot(p_w.astype(bf16).reshape(hq, n_win), vvw_ref[0, pl.ds(w0, n_win), :],
                      preferred_element_type=f32)

        for hh in range(HEADS_PER_KV):
            h = g * HEADS_PER_KV + hh
            half = 0 if (h % 2) == g else 1
            rows = slice(hh * QBLOCK, (hh + 1) * QBLOCK)
            cols = slice(half * LANES, (half + 1) * LANES)
            gc = gate_ref[:, N_BRANCH * h:N_BRANCH * h + 1]
            gs = gate_ref[:, N_BRANCH * h + 1:N_BRANCH * h + 2]
            gw = gate_ref[:, N_BRANCH * h + 2:N_BRANCH * h + 3]
            heads_out.append(gc * o_c[rows, cols] + gs * o_s[rows, cols] + gw * o_w[rows, cols])

    lane = lax.broadcasted_iota(jnp.int32, (QBLOCK, LANES), 1)
    for pair in range(N_HEADS_ATTN // 2):
        o_ref[0, :, pair * LANES:(pair + 1) * LANES] = jnp.where(
            lane < HEAD_DIM, heads_out[2 * pair], heads_out[2 * pair + 1])


def nsa_tables(seq):
    n_slc, n_cmp = seq // BLOCK_SLC, seq // STRIDE_CMP
    start = np.arange(n_cmp) * STRIDE_CMP
    first, last = start // BLOCK_SLC, (start + BLOCK_CMP - 1) // BLOCK_SLC
    j = np.arange(n_slc)
    smap = (first[None, :] <= j[:, None]) & (last[None, :] >= j[:, None]) & (np.arange(n_cmp)[None, :] < n_cmp - 1)
    expand = (np.arange(seq)[None, :] // BLOCK_SLC) == np.arange(max(n_slc, LANES))[:, None]
    return jnp.asarray(smap, bf16), jnp.asarray(expand, bf16)


def nsa_prompt(q_ext, gates, kc, vvc, kb, vvb, tables, batch, seq):
    smap, expand = tables
    n_cmp = seq // STRIDE_CMP
    hq = HEADS_PER_KV * QBLOCK
    per_b = lambda b, i: (b, 0, 0)
    const = lambda b, i: (0, 0)
    gates3 = gates.reshape(batch, seq, LANES)
    kb4 = kb.reshape(2, batch, seq, D_KV)
    vvb4 = vvb.reshape(2, batch, seq, KV_COLS)
    branch = lambda br: (lambda b, i: (br, b, 0, 0))
    return pl.pallas_call(
        functools.partial(_nsa_prompt_kernel, seq=seq),
        grid=(batch, seq // QBLOCK),
        in_specs=[pl.BlockSpec((1, N_HEADS_ATTN, QBLOCK, LANES), lambda b, i: (b, 0, i, 0)),
                  pl.BlockSpec((None, QBLOCK, LANES), lambda b, i: (b, i, 0)),
                  pl.BlockSpec((1, n_cmp, D_KV), per_b), pl.BlockSpec((1, n_cmp, KV_COLS), per_b),
                  pl.BlockSpec((None, 1, seq, D_KV), branch(0)), pl.BlockSpec((None, 1, seq, KV_COLS), branch(0)),
                  pl.BlockSpec((None, 1, seq, D_KV), branch(1)), pl.BlockSpec((None, 1, seq, KV_COLS), branch(1)),
                  pl.BlockSpec(smap.shape, const), pl.BlockSpec(expand.shape, const)],
        out_specs=pl.BlockSpec((1, QBLOCK, D_ATTN), lambda b, i: (b, i, 0)),
        out_shape=jax.ShapeDtypeStruct((batch, seq, D_ATTN), f32),
        scratch_shapes=[pltpu.VMEM((N_KV_HEADS * (seq // KEY_CHUNK), QBLOCK, KEY_CHUNK), f32),
                        pltpu.VMEM((N_HEADS_ATTN, QBLOCK, 1), f32),
                        pltpu.VMEM((N_HEADS_ATTN, QBLOCK, 1), f32),
                        pltpu.VMEM((N_HEADS_ATTN, QBLOCK, KV_COLS), f32)],
        compiler_params=pltpu.CompilerParams(
            dimension_semantics=("parallel", "arbitrary"), vmem_limit_bytes=VMEM_LIMIT),
        name="nsa_prompt",
    )(q_ext, gates3, kc, vvc, kb4, vvb4, kb4, vvb4, smap, expand)


RW_CHUNK = 64
N_PAIRS = N_HEADS_RWKV // 2
TN_DIMS = (((0,), (0,)), ((), ()))


def _mm(a, b):
    return jnp.dot(a.astype(bf16), b.astype(bf16), preferred_element_type=f32)


def _mm_nt(a, b):
    return lax.dot_general(a.astype(bf16), b.astype(bf16), NT_DIMS, preferred_element_type=f32)


def _mm_tn(a, b):
    return lax.dot_general(a.astype(bf16), b.astype(bf16), TN_DIMS, preferred_element_type=f32)


def _mm_split(a01, x):
    hi = x.astype(bf16)
    lo = (x - hi.astype(f32)).astype(bf16)
    return jnp.dot(a01, hi, preferred_element_type=f32) + jnp.dot(a01, lo, preferred_element_type=f32)


def _seg_sum(x, p_ref):
    hi = x.astype(bf16)
    lo = (x - hi.astype(f32)).astype(bf16)
    return (jnp.dot(hi, p_ref[...], preferred_element_type=f32)
            + jnp.dot(lo, p_ref[...], preferred_element_type=f32))


def _stack_heads(x):
    lane = lax.broadcasted_iota(jnp.int32, x.shape, 1)
    return jnp.concatenate([jnp.where(lane < HEAD_DIM, x, 0.0), jnp.where(lane >= HEAD_DIM, x, 0.0)], axis=0)


def _chunk_pair(r, cum, lw, alpha, beta, k, v, st):
    c = r.shape[0]
    eg, egi, egp = jnp.exp(cum), jnp.exp(-cum), jnp.exp(cum - lw)
    g_end = eg[c - 1:c, :]
    x_a, x_r = _stack_heads(alpha * egp), _stack_heads(r * eg)
    bt, kt = beta * egi, k * egi
    b_b, b_k = jnp.concatenate([bt, bt], axis=0), jnp.concatenate([kt, kt], axis=0)
    x_bg, x_kg, v_s = _stack_heads(bt * g_end), _stack_heads(kt * g_end), _stack_heads(v)
    row = lax.broadcasted_iota(jnp.int32, (2 * c, 2 * c), 0)
    col = lax.broadcasted_iota(jnp.int32, (2 * c, 2 * c), 1)
    same = (row // c) == (col // c)
    strict = same & (col < row)
    incl = same & (col <= row)
    n = jnp.where(strict, _mm_nt(x_a, b_b), 0.0)
    a_ak = jnp.where(strict, _mm_nt(x_a, b_k), 0.0)
    a_rb = jnp.where(incl, _mm_nt(x_r, b_b), 0.0)
    a_rk = jnp.where(incl, _mm_nt(x_r, b_k), 0.0)
    def level(k):
        return jnp.where(((row >> (k + 1)) == (col >> (k + 1))) & ((row >> k) != (col >> k)), n, 0.0)
    inv = jnp.where(row == col, 1.0, 0.0) + level(0)
    for k in range(1, int(np.log2(c))):
        inv = inv + _mm(_mm(inv, level(k)), inv)
    st_b = st.astype(bf16)
    p = _mm(x_a, st_b) + _mm(a_ak, v_s)
    e = _mm(inv, p)
    y_s = _mm(x_r, st_b) + _mm(a_rb, e) + _mm(a_rk, v_s)
    g_col = jnp.broadcast_to(g_end, (2 * c, 2 * c)).T
    st_new = st * g_col + _mm_tn(x_bg, e) + _mm_tn(x_kg, v_s)
    return y_s[:c] + y_s[c:], st_new


def _rwkv_prompt_kernel(*refs, first_layer):
    zr_ref, zk_ref, zv_ref, zl_ref, sp_ref, mu_ref, vec_ref, lr_ref, p_ref, tri_ref = refs[:10]
    if first_layer:
        o_ref, st_ref, vfo_ref, prev_ref = refs[10:]
    else:
        vf_ref, o_ref, st_ref, prev_ref = refs[10:]
    c = RW_CHUNK
    ci = pl.program_id(1)

    @pl.when(ci == 0)
    def _():
        prev_ref[0:1, :] = sp_ref[...]
        st_ref[...] = jnp.zeros(st_ref.shape, f32)

    def shifted(z_ref, lo, hi):
        z = z_ref[...]
        first = lax.broadcasted_iota(jnp.int32, z.shape, 0) == 0
        prev = jnp.where(first, prev_ref[0:1, lo:hi], pltpu.roll(z, 1, 0))
        xs = z + (prev - z) * mu_ref[:, lo:hi]
        prev_ref[0:1, lo:hi] = z[c - 1:c, :]
        return xs

    xr = shifted(zr_ref, 0, D_RWKV)
    xk = shifted(zk_ref, D_RWKV, 2 * D_RWKV)
    xv = shifted(zv_ref, 2 * D_RWKV, 3 * D_RWKV)
    xl = shifted(zl_ref, 3 * D_RWKV, SHIFT_COLS)
    w0, a0, vres0, k_k = vec_ref[0:1, :], vec_ref[1:2, :], vec_ref[2:3, :], vec_ref[3:4, :]
    k_a, r_k, lnx_w, lnx_b = vec_ref[4:5, :], vec_ref[5:6, :], vec_ref[6:7, :], vec_ref[7:8, :]

    xl_b = xl.astype(bf16)
    u = -(w0 + jnp.dot(jnp.tanh(xl).astype(bf16), lr_ref[0], preferred_element_type=f32))
    softplus = jnp.maximum(u, 0.0) + jnp.log(1.0 + jnp.exp(-jnp.abs(u)))
    lw = -jnp.exp(-softplus - 0.5)
    a = jax.nn.sigmoid(a0 + jnp.dot(xl_b, lr_ref[1], preferred_element_type=f32))
    if first_layer:
        v = xv
        vfo_ref[...] = xv
    else:
        v = xv + (vf_ref[...] - xv) * jax.nn.sigmoid(vres0 + jnp.dot(xl_b, lr_ref[2], preferred_element_type=f32))
    gate = jnp.dot(jax.nn.sigmoid(xl).astype(bf16), lr_ref[3], preferred_element_type=f32)
    kk = xk * k_k
    k2 = xk * (1.0 + (a - 1.0) * k_a)
    cum = _mm_split(tri_ref[...], lw)

    for pair in range(N_PAIRS):
        cols = slice(pair * LANES, (pair + 1) * LANES)
        kk_p = kk[:, cols]
        kk_p = kk_p * lax.rsqrt(jnp.maximum(_seg_sum(kk_p * kk_p, p_ref), 1e-24))
        r_p, k_p, v_p = xr[:, cols], k2[:, cols], v[:, cols]
        y, st_new = _chunk_pair(r_p, cum[:, cols], lw[:, cols], -kk_p, kk_p * a[:, cols], k_p, v_p, st_ref[pair])
        st_ref[pair] = st_new
        mean = _seg_sum(y, p_ref) * (1.0 / HEAD_DIM)
        d = y - mean
        var = _seg_sum(d * d, p_ref) * (1.0 / HEAD_DIM)
        yn = d * lax.rsqrt(var + GN_EPS) * lnx_w[:, cols] + lnx_b[:, cols]
        bonus = _seg_sum(r_p * k_p * r_k[:, cols], p_ref) * v_p
        o_ref[:, cols] = (yn + bonus) * gate[:, cols]


def rwkv_prompt(z, v_first, lp, consts, batch, seq):
    m = batch * seq
    c = RW_CHUNK
    n_chunks = seq // c
    first_layer = v_first is None
    rows = lambda j: (lambda b, i: (b * n_chunks + i, j))
    const2 = lambda b, i: (0, 0)
    tok = pl.BlockSpec((c, D_RWKV), rows(0))
    tok_shape = jax.ShapeDtypeStruct((m, D_RWKV), f32)
    out = pl.pallas_call(
        functools.partial(_rwkv_prompt_kernel, first_layer=first_layer),
        grid=(batch, n_chunks),
        in_specs=[pl.BlockSpec((c, D_RWKV), rows(Z_R // D_RWKV)),
                  pl.BlockSpec((c, D_RWKV), rows(Z_K // D_RWKV)),
                  pl.BlockSpec((c, D_RWKV), rows(Z_V // D_RWKV)),
                  pl.BlockSpec((c, LANES), rows(Z_LR // LANES)),
                  pl.BlockSpec((None, 1, SHIFT_COLS), lambda b, i: (b, 0, 0)),
                  pl.BlockSpec((1, SHIFT_COLS), const2),
                  pl.BlockSpec((8, D_RWKV), const2),
                  pl.BlockSpec((4, LANES, D_RWKV), lambda b, i: (0, 0, 0)),
                  pl.BlockSpec((LANES, LANES), const2),
                  pl.BlockSpec((c, c), const2)] + ([] if first_layer else [tok]),
        out_specs=[tok, pl.BlockSpec((None, N_PAIRS, LANES, LANES), lambda b, i: (b, 0, 0, 0))]
                  + ([tok] if first_layer else []),
        out_shape=[tok_shape, jax.ShapeDtypeStruct((batch, N_PAIRS, LANES, LANES), f32)]
                  + ([tok_shape] if first_layer else []),
        scratch_shapes=[pltpu.VMEM((8, SHIFT_COLS), f32)],
        compiler_params=pltpu.CompilerParams(
            dimension_semantics=("parallel", "arbitrary"), vmem_limit_bytes=VMEM_LIMIT),
        name="rwkv_prompt",
    )(z, z, z, z, consts['shift_zero'], lp['mu_row'], lp['rwkv_vecs'], lp['rwkv_lowrank'],
      consts['seg_ones'], consts['tri'], *([] if first_layer else [v_first]))
    return out[0], out[1], (out[2] if first_layer else v_first)


def state_from_pair_tiles(st):
    b = st.shape[0]
    t = st.reshape(b, N_PAIRS, 2, HEAD_DIM, 2, HEAD_DIM)
    diag = jnp.stack([t[:, :, 0, :, 0, :], t[:, :, 1, :, 1, :]], axis=2)
    return jnp.swapaxes(diag, -1, -2).reshape(b, N_HEADS_RWKV, HEAD_DIM, HEAD_DIM)


def rwkv_params(lp_raw, layer):
    zrow = jnp.zeros((D_RWKV,), f32)
    vres0 = lp_raw['vres0'][layer - 1] if layer > 0 else zrow
    vecs = jnp.stack([lp_raw['w0'][layer], lp_raw['a0'][layer], vres0, lp_raw['k_k'][layer], lp_raw['k_a'][layer],
                      lp_raw['r_k'][layer].reshape(D_RWKV), lp_raw['lnx_w'][layer], lp_raw['lnx_b'][layer]])
    def rows(w, lo):
        return jnp.zeros((LANES, D_RWKV), f32).at[lo:lo + w.shape[0]].set(w)
    vres2 = lp_raw['vres2'][layer - 1] if layer > 0 else jnp.zeros((RANK_A, D_RWKV), f32)
    lowrank = jnp.stack([rows(lp_raw['w2'][layer], 0), rows(lp_raw['a2'][layer], RANK_W),
                         rows(vres2, RANK_W), rows(lp_raw['g2'][layer], RANK_W + RANK_A)]).astype(bf16)
    return vecs, lowrank


CHUNKS_PER_PAGE = PAGE_SIZE // STRIDE_CMP
PAGES_PER_STEP = 8
BLOCKS_PER_STEP = PAGES_PER_STEP * PAGE_SIZE // BLOCK_SLC
N_BLK_PAD = 256


POOL_PAGES_PER_STEP = 64


def _compress_pool_kernel(x_ref, pe_ref, w_ref, y_ref, xk_ref, xv_ref):
    xs_refs = (xk_ref, xv_ref)
    n_pages = x_ref.shape[0]
    n_chunks = n_pages * CHUNKS_PER_PAGE

    def to_token_major(p, carry):
        r0 = pl.multiple_of(p * PAGE_SIZE, PAGE_SIZE)
        page = x_ref[p]
        for c in range(2):
            tile = jnp.concatenate([page[c, g] for g in range(N_KV_HEADS)], axis=0)
            xs_refs[c][pl.ds(r0, PAGE_SIZE), :] = tile.T
        return carry

    lax.fori_loop(0, n_pages, to_token_major, 0)
    for c in range(2):
        acc = [jnp.zeros((n_chunks, D_KV), f32) for _ in range(CMP_R)]
        for s in range(STRIDE_CMP):
            xs = xs_refs[c][pl.ds(s, n_chunks, stride=STRIDE_CMP), :]
            rows = slice(s * KV_COLS + c * D_KV, s * KV_COLS + (c + 1) * D_KV)
            for i in range(CMP_R):
                acc[i] = acc[i] + jnp.dot((xs + pe_ref[i, :, rows]).astype(bf16),
                                          w_ref[i, rows, c * D_KV:(c + 1) * D_KV], preferred_element_type=f32)
        for i in range(CMP_R):
            y_ref[:, i * KV_COLS + c * D_KV:i * KV_COLS + (c + 1) * D_KV] = acc[i]


def compress_pool(cache_cmp, wb, peb):
    depth, n_pool = cache_cmp.shape[:2]
    rows = n_pool * CHUNKS_PER_PAGE
    pp = POOL_PAGES_PER_STEP
    tm = pp * CHUNKS_PER_PAGE
    assert n_pool % pp == 0
    return pl.pallas_call(
        _compress_pool_kernel,
        grid=(depth, n_pool // pp),
        in_specs=[pl.BlockSpec((None, pp, 2, N_KV_HEADS, HEAD_DIM, PAGE_SIZE), lambda l, i: (l, i, 0, 0, 0, 0)),
                  pl.BlockSpec((None, CMP_R, 1, CHUNK_COLS), lambda l, i: (l, 0, 0, 0)),
                  pl.BlockSpec((None, CMP_R, CHUNK_COLS, KV_COLS), lambda l, i: (l, 0, 0, 0))],
        out_specs=pl.BlockSpec((None, tm, 2 * KV_COLS), lambda l, i: (l, i, 0)),
        out_shape=jax.ShapeDtypeStruct((depth, rows, 2 * KV_COLS), f32),
        scratch_shapes=[pltpu.VMEM((pp * PAGE_SIZE, D_KV), f32), pltpu.VMEM((pp * PAGE_SIZE, D_KV), f32)],
        compiler_params=pltpu.CompilerParams(
            dimension_semantics=("parallel", "parallel"), vmem_limit_bytes=VMEM_LIMIT),
        name="compress_pool",
    )(cache_tiles(cache_cmp), peb, wb)


def _softmax_2d(s, mask):
    s = jnp.where(mask, s, NEG)
    m = jnp.max(s, axis=-1, keepdims=True)
    p = jnp.where(mask, jnp.exp(s - m), 0.0)
    l = jnp.sum(p, axis=-1, keepdims=True)
    return p * jnp.where(l > 0.0, 1.0 / l, 0.0)


def _pad_rows(x, rows):
    return jnp.concatenate([x, jnp.zeros((rows - x.shape[0], x.shape[1]), x.dtype)], axis=0)


def head_dim_major(tile_ref, which):
    return jnp.concatenate([tile_ref[which, g] for g in range(N_KV_HEADS)], axis=0).astype(bf16)


def cache_tiles(cache):
    nd = cache.ndim
    return jnp.transpose(cache, tuple(range(nd - 4)) + (nd - 3, nd - 2, nd - 1, nd - 4))


def _sample_attn_kernel(pt_ref, q_ref, gate_ref, y_ref, *rest, past, n_new):
    pages = rest[:PAGES_PER_STEP]
    (ksn_ref, kwn_ref, cw_ref, smap_ref, o_ref,
     oc_ref, ow_ref, sel_ref, imp_ref, m_ref, l_ref, acc_ref) = rest[PAGES_PER_STEP:]
    del pt_ref
    c = pl.program_id(1)
    n_steps = pl.num_programs(1)
    rows = N_HEADS_ATTN * n_new
    q = q_ref[...]
    qpos = past + lax.broadcasted_iota(jnp.int32, (rows, 1), 0) % n_new
    n_blk = -(-(past + n_new) // BLOCK_SLC)

    @pl.when(c == 0)
    def _():
        y = y_ref[...]
        n_chunks = y.shape[0]
        comp = y[:, :KV_COLS] + pltpu.roll(y[:, KV_COLS:], n_chunks - 1, 0)
        s = lax.dot_general(q, comp[:, :D_KV].astype(bf16), NT_DIMS, preferred_element_type=f32)
        c_end = lax.broadcasted_iota(jnp.int32, (1, n_chunks), 1) * STRIDE_CMP + (BLOCK_CMP - 1)
        valid = (c_end <= qpos) & (lax.broadcasted_iota(jnp.int32, (1, n_chunks), 1) < n_chunks - 1)
        p_c = _softmax_2d(s, valid)
        oc_ref[...] = jnp.dot(p_c.astype(bf16), comp[:, D_KV:].astype(bf16), preferred_element_type=f32)

        hq = HEADS_PER_KV * n_new
        p_sum = [sum(p_c[g * hq + hh * n_new:g * hq + (hh + 1) * n_new] for hh in range(HEADS_PER_KV))
                 for g in range(N_KV_HEADS)]
        p_sum = _pad_rows(jnp.concatenate(p_sum, axis=0), LANES).astype(bf16)
        imp = lax.dot_general(smap_ref[...], p_sum, NT_DIMS, preferred_element_type=f32)
        jj = lax.broadcasted_iota(jnp.int32, (N_BLK_PAD, LANES), 0)
        qp = past + lax.broadcasted_iota(jnp.int32, (N_BLK_PAD, LANES), 1) % n_new
        cur = qp // BLOCK_SLC
        forced = (jj == 0) | ((jj <= cur) & (jj > cur - N_LOCAL))
        imp = jnp.where(jj * BLOCK_SLC <= qp, jnp.where(forced, FORCED_SCORE, imp), NEG)
        imp_ref[...] = imp

        def rank_step(j, rank):
            row = imp_ref[pl.ds(j, 1), :]
            return rank + jnp.where((row > imp) | ((row == imp) & (jj > j)), 1.0, 0.0)

        rank = lax.fori_loop(0, n_blk, rank_step, jnp.zeros((N_BLK_PAD, LANES), f32))
        sel = jnp.where(rank < float(min(N_SELECT, n_blk)), 1.0, 0.0).T
        sel_ref[...] = jnp.concatenate(
            [sel[g * n_new:(g + 1) * n_new] for g in range(N_KV_HEADS) for _ in range(HEADS_PER_KV)], axis=0)

        w_buf = cw_ref.shape[-1]
        kt_w, vt_w = head_dim_major(cw_ref, 0), head_dim_major(cw_ref, 1)
        kwn = _pad_rows(kwn_ref[...], LANES)
        s1 = jnp.dot(q, kt_w, preferred_element_type=f32)
        s2 = lax.dot_general(q, kwn[:, :D_KV].astype(bf16), NT_DIMS, preferred_element_type=f32)
        wpos = jnp.concatenate([past - w_buf + lax.broadcasted_iota(jnp.int32, (1, w_buf), 1),
                                past + lax.broadcasted_iota(jnp.int32, (1, LANES), 1)], axis=1)
        dist = qpos - wpos
        p_w = _softmax_2d(jnp.concatenate([s1, s2], axis=1), (dist >= 0) & (dist < WINDOW)).astype(bf16)
        ow_ref[...] = (lax.dot_general(p_w[:, :w_buf], vt_w, NT_DIMS, preferred_element_type=f32)
                       + jnp.dot(p_w[:, w_buf:], kwn[:, D_KV:].astype(bf16), preferred_element_type=f32))

        m_ref[...] = jnp.full(m_ref.shape, NEG, f32)
        l_ref[...] = jnp.zeros(l_ref.shape, f32)
        acc_ref[...] = jnp.zeros(acc_ref.shape, f32)

    def online_update(s, msk, pv):
        s = jnp.where(msk, s, NEG)
        m_old = m_ref[...]
        m_new = jnp.maximum(m_old, jnp.max(s, axis=-1, keepdims=True))
        alpha = jnp.exp(m_old - m_new)
        p = jnp.where(msk, jnp.exp(s - m_new), 0.0)
        l_ref[...] = alpha * l_ref[...] + jnp.sum(p, axis=-1, keepdims=True)
        acc_ref[...] = alpha * acc_ref[...] + pv(p.astype(bf16))
        m_ref[...] = m_new

    kt = jnp.concatenate([head_dim_major(pg, 0) for pg in pages], axis=1)
    vt = jnp.concatenate([head_dim_major(pg, 1) for pg in pages], axis=1)
    n_keys = kt.shape[1]
    blk = lax.broadcasted_iota(jnp.int32, (N_BLK_PAD, n_keys), 0)
    key_blk = c * BLOCKS_PER_STEP + lax.broadcasted_iota(jnp.int32, (N_BLK_PAD, n_keys), 1) // BLOCK_SLC
    expand = jnp.where(blk == key_blk, 1.0, 0.0).astype(bf16)
    msk = jnp.dot(sel_ref[...].astype(bf16), expand, preferred_element_type=f32) > 0.5
    online_update(jnp.dot(q, kt, preferred_element_type=f32), msk,
                  lambda p: lax.dot_general(p, vt, NT_DIMS, preferred_element_type=f32))

    @pl.when(c == n_steps - 1)
    def _():
        ksn = _pad_rows(ksn_ref[...], LANES)
        kpos = past + lax.broadcasted_iota(jnp.int32, (1, LANES), 1)
        last_blk = past // BLOCK_SLC
        msk_new = (sel_ref[:, last_blk:last_blk + 1] > 0.5) & (kpos <= qpos)
        online_update(lax.dot_general(q, ksn[:, :D_KV].astype(bf16), NT_DIMS, preferred_element_type=f32), msk_new,
                      lambda p: jnp.dot(p, ksn[:, D_KV:].astype(bf16), preferred_element_type=f32))
        o_s = acc_ref[...] / l_ref[...]
        o_c, o_w = oc_ref[...], ow_ref[...]
        lane = lax.broadcasted_iota(jnp.int32, (n_new, LANES), 1)
        heads = []
        for h in range(N_HEADS_ATTN):
            r = slice(h * n_new, (h + 1) * n_new)
            gc = gate_ref[:, N_BRANCH * h:N_BRANCH * h + 1]
            gs = gate_ref[:, N_BRANCH * h + 1:N_BRANCH * h + 2]
            gw = gate_ref[:, N_BRANCH * h + 2:N_BRANCH * h + 3]
            o_h = gc * o_c[r] + gs * o_s[r] + gw * o_w[r]
            heads.append(o_h if (h // HEADS_PER_KV) == (h % 2) else pltpu.roll(o_h, HEAD_DIM, 1))
        for pair in range(N_HEADS_ATTN // 2):
            o_ref[:, pair * LANES:(pair + 1) * LANES] = jnp.where(lane < HEAD_DIM, heads[2 * pair], heads[2 * pair + 1])


def sample_tables(past, n_new):
    n_chunks = past // STRIDE_CMP
    start = np.arange(n_chunks) * STRIDE_CMP
    first, last = start // BLOCK_SLC, (start + BLOCK_CMP - 1) // BLOCK_SLC
    j = np.arange(N_BLK_PAD)
    smap = (first[None, :] <= j[:, None]) & (last[None, :] >= j[:, None]) & (np.arange(n_chunks)[None, :] < n_chunks - 1)
    return jnp.asarray(smap, bf16)


def sample_attn(q_rows, gates, y_req, cache_slc, layer, page_table, kv_s_new, kv_w_new, cache_win, smap):
    bd, n_pages = page_table.shape
    n_new = kv_s_new.shape[1]
    past = n_pages * PAGE_SIZE
    rows = N_HEADS_ATTN * n_new
    w_buf = cache_win.shape[-1]
    tile = (None, None, 2, N_KV_HEADS, HEAD_DIM)
    assert n_pages % PAGES_PER_STEP == 0 and -(-(past + n_new) // BLOCK_SLC) <= N_BLK_PAD and n_new <= LANES
    per_b = lambda b, c, pt: (b, 0, 0)
    page = lambda j: (lambda b, c, pt: (layer, pt[b, c * PAGES_PER_STEP + j], 0, 0, 0, 0))
    grid_spec = pltpu.PrefetchScalarGridSpec(
        num_scalar_prefetch=1,
        grid=(bd, n_pages // PAGES_PER_STEP),
        in_specs=[pl.BlockSpec((None, rows, LANES), per_b),
                  pl.BlockSpec((None, n_new, LANES), per_b),
                  pl.BlockSpec((None, past // STRIDE_CMP, 2 * KV_COLS), per_b)]
                 + [pl.BlockSpec(tile + (PAGE_SIZE,), page(j)) for j in range(PAGES_PER_STEP)]
                 + [pl.BlockSpec((None, n_new, KV_COLS), per_b),
                    pl.BlockSpec((None, n_new, KV_COLS), per_b),
                    pl.BlockSpec(tile + (w_buf,), lambda b, c, pt: (layer, b, 0, 0, 0, 0)),
                    pl.BlockSpec(smap.shape, lambda b, c, pt: (0, 0))],
        out_specs=pl.BlockSpec((None, n_new, D_ATTN), per_b),
        scratch_shapes=[pltpu.VMEM((rows, LANES), f32), pltpu.VMEM((rows, LANES), f32),
                        pltpu.VMEM((rows, N_BLK_PAD), f32), pltpu.VMEM((N_BLK_PAD, LANES), f32),
                        pltpu.VMEM((rows, 1), f32), pltpu.VMEM((rows, 1), f32), pltpu.VMEM((rows, LANES), f32)])
    return pl.pallas_call(
        functools.partial(_sample_attn_kernel, past=past, n_new=n_new),
        grid_spec=grid_spec,
        out_shape=jax.ShapeDtypeStruct((bd, n_new, D_ATTN), f32),
        compiler_params=pltpu.CompilerParams(
            dimension_semantics=("parallel", "arbitrary"), vmem_limit_bytes=VMEM_LIMIT),
        name="sample_attn",
    )(page_table, q_rows, gates, y_req, *([cache_slc] * PAGES_PER_STEP), kv_s_new, kv_w_new, cache_win, smap)


def _split(x, sizes):
    offs = [int(o) for o in np.cumsum(sizes)[:-1]]
    return jnp.split(x, offs, axis=-1)


def rwkv7_time_mix(zs, shift_prev, wkv0, v_first, lp):
    B, T = zs.shape[:2]
    zf = zs.astype(f32)
    prev = jnp.concatenate([shift_prev.astype(f32)[:, None], zf[:, :-1]], axis=1)
    xs = zf + (prev - zf) * lp['mu']
    r, k, v, xw, xa, xg = _split(xs, [D_RWKV] * 3 + [RANK_W, RANK_A, RANK_G])
    w_raw = -jax.nn.softplus(-(lp['w0'] + jnp.tanh(xw) @ lp['w2'])) - 0.5
    decay = jnp.exp(-jnp.exp(w_raw))
    a = jax.nn.sigmoid(lp['a0'] + xa @ lp['a2'])
    if lp['vres'] is None:
        v_first = v
    else:
        vres0, vres2 = lp['vres']
        v = v + (v_first - v) * jax.nn.sigmoid(vres0 + xa @ vres2)
    g = jax.nn.sigmoid(xg) @ lp['g2']
    heads = lambda t: t.reshape(B, T, N_HEADS_RWKV, HEAD_DIM)
    kk = heads(k * lp['k_k'])
    kk = kk * lax.rsqrt(jnp.maximum(jnp.sum(kk * kk, axis=-1, keepdims=True), 1e-24))
    k = k * (1.0 + (a - 1.0) * lp['k_a'])
    r_h, k_h, v_h, a_h, d_h = heads(r), heads(k), heads(v), heads(a), heads(decay)

    def step(S, inp):
        r_t, d_t, kk_t, a_t, k_t, v_t = inp
        sa = jnp.einsum('bhvk,bhk->bhv', S, -kk_t)
        S = S * d_t[:, :, None, :] + sa[..., None] * (kk_t * a_t)[:, :, None, :] + v_t[..., None] * k_t[:, :, None, :]
        return S, jnp.einsum('bhvk,bhk->bhv', S, r_t)

    tm = lambda t: jnp.moveaxis(t, 1, 0)
    S_T, ys = lax.scan(step, wkv0.astype(f32), (tm(r_h), tm(d_h), tm(kk), tm(a_h), tm(k_h), tm(v_h)))
    y = jnp.moveaxis(ys, 0, 1)
    mean = jnp.mean(y, axis=-1, keepdims=True)
    var = jnp.mean(jnp.square(y - mean), axis=-1, keepdims=True)
    y = (y - mean) * lax.rsqrt(var + GN_EPS) * lp['lnx_w'].reshape(N_HEADS_RWKV, HEAD_DIM) + lp['lnx_b'].reshape(N_HEADS_RWKV, HEAD_DIM)
    y = y + jnp.sum(r_h * k_h * lp['r_k'], axis=-1, keepdims=True) * v_h
    y = y.reshape(B, T, D_RWKV) * g
    return y, S_T, zs[:, -1], v_first


def sample_trunk_layer(x, layer, shift_prev, wkv0, v_first, lp, consts, caches):
    B, Tn = x.shape[:2]
    rows = B * Tn
    page_table = caches['page_table']
    x2 = x.reshape(rows, D_MODEL)
    z = norm_proj(x2, lp['norm1_g'], lp['w_in'])
    q_ext, kv_c, kv_s, kv_w, _, _, gates = attn_prep(
        z, consts['rope_s'], consts['seg_ones'], lp['q_g_lanes'], lp['k_g_lanes'], lp['gate_b_lanes'], 1, rows)
    q_rows = q_ext.reshape(N_HEADS_ATTN, B, Tn, LANES).transpose(1, 0, 2, 3).reshape(B, N_HEADS_ATTN * Tn, LANES)
    n_pool = caches['slc'].shape[1]
    y_req = caches['y_pool'][layer].reshape(n_pool, CHUNKS_PER_PAGE, 2 * KV_COLS)[page_table]
    y_req = y_req.reshape(B, page_table.shape[1] * CHUNKS_PER_PAGE, 2 * KV_COLS)
    new_rows = lambda kv: kv.reshape(B, Tn, KV_COLS)
    o_attn = sample_attn(q_rows, gates.reshape(B, Tn, LANES), y_req, caches['slc'], layer, page_table,
                         new_rows(kv_s), new_rows(kv_w), caches['win'], consts['smap_s'])
    tok = lambda kv: kv.reshape(B, Tn, 2, N_KV_HEADS, HEAD_DIM)
    win = jnp.concatenate([caches['win_raw'][layer][:, Tn:], tok(kv_w)], axis=1)
    attn_state = (tok(kv_c), tok(kv_s), win)
    z_rwkv = split_z(z)[2].reshape(B, Tn, SHIFT_COLS)
    o_rwkv, wkv_t, shift_t, v_first = rwkv7_time_mix(z_rwkv, shift_prev, wkv0, v_first, lp)
    h = out_proj(x2, o_attn.reshape(rows, D_ATTN), o_rwkv.reshape(rows, D_RWKV), lp['w_out'])
    y = ffn(h, lp['norm2_g'], lp['w_up'], lp['w_down']).reshape(B, Tn, D_MODEL)
    return y, attn_state, wkv_t, shift_t, v_first


def prompt_trunk_layer(x, v_first, lp, consts):
    B, T = x.shape[:2]
    x2 = x.reshape(B * T, D_MODEL)
    z = norm_proj(x2, lp['norm1_g'], lp['w_in'])
    q_ext, kv_c, kv_s, kv_w, kb, vvb, gates = attn_prep(
        z, consts['rope'], consts['seg_ones'], lp['q_g_lanes'], lp['k_g_lanes'], lp['gate_b_lanes'], B, T)
    kc, vvc = compress(kv_c, lp['cmp_wb'], lp['cmp_peb'], B)
    o_attn = nsa_prompt(q_ext, gates, kc, vvc, kb, vvb, consts['nsa'], B, T)
    pages = lambda kv: kv.reshape(B, T // PAGE_SIZE, PAGE_SIZE, 2, N_KV_HEADS, HEAD_DIM)
    attn_state = (pages(kv_c), pages(kv_s),
                  kv_w.reshape(B, T, 2, N_KV_HEADS, HEAD_DIM)[:, -min(WINDOW, T):])
    o_rwkv, st, v_first = rwkv_prompt(z, v_first, lp, consts, B, T)
    z_last = z.reshape(B, T, IN_COLS_PAD)[:, -1]
    shift_t = jnp.concatenate([z_last[:, Z_R:Z_KV], z_last[:, Z_LR:]], axis=-1)
    h = out_proj(x2, o_attn.reshape(B * T, D_ATTN), o_rwkv, lp['w_out'])
    y = ffn(h, lp['norm2_g'], lp['w_up'], lp['w_down']).reshape(B, T, D_MODEL)
    return y, attn_state, state_from_pair_tiles(st), shift_t, v_first


def kernel(x_prompt, x_sample, cache_cmp_kv, cache_slc_kv, cache_win_kv, state_wkv, state_shift, page_table,
           norm1_g, w_in, q_norm_g, k_norm_g, gate_b, cmp_pe, cmp_w, shift_mu, w0, w2, a0, a2, vres0, vres2,
           g2, k_k, k_a, r_k, lnx_w, lnx_b, w_out, norm2_g, w_up, w_down):
    B, T = x_prompt.shape[:2]
    Tn = x_sample.shape[1]
    past = page_table.shape[1] * PAGE_SIZE
    pos_p = jnp.arange(T, dtype=jnp.int32)
    pos_s = past + jnp.arange(Tn, dtype=jnp.int32)
    w_in_b = permute_w_in(w_in).astype(bf16)
    w_out_b, w_up_b, w_down_b = w_out.astype(bf16), w_up.astype(bf16), w_down.astype(bf16)
    seg = np.arange(LANES) // HEAD_DIM
    consts = {'rope': rope_lane_tables(pos_p), 'nsa': nsa_tables(T),
              'seg_ones': jnp.asarray(seg[:, None] == seg[None, :], bf16),
              'tri': jnp.asarray(np.tril(np.ones((RW_CHUNK, RW_CHUNK))), bf16),
              'shift_zero': jnp.zeros((B, 1, SHIFT_COLS), f32)}
    rwkv_raw = {'w0': w0, 'w2': w2, 'a0': a0, 'a2': a2, 'vres0': vres0, 'vres2': vres2, 'g2': g2, 'k_k': k_k,
                'k_a': k_a, 'r_k': r_k, 'lnx_w': lnx_w, 'lnx_b': lnx_b}
    Bd = x_sample.shape[0]
    n_pool, w_buf = cache_slc_kv.shape[1], cache_win_kv.shape[2]
    consts['rope_s'] = tuple(jnp.tile(t, (Bd, 1)) for t in rope_lane_tables(pos_s))
    consts['smap_s'] = sample_tables(past, Tn)
    cmp_wb, cmp_peb = jax.vmap(compress_weights)(cmp_pe, cmp_w)
    caches = {'page_table': page_table,
              'slc': cache_tiles(cache_slc_kv), 'win': cache_tiles(cache_win_kv), 'win_raw': cache_win_kv,
              'y_pool': compress_pool(cache_cmp_kv, cmp_wb, cmp_peb)}
    xp, xs = x_prompt, x_sample
    vf_p, vf_s = None, None
    p_cmp, p_slc, p_win, p_wkv, p_shift = [], [], [], [], []
    s_cmp, s_slc, s_win, s_wkv, s_shift = [], [], [], [], []
    for l in range(DEPTH):
        lp = {'norm1_g': norm1_g[l], 'w_in': w_in_b[l], 'q_norm_g': q_norm_g[l], 'k_norm_g': k_norm_g[l],
              'gate_b': gate_b[l], 'cmp_pe': cmp_pe[l], 'cmp_w': cmp_w[l], 'mu': shift_mu[l], 'w0': w0[l],
              'w2': w2[l], 'a0': a0[l], 'a2': a2[l],
              'vres': None if l == 0 else (vres0[l - 1], vres2[l - 1]),
              'g2': g2[l], 'k_k': k_k[l], 'k_a': k_a[l], 'r_k': r_k[l], 'lnx_w': lnx_w[l], 'lnx_b': lnx_b[l],
              'w_out': w_out_b[l], 'norm2_g': norm2_g[l], 'w_up': w_up_b[l], 'w_down': w_down_b[l]}
        lp['cmp_wb'], lp['cmp_peb'] = cmp_wb[l], cmp_peb[l]
        lp['q_g_lanes'] = jnp.tile(q_norm_g[l], 2).reshape(1, LANES)
        lp['k_g_lanes'] = jnp.tile(k_norm_g[l], (1, 2))
        lp['gate_b_lanes'] = jnp.pad(gate_b[l], (0, LANES - N_BRANCH * N_HEADS_ATTN)).reshape(1, LANES)
        lp['mu_row'] = shift_mu[l].reshape(1, SHIFT_COLS)
        lp['rwkv_vecs'], lp['rwkv_lowrank'] = rwkv_params(rwkv_raw, l)
        xp, (c_kv, sl_kv, w_kv), wkv_t, sh_t, vf_p = prompt_trunk_layer(xp, vf_p, lp, consts)
        p_cmp.append(c_kv); p_slc.append(sl_kv); p_win.append(w_kv); p_wkv.append(wkv_t); p_shift.append(sh_t)
        xs, (c_kv, sl_kv, w_kv), wkv_t, sh_t, vf_s = sample_trunk_layer(
            xs, l, state_shift[l], state_wkv[l], vf_s, lp, consts, caches)
        s_cmp.append(c_kv); s_slc.append(sl_kv); s_win.append(w_kv); s_wkv.append(wkv_t); s_shift.append(sh_t)
    return (xp, xs, jnp.stack(p_cmp), jnp.stack(p_slc), jnp.stack(p_win), jnp.stack(p_wkv), jnp.stack(p_shift),
            jnp.stack(s_cmp), jnp.stack(s_slc), jnp.stack(s_win), jnp.stack(s_wkv), jnp.stack(s_shift))
```

```python
import functools
import jax, jax.numpy as jnp
from jax import lax
import numpy as np
from jax.experimental import pallas as pl
from jax.experimental.pallas import tpu as pltpu

D_MODEL = 1024
DEPTH = 4
PAGE_SIZE = 128
HEAD_DIM = 64
N_HEADS_ATTN = 8
N_KV_HEADS = 2
HEADS_PER_KV = N_HEADS_ATTN // N_KV_HEADS
D_ATTN = N_HEADS_ATTN * HEAD_DIM
D_KV = N_KV_HEADS * HEAD_DIM
N_BRANCH = 3
N_HEADS_RWKV = 8
D_RWKV = N_HEADS_RWKV * HEAD_DIM
ROT_DIM = HEAD_DIM // 4
ROPE_THETA = 500000.0
BLOCK_CMP = 32
STRIDE_CMP = 16
BLOCK_SLC = 64
N_SELECT = 16
N_LOCAL = 2
WINDOW = 512
QBLOCK = 128
RANK_W = 32
RANK_A = 32
RANK_G = 64
D_FF = 4 * D_MODEL
SHIFT_COLS = 3 * D_RWKV + RANK_W + RANK_A + RANK_G
IN_COLS = D_ATTN + 6 * D_KV + N_BRANCH * N_HEADS_ATTN + SHIFT_COLS
IN_COLS_PAD = 3072
NORM_EPS = 1e-6
GN_EPS = 64e-5
NEG = -1e30
FORCED_SCORE = 1e9

VMEM_LIMIT = 48 * 1024 * 1024
bf16 = jnp.bfloat16
f32 = jnp.float32


def _row_tile(m):
    return min(m, 1024)


def _norm_proj_kernel(x_ref, g_ref, w_ref, o_ref):
    x = x_ref[...]
    n = x * lax.rsqrt(jnp.mean(x * x, axis=-1, keepdims=True) + NORM_EPS) * g_ref[...]
    o_ref[...] = jnp.dot(n.astype(bf16), w_ref[...], preferred_element_type=f32)


def norm_proj(x, g, w):
    m, n = x.shape[0], w.shape[1]
    tm, tn = _row_tile(m), 512
    return pl.pallas_call(
        _norm_proj_kernel,
        grid=(m // tm, n // tn),
        in_specs=[pl.BlockSpec((tm, D_MODEL), lambda i, j: (i, 0)),
                  pl.BlockSpec((1, D_MODEL), lambda i, j: (0, 0)),
                  pl.BlockSpec((D_MODEL, tn), lambda i, j: (0, j))],
        out_specs=pl.BlockSpec((tm, tn), lambda i, j: (i, j)),
        out_shape=jax.ShapeDtypeStruct((m, n), f32),
        compiler_params=pltpu.CompilerParams(
            dimension_semantics=("parallel", "arbitrary"), vmem_limit_bytes=VMEM_LIMIT),
        name="norm_proj",
    )(x, g.reshape(1, D_MODEL), w)


def _out_proj_kernel(x_ref, a_ref, r_ref, w_ref, o_ref):
    o_ref[...] = (x_ref[...]
                  + jnp.dot(a_ref[...].astype(bf16), w_ref[:D_ATTN, :], preferred_element_type=f32)
                  + jnp.dot(r_ref[...].astype(bf16), w_ref[D_ATTN:, :], preferred_element_type=f32))


def out_proj(x, o_attn, o_rwkv, w):
    m = x.shape[0]
    tm = _row_tile(m)
    return pl.pallas_call(
        _out_proj_kernel,
        grid=(m // tm,),
        in_specs=[pl.BlockSpec((tm, D_MODEL), lambda i: (i, 0)),
                  pl.BlockSpec((tm, D_ATTN), lambda i: (i, 0)),
                  pl.BlockSpec((tm, D_RWKV), lambda i: (i, 0)),
                  pl.BlockSpec((D_MODEL, D_MODEL), lambda i: (0, 0))],
        out_specs=pl.BlockSpec((tm, D_MODEL), lambda i: (i, 0)),
        out_shape=jax.ShapeDtypeStruct((m, D_MODEL), f32),
        compiler_params=pltpu.CompilerParams(
            dimension_semantics=("parallel",), vmem_limit_bytes=VMEM_LIMIT),
        name="out_proj",
    )(x, o_attn, o_rwkv, w)


def _ffn_kernel(h_ref, g_ref, wu_ref, wd_ref, o_ref, n_ref):
    j = pl.program_id(1)

    @pl.when(j == 0)
    def _():
        h = h_ref[...]
        n = h * lax.rsqrt(jnp.mean(h * h, axis=-1, keepdims=True) + NORM_EPS) * g_ref[...]
        n_ref[...] = n.astype(bf16)
        o_ref[...] = h

    u = jnp.dot(n_ref[...], wu_ref[...], preferred_element_type=f32)
    u = jnp.square(jnp.maximum(u, 0.0))
    o_ref[...] += jnp.dot(u.astype(bf16), wd_ref[...], preferred_element_type=f32)


def ffn(h, g, w_up, w_down):
    m = h.shape[0]
    tm, tf = _row_tile(m), 512
    return pl.pallas_call(
        _ffn_kernel,
        grid=(m // tm, D_FF // tf),
        in_specs=[pl.BlockSpec((tm, D_MODEL), lambda i, j: (i, 0)),
                  pl.BlockSpec((1, D_MODEL), lambda i, j: (0, 0)),
                  pl.BlockSpec((D_MODEL, tf), lambda i, j: (0, j)),
                  pl.BlockSpec((tf, D_MODEL), lambda i, j: (j, 0))],
        out_specs=pl.BlockSpec((tm, D_MODEL), lambda i, j: (i, 0)),
        out_shape=jax.ShapeDtypeStruct((m, D_MODEL), f32),
        scratch_shapes=[pltpu.VMEM((tm, D_MODEL), bf16)],
        compiler_params=pltpu.CompilerParams(
            dimension_semantics=("parallel", "arbitrary"), vmem_limit_bytes=VMEM_LIMIT),
        name="ffn",
    )(h, g.reshape(1, D_MODEL), w_up, w_down)


LANES = 128
KV_COLS = 2 * D_KV
N_GATES = N_BRANCH * N_HEADS_ATTN
N_LOWRANK = RANK_W + RANK_A + RANK_G
Z_Q, Z_R, Z_K, Z_V = 0, D_ATTN, D_ATTN + D_RWKV, D_ATTN + 2 * D_RWKV
Z_KV = D_ATTN + 3 * D_RWKV
Z_GATE = Z_KV + N_BRANCH * KV_COLS
Z_LR = Z_GATE + LANES
assert N_LOWRANK == LANES and Z_LR + N_LOWRANK == IN_COLS_PAD and N_GATES <= LANES


def permute_w_in(w_in):
    o_kv, o_gate, o_rwkv = D_ATTN, D_ATTN + 6 * D_KV, D_ATTN + 6 * D_KV + N_GATES
    pad = jnp.zeros(w_in.shape[:-1] + (LANES - N_GATES,), w_in.dtype)
    return jnp.concatenate([w_in[..., :o_kv], w_in[..., o_rwkv:o_rwkv + 3 * D_RWKV], w_in[..., o_kv:o_gate],
                            w_in[..., o_gate:o_rwkv], pad, w_in[..., o_rwkv + 3 * D_RWKV:]], axis=-1)


def split_z(z):
    z_attn = jnp.concatenate([z[..., :D_ATTN], z[..., Z_KV:Z_GATE]], axis=-1)
    z_rwkv = jnp.concatenate([z[..., Z_R:Z_KV], z[..., Z_LR:]], axis=-1)
    return z_attn, z[..., Z_GATE:Z_GATE + N_GATES], z_rwkv


def _seg_sumsq(x, p_ref):
    x2 = x * x
    hi = x2.astype(bf16)
    lo = (x2 - hi.astype(f32)).astype(bf16)
    return (jnp.dot(hi, p_ref[...], preferred_element_type=f32)
            + jnp.dot(lo, p_ref[...], preferred_element_type=f32))


def _norm_rope(x, g, p_ref, c, s1, s2):
    y = x * lax.rsqrt(_seg_sumsq(x, p_ref) * (1.0 / HEAD_DIM) + NORM_EPS) * g
    return y * c + pltpu.roll(y, LANES - ROT_DIM // 2, 1) * s1 + pltpu.roll(y, ROT_DIM // 2, 1) * s2


def _attn_prep_kernel(zq_ref, zc_ref, zs_ref, zw_ref, zg_ref, c_ref, s1_ref, s2_ref, p_ref, qg_ref, kg_ref, gb_ref,
                      q_ref, kvc_ref, kvs_ref, kvw_ref, kb_ref, vvb_ref, gate_ref):
    c, s1, s2 = c_ref[...], s1_ref[...], s2_ref[...]
    lane = lax.broadcasted_iota(jnp.int32, (zq_ref.shape[0], LANES), 1)
    for pair in range(N_HEADS_ATTN // 2):
        y = _norm_rope(zq_ref[:, pair * LANES:(pair + 1) * LANES], qg_ref[...], p_ref, c, s1, s2)
        y = y * (HEAD_DIM ** -0.5)
        y_sw = pltpu.roll(y, HEAD_DIM, 1)
        for half in range(2):
            h = 2 * pair + half
            grp = h // HEADS_PER_KV
            src = y if half == grp else y_sw
            keep = (lane >= HEAD_DIM) if grp == 1 else (lane < HEAD_DIM)
            q_ref[0, h] = jnp.where(keep, src, 0.0).astype(bf16)
    ins = (zc_ref, zs_ref, zw_ref)
    outs = (kvc_ref, kvs_ref, kvw_ref)
    for br in range(N_BRANCH):
        k = _norm_rope(ins[br][:, :D_KV], kg_ref[br:br + 1, :], p_ref, c, s1, s2)
        v = ins[br][:, D_KV:]
        outs[br][:, :D_KV] = k
        outs[br][:, D_KV:] = v
        if br > 0:
            kb_ref[br - 1] = k.astype(bf16)
            vvb_ref[br - 1, :, :D_KV] = v.astype(bf16)
            vvb_ref[br - 1, :, D_KV:] = pltpu.roll(v, HEAD_DIM, 1).astype(bf16)
    gate_ref[...] = jax.nn.sigmoid(zg_ref[...] + gb_ref[...])


def rope_lane_tables(pos):
    half = ROT_DIM // 2
    inv = ROPE_THETA ** (-(jnp.arange(0, ROT_DIM, 2, dtype=f32) / ROT_DIM))
    ang = pos.astype(f32)[:, None] * inv[None, :]
    cos, sin = jnp.cos(ang), jnp.sin(ang)
    t = pos.shape[0]
    one, zero = jnp.ones((t, HEAD_DIM - ROT_DIM), f32), jnp.zeros((t, HEAD_DIM - ROT_DIM), f32)
    zh = jnp.zeros((t, half), f32)
    c = jnp.concatenate([cos, cos, one], axis=1)
    s1 = jnp.concatenate([-sin, zh, zero], axis=1)
    s2 = jnp.concatenate([zh, sin, zero], axis=1)
    return tuple(jnp.tile(a, (1, 2)) for a in (c, s1, s2))


def attn_prep(z, rope_tabs, seg_ones, q_g, k_g, gate_b, batch, seq):
    m = batch * seq
    tm = min(512, seq)
    tiles = seq // tm
    c, s1, s2 = rope_tabs
    row = lambda i: (i, 0)
    tab = lambda i: (i % tiles, 0)
    const = lambda i: (0, 0)
    kv_shape = jax.ShapeDtypeStruct((m, KV_COLS), f32)
    return pl.pallas_call(
        _attn_prep_kernel,
        grid=(m // tm,),
        in_specs=[pl.BlockSpec((tm, D_ATTN), row),
                  pl.BlockSpec((tm, KV_COLS), lambda i: (i, Z_KV // KV_COLS)),
                  pl.BlockSpec((tm, KV_COLS), lambda i: (i, Z_KV // KV_COLS + 1)),
                  pl.BlockSpec((tm, KV_COLS), lambda i: (i, Z_KV // KV_COLS + 2)),
                  pl.BlockSpec((tm, LANES), lambda i: (i, Z_GATE // LANES)),
                  pl.BlockSpec((tm, LANES), tab), pl.BlockSpec((tm, LANES), tab), pl.BlockSpec((tm, LANES), tab),
                  pl.BlockSpec((LANES, LANES), const),
                  pl.BlockSpec((1, LANES), const), pl.BlockSpec((N_BRANCH, LANES), const),
                  pl.BlockSpec((1, LANES), const)],
        out_specs=[pl.BlockSpec((1, N_HEADS_ATTN, tm, LANES), lambda i: (i // tiles, 0, i % tiles, 0)),
                   pl.BlockSpec((tm, KV_COLS), row), pl.BlockSpec((tm, KV_COLS), row),
                   pl.BlockSpec((tm, KV_COLS), row),
                   pl.BlockSpec((2, tm, D_KV), lambda i: (0, i, 0)),
                   pl.BlockSpec((2, tm, KV_COLS), lambda i: (0, i, 0)),
                   pl.BlockSpec((tm, LANES), row)],
        out_shape=[jax.ShapeDtypeStruct((batch, N_HEADS_ATTN, seq, LANES), bf16),
                   kv_shape, kv_shape, kv_shape,
                   jax.ShapeDtypeStruct((2, m, D_KV), bf16),
                   jax.ShapeDtypeStruct((2, m, KV_COLS), bf16),
                   jax.ShapeDtypeStruct((m, LANES), f32)],
        compiler_params=pltpu.CompilerParams(
            dimension_semantics=("parallel",), vmem_limit_bytes=VMEM_LIMIT),
        name="attn_prep",
    )(z, z, z, z, z, c, s1, s2, seg_ones, q_g, k_g, gate_b)


CHUNK_COLS = STRIDE_CMP * KV_COLS
CMP_R = BLOCK_CMP // STRIDE_CMP


def compress_weights(cmp_pe, cmp_w):
    pe = cmp_pe.reshape(2, CMP_R, STRIDE_CMP, HEAD_DIM)
    w = cmp_w.reshape(2, CMP_R, STRIDE_CMP, HEAD_DIM, HEAD_DIM)
    eye_c, eye_g = jnp.eye(2, dtype=f32), jnp.eye(N_KV_HEADS, dtype=f32)
    wb = jnp.einsum('cisde,cx,gy->iscgdxye', w, eye_c, eye_g).reshape(CMP_R, CHUNK_COLS, KV_COLS)
    peb = jnp.broadcast_to(jnp.transpose(pe, (1, 2, 0, 3))[:, :, :, None, :],
                           (CMP_R, STRIDE_CMP, 2, N_KV_HEADS, HEAD_DIM)).reshape(CMP_R, 1, CHUNK_COLS)
    return wb.astype(bf16), peb


def _compress_kernel(x_ref, pe_ref, w_ref, kc_ref, vvc_ref):
    x = x_ref[0]
    n = x.shape[0]
    y0 = jnp.dot((x + pe_ref[0]).astype(bf16), w_ref[0], preferred_element_type=f32)
    y1 = jnp.dot((x + pe_ref[1]).astype(bf16), w_ref[1], preferred_element_type=f32)
    comp = y0 + pltpu.roll(y1, n - 1, 0)
    row = lax.broadcasted_iota(jnp.int32, comp.shape, 0)
    comp = jnp.where(row < n - 1, comp, 0.0)
    kc_ref[0] = comp[:, :D_KV].astype(bf16)
    vc = comp[:, D_KV:]
    vvc_ref[0, :, :D_KV] = vc.astype(bf16)
    vvc_ref[0, :, D_KV:] = pltpu.roll(vc, HEAD_DIM, 1).astype(bf16)


def compress(kv, wb, peb, batch):
    n_chunks = kv.shape[0] // batch // STRIDE_CMP
    x = kv.reshape(batch, n_chunks, CHUNK_COLS)
    return pl.pallas_call(
        _compress_kernel,
        grid=(batch,),
        in_specs=[pl.BlockSpec((1, n_chunks, CHUNK_COLS), lambda b: (b, 0, 0)),
                  pl.BlockSpec((CMP_R, 1, CHUNK_COLS), lambda b: (0, 0, 0)),
                  pl.BlockSpec((CMP_R, CHUNK_COLS, KV_COLS), lambda b: (0, 0, 0))],
        out_specs=[pl.BlockSpec((1, n_chunks, D_KV), lambda b: (b, 0, 0)),
                   pl.BlockSpec((1, n_chunks, KV_COLS), lambda b: (b, 0, 0))],
        out_shape=[jax.ShapeDtypeStruct((batch, n_chunks, D_KV), bf16),
                   jax.ShapeDtypeStruct((batch, n_chunks, KV_COLS), bf16)],
        compiler_params=pltpu.CompilerParams(
            dimension_semantics=("parallel",), vmem_limit_bytes=VMEM_LIMIT),
        name="compress",
    )(x, peb, wb)


KEY_CHUNK = 512
NT_DIMS = (((1,), (1,)), ((), ()))


def _softmax_rows(s3, mask):
    s3 = jnp.where(mask[None], s3, NEG)
    m = jnp.max(s3, axis=-1, keepdims=True)
    p = jnp.where(mask[None], jnp.exp(s3 - m), 0.0)
    l = jnp.sum(p, axis=-1, keepdims=True)
    return p * jnp.where(l > 0.0, 1.0 / l, 0.0)


def _nsa_prompt_kernel(q_ref, gate_ref, kc_ref, vvc_ref, ks_ref, vvs_ref, kw_ref, vvw_ref, smap_ref, e_ref,
                       o_ref, mask_ref, m_ref, l_ref, acc_ref, *, seq):
    n_slc = seq // BLOCK_SLC
    n_cmp = seq // STRIDE_CMP
    hq = HEADS_PER_KV * QBLOCK
    q0 = pl.program_id(1) * QBLOCK
    qpos_col = q0 + lax.broadcasted_iota(jnp.int32, (QBLOCK, 1), 0)

    n_key_chunks = seq // KEY_CHUNK
    qs, o_cs = [], []
    for g in range(N_KV_HEADS):
        q = q_ref[0, g * HEADS_PER_KV:(g + 1) * HEADS_PER_KV].reshape(hq, LANES)
        qs.append(q)

        s = lax.dot_general(q, kc_ref[0], NT_DIMS, preferred_element_type=f32)
        c_end = lax.broadcasted_iota(jnp.int32, (1, n_cmp), 1) * STRIDE_CMP + (BLOCK_CMP - 1)
        p_c = _softmax_rows(s.reshape(HEADS_PER_KV, QBLOCK, n_cmp), c_end <= qpos_col)
        p_c = p_c.astype(bf16).reshape(hq, n_cmp)
        o_cs.append(jnp.dot(p_c, vvc_ref[0], preferred_element_type=f32))

        imp = jnp.zeros((n_slc, QBLOCK), f32)
        for h in range(HEADS_PER_KV):
            imp = imp + lax.dot_general(smap_ref[...], p_c[h * QBLOCK:(h + 1) * QBLOCK], NT_DIMS,
                                        preferred_element_type=f32)
        jj = lax.broadcasted_iota(jnp.int32, (n_slc, QBLOCK), 0)
        qp = q0 + lax.broadcasted_iota(jnp.int32, (n_slc, QBLOCK), 1)
        cur = qp // BLOCK_SLC
        forced = (jj == 0) | ((jj <= cur) & (jj > cur - N_LOCAL))
        imp = jnp.where(jj * BLOCK_SLC <= qp, jnp.where(forced, FORCED_SCORE, imp), NEG)
        rank = jnp.zeros((n_slc, QBLOCK), f32)
        for j in range(n_slc):
            row = imp[j:j + 1, :]
            rank = rank + jnp.where((row > imp) | ((row == imp) & (jj > j)), 1.0, 0.0)
        sel_t = jnp.where(rank < float(min(N_SELECT, n_slc)), 1.0, 0.0)
        if n_slc < LANES:
            sel_t = jnp.concatenate([sel_t, jnp.zeros((LANES - n_slc, QBLOCK), f32)], axis=0)
        sel = sel_t.T.astype(bf16)
        for c in range(n_key_chunks):
            mask_ref[g * n_key_chunks + c] = jnp.dot(sel, e_ref[:, c * KEY_CHUNK:(c + 1) * KEY_CHUNK],
                                                     preferred_element_type=f32)

    m_ref[...] = jnp.full(m_ref.shape, NEG, f32)
    l_ref[...] = jnp.zeros(l_ref.shape, f32)
    acc_ref[...] = jnp.zeros(acc_ref.shape, f32)

    def sel_step(c, carry):
        k0 = pl.multiple_of(c * KEY_CHUNK, KEY_CHUNK)
        k_chunk = ks_ref[0, pl.ds(k0, KEY_CHUNK), :]
        vv_chunk = vvs_ref[0, pl.ds(k0, KEY_CHUNK), :]
        causal = (k0 + lax.broadcasted_iota(jnp.int32, (1, KEY_CHUNK), 1)) <= qpos_col
        for g in range(N_KV_HEADS):
            hs = slice(g * HEADS_PER_KV, (g + 1) * HEADS_PER_KV)
            sc = lax.dot_general(qs[g], k_chunk, NT_DIMS, preferred_element_type=f32)
            msk = ((mask_ref[g * n_key_chunks + c] > 0.5) & causal)[None]
            s3 = jnp.where(msk, sc.reshape(HEADS_PER_KV, QBLOCK, KEY_CHUNK), NEG)
            m_old = m_ref[hs]
            m_new = jnp.maximum(m_old, jnp.max(s3, axis=-1, keepdims=True))
            alpha = jnp.exp(m_old - m_new)
            p = jnp.where(msk, jnp.exp(s3 - m_new), 0.0)
            l_ref[hs] = alpha * l_ref[hs] + jnp.sum(p, axis=-1, keepdims=True)
            pv = jnp.dot(p.astype(bf16).reshape(hq, KEY_CHUNK), vv_chunk, preferred_element_type=f32)
            acc_ref[hs] = alpha * acc_ref[hs] + pv.reshape(HEADS_PER_KV, QBLOCK, KV_COLS)
            m_ref[hs] = m_new
        return carry

    lax.fori_loop(0, q0 // KEY_CHUNK + 1, sel_step, 0)

    heads_out = []
    for g in range(N_KV_HEADS):
        q, o_c = qs[g], o_cs[g]
        hs = slice(g * HEADS_PER_KV, (g + 1) * HEADS_PER_KV)
        o_s = (acc_ref[hs] / l_ref[hs]).reshape(hq, KV_COLS)

        n_win = WINDOW + QBLOCK
        w0 = pl.multiple_of(jnp.maximum(q0 - WINDOW, 0), QBLOCK)
        sw = lax.dot_general(q, kw_ref[0, pl.ds(w0, n_win), :], NT_DIMS, preferred_element_type=f32)
        dist = qpos_col - (w0 + lax.broadcasted_iota(jnp.int32, (1, n_win), 1))
        p_w = _softmax_rows(sw.reshape(HEADS_PER_KV, QBLOCK, n_win), (dist >= 0) & (dist < WINDOW))
        o_w = jnp.dot(p_w.astype(bf16).reshape(hq, n_win), vvw_ref[0, pl.ds(w0, n_win), :],
                      preferred_element_type=f32)

        for hh in range(HEADS_PER_KV):
            h = g * HEADS_PER_KV + hh
            half = 0 if (h % 2) == g else 1
            rows = slice(hh * QBLOCK, (hh + 1) * QBLOCK)
            cols = slice(half * LANES, (half + 1) * LANES)
            gc = gate_ref[:, N_BRANCH * h:N_BRANCH * h + 1]
            gs = gate_ref[:, N_BRANCH * h + 1:N_BRANCH * h + 2]
            gw = gate_ref[:, N_BRANCH * h + 2:N_BRANCH * h + 3]
            heads_out.append(gc * o_c[rows, cols] + gs * o_s[rows, cols] + gw * o_w[rows, cols])

    lane = lax.broadcasted_iota(jnp.int32, (QBLOCK, LANES), 1)
    for pair in range(N_HEADS_ATTN // 2):
        o_ref[0, :, pair * LANES:(pair + 1) * LANES] = jnp.where(
            lane < HEAD_DIM, heads_out[2 * pair], heads_out[2 * pair + 1])


def nsa_tables(seq):
    n_slc, n_cmp = seq // BLOCK_SLC, seq // STRIDE_CMP
    start = np.arange(n_cmp) * STRIDE_CMP
    first, last = start // BLOCK_SLC, (start + BLOCK_CMP - 1) // BLOCK_SLC
    j = np.arange(n_slc)
    smap = (first[None, :] <= j[:, None]) & (last[None, :] >= j[:, None]) & (np.arange(n_cmp)[None, :] < n_cmp - 1)
    expand = (np.arange(seq)[None, :] // BLOCK_SLC) == np.arange(max(n_slc, LANES))[:, None]
    return jnp.asarray(smap, bf16), jnp.asarray(expand, bf16)


def nsa_prompt(q_ext, gates, kc, vvc, kb, vvb, tables, batch, seq):
    smap, expand = tables
    n_cmp = seq // STRIDE_CMP
    hq = HEADS_PER_KV * QBLOCK
    per_b = lambda b, i: (b, 0, 0)
    const = lambda b, i: (0, 0)
    gates3 = gates.reshape(batch, seq, LANES)
    kb4 = kb.reshape(2, batch, seq, D_KV)
    vvb4 = vvb.reshape(2, batch, seq, KV_COLS)
    branch = lambda br: (lambda b, i: (br, b, 0, 0))
    return pl.pallas_call(
        functools.partial(_nsa_prompt_kernel, seq=seq),
        grid=(batch, seq // QBLOCK),
        in_specs=[pl.BlockSpec((1, N_HEADS_ATTN, QBLOCK, LANES), lambda b, i: (b, 0, i, 0)),
                  pl.BlockSpec((None, QBLOCK, LANES), lambda b, i: (b, i, 0)),
                  pl.BlockSpec((1, n_cmp, D_KV), per_b), pl.BlockSpec((1, n_cmp, KV_COLS), per_b),
                  pl.BlockSpec((None, 1, seq, D_KV), branch(0)), pl.BlockSpec((None, 1, seq, KV_COLS), branch(0)),
                  pl.BlockSpec((None, 1, seq, D_KV), branch(1)), pl.BlockSpec((None, 1, seq, KV_COLS), branch(1)),
                  pl.BlockSpec(smap.shape, const), pl.BlockSpec(expand.shape, const)],
        out_specs=pl.BlockSpec((1, QBLOCK, D_ATTN), lambda b, i: (b, i, 0)),
        out_shape=jax.ShapeDtypeStruct((batch, seq, D_ATTN), f32),
        scratch_shapes=[pltpu.VMEM((N_KV_HEADS * (seq // KEY_CHUNK), QBLOCK, KEY_CHUNK), f32),
                        pltpu.VMEM((N_HEADS_ATTN, QBLOCK, 1), f32),
                        pltpu.VMEM((N_HEADS_ATTN, QBLOCK, 1), f32),
                        pltpu.VMEM((N_HEADS_ATTN, QBLOCK, KV_COLS), f32)],
        compiler_params=pltpu.CompilerParams(
            dimension_semantics=("parallel", "arbitrary"), vmem_limit_bytes=VMEM_LIMIT),
        name="nsa_prompt",
    )(q_ext, gates3, kc, vvc, kb4, vvb4, kb4, vvb4, smap, expand)


RW_CHUNK = 64
N_PAIRS = N_HEADS_RWKV // 2
TN_DIMS = (((0,), (0,)), ((), ()))


def _mm(a, b):
    return jnp.dot(a.astype(bf16), b.astype(bf16), preferred_element_type=f32)


def _mm_nt(a, b):
    return lax.dot_general(a.astype(bf16), b.astype(bf16), NT_DIMS, preferred_element_type=f32)


def _mm_tn(a, b):
    return lax.dot_general(a.astype(bf16), b.astype(bf16), TN_DIMS, preferred_element_type=f32)


def _mm_split(a01, x):
    hi = x.astype(bf16)
    lo = (x - hi.astype(f32)).astype(bf16)
    return jnp.dot(a01, hi, preferred_element_type=f32) + jnp.dot(a01, lo, preferred_element_type=f32)


def _seg_sum(x, p_ref):
    hi = x.astype(bf16)
    lo = (x - hi.astype(f32)).astype(bf16)
    return (jnp.dot(hi, p_ref[...], preferred_element_type=f32)
            + jnp.dot(lo, p_ref[...], preferred_element_type=f32))


def _stack_heads(x):
    lane = lax.broadcasted_iota(jnp.int32, x.shape, 1)
    return jnp.concatenate([jnp.where(lane < HEAD_DIM, x, 0.0), jnp.where(lane >= HEAD_DIM, x, 0.0)], axis=0)


def _chunk_pair(r, cum, lw, alpha, beta, k, v, st):
    pairs = range(len(r))
    c = r[0].shape[0]
    eg = [jnp.exp(cum[i]) for i in pairs]
    egi = [jnp.exp(-cum[i]) for i in pairs]
    egp = [jnp.exp(cum[i] - lw[i]) for i in pairs]
    g_end = [eg[i][c - 1:c, :] for i in pairs]
    x_a = [_stack_heads(alpha[i] * egp[i]) for i in pairs]
    x_r = [_stack_heads(r[i] * eg[i]) for i in pairs]
    bt = [beta[i] * egi[i] for i in pairs]
    kt = [k[i] * egi[i] for i in pairs]
    b_b = [jnp.concatenate([bt[i], bt[i]], axis=0) for i in pairs]
    b_k = [jnp.concatenate([kt[i], kt[i]], axis=0) for i in pairs]
    x_bg = [_stack_heads(bt[i] * g_end[i]) for i in pairs]
    x_kg = [_stack_heads(kt[i] * g_end[i]) for i in pairs]
    v_s = [_stack_heads(v[i]) for i in pairs]
    row = lax.broadcasted_iota(jnp.int32, (2 * c, 2 * c), 0)
    col = lax.broadcasted_iota(jnp.int32, (2 * c, 2 * c), 1)
    same = (row // c) == (col // c)
    strict = same & (col < row)
    incl = same & (col <= row)
    n = [jnp.where(strict, _mm_nt(x_a[i], b_b[i]), 0.0) for i in pairs]
    a_ak = [jnp.where(strict, _mm_nt(x_a[i], b_k[i]), 0.0) for i in pairs]
    a_rb = [jnp.where(incl, _mm_nt(x_r[i], b_b[i]), 0.0) for i in pairs]
    a_rk = [jnp.where(incl, _mm_nt(x_r[i], b_k[i]), 0.0) for i in pairs]
    def level(i, lv):
        return jnp.where(((row >> (lv + 1)) == (col >> (lv + 1))) & ((row >> lv) != (col >> lv)), n[i], 0.0)
    inv = [jnp.where(row == col, 1.0, 0.0) + level(i, 0) for i in pairs]
    for lv in range(1, int(np.log2(c))):
        half = [_mm(inv[i], level(i, lv)) for i in pairs]
        inv = [inv[i] + _mm(half[i], inv[i]) for i in pairs]
    st_b = [st[i].astype(bf16) for i in pairs]
    p = [_mm(x_a[i], st_b[i]) + _mm(a_ak[i], v_s[i]) for i in pairs]
    e = [_mm(inv[i], p[i]) for i in pairs]
    y_s = [_mm(x_r[i], st_b[i]) + _mm(a_rb[i], e[i]) + _mm(a_rk[i], v_s[i]) for i in pairs]
    g_col = [jnp.broadcast_to(g_end[i], (2 * c, 2 * c)).T for i in pairs]
    st_new = [st[i] * g_col[i] + _mm_tn(x_bg[i], e[i]) + _mm_tn(x_kg[i], v_s[i]) for i in pairs]
    return [y_s[i][:c] + y_s[i][c:] for i in pairs], st_new


def _rwkv_prompt_kernel(*refs, first_layer):
    zr_ref, zk_ref, zv_ref, zl_ref, sp_ref, mu_ref, vec_ref, lr_ref, p_ref, tri_ref = refs[:10]
    if first_layer:
        o_ref, st_ref, vfo_ref, prev_ref = refs[10:]
    else:
        vf_ref, o_ref, st_ref, prev_ref = refs[10:]
    c = RW_CHUNK
    ci = pl.program_id(1)

    @pl.when(ci == 0)
    def _():
        prev_ref[0:1, :] = sp_ref[...]
        st_ref[...] = jnp.zeros(st_ref.shape, f32)

    def shifted(z_ref, lo, hi):
        z = z_ref[...]
        first = lax.broadcasted_iota(jnp.int32, z.shape, 0) == 0
        prev = jnp.where(first, prev_ref[0:1, lo:hi], pltpu.roll(z, 1, 0))
        xs = z + (prev - z) * mu_ref[:, lo:hi]
        prev_ref[0:1, lo:hi] = z[c - 1:c, :]
        return xs

    xr = shifted(zr_ref, 0, D_RWKV)
    xk = shifted(zk_ref, D_RWKV, 2 * D_RWKV)
    xv = shifted(zv_ref, 2 * D_RWKV, 3 * D_RWKV)
    xl = shifted(zl_ref, 3 * D_RWKV, SHIFT_COLS)
    w0, a0, vres0, k_k = vec_ref[0:1, :], vec_ref[1:2, :], vec_ref[2:3, :], vec_ref[3:4, :]
    k_a, r_k, lnx_w, lnx_b = vec_ref[4:5, :], vec_ref[5:6, :], vec_ref[6:7, :], vec_ref[7:8, :]

    xl_b = xl.astype(bf16)
    u = -(w0 + jnp.dot(jnp.tanh(xl).astype(bf16), lr_ref[0], preferred_element_type=f32))
    softplus = jnp.maximum(u, 0.0) + jnp.log(1.0 + jnp.exp(-jnp.abs(u)))
    lw = -jnp.exp(-softplus - 0.5)
    a = jax.nn.sigmoid(a0 + jnp.dot(xl_b, lr_ref[1], preferred_element_type=f32))
    if first_layer:
        v = xv
        vfo_ref[...] = xv
    else:
        v = xv + (vf_ref[...] - xv) * jax.nn.sigmoid(vres0 + jnp.dot(xl_b, lr_ref[2], preferred_element_type=f32))
    gate = jnp.dot(jax.nn.sigmoid(xl).astype(bf16), lr_ref[3], preferred_element_type=f32)
    kk = xk * k_k
    k2 = xk * (1.0 + (a - 1.0) * k_a)
    cum = _mm_split(tri_ref[...], lw)

    pairs = range(N_PAIRS)
    cols = [slice(pair * LANES, (pair + 1) * LANES) for pair in pairs]
    kk_p = [kk[:, cols[i]] for i in pairs]
    kk_p = [kk_p[i] * lax.rsqrt(jnp.maximum(_seg_sum(kk_p[i] * kk_p[i], p_ref), 1e-24)) for i in pairs]
    r_p, k_p, v_p = [xr[:, cols[i]] for i in pairs], [k2[:, cols[i]] for i in pairs], [v[:, cols[i]] for i in pairs]
    y, st_new = _chunk_pair(r_p, [cum[:, cols[i]] for i in pairs], [lw[:, cols[i]] for i in pairs],
                            [-kk_p[i] for i in pairs], [kk_p[i] * a[:, cols[i]] for i in pairs], k_p, v_p,
                            [st_ref[i] for i in pairs])
    for i in pairs:
        st_ref[i] = st_new[i]
    mean = [_seg_sum(y[i], p_ref) * (1.0 / HEAD_DIM) for i in pairs]
    d = [y[i] - mean[i] for i in pairs]
    var = [_seg_sum(d[i] * d[i], p_ref) * (1.0 / HEAD_DIM) for i in pairs]
    for i in pairs:
        yn = d[i] * lax.rsqrt(var[i] + GN_EPS) * lnx_w[:, cols[i]] + lnx_b[:, cols[i]]
        bonus = _seg_sum(r_p[i] * k_p[i] * r_k[:, cols[i]], p_ref) * v_p[i]
        o_ref[:, cols[i]] = (yn + bonus) * gate[:, cols[i]]


def rwkv_prompt(z, v_first, lp, consts, batch, seq):
    m = batch * seq
    c = RW_CHUNK
    n_chunks = seq // c
    first_layer = v_first is None
    rows = lambda j: (lambda b, i: (b * n_chunks + i, j))
    const2 = lambda b, i: (0, 0)
    tok = pl.BlockSpec((c, D_RWKV), rows(0))
    tok_shape = jax.ShapeDtypeStruct((m, D_RWKV), f32)
    out = pl.pallas_call(
        functools.partial(_rwkv_prompt_kernel, first_layer=first_layer),
        grid=(batch, n_chunks),
        in_specs=[pl.BlockSpec((c, D_RWKV), rows(Z_R // D_RWKV)),
                  pl.BlockSpec((c, D_RWKV), rows(Z_K // D_RWKV)),
                  pl.BlockSpec((c, D_RWKV), rows(Z_V // D_RWKV)),
                  pl.BlockSpec((c, LANES), rows(Z_LR // LANES)),
                  pl.BlockSpec((None, 1, SHIFT_COLS), lambda b, i: (b, 0, 0)),
                  pl.BlockSpec((1, SHIFT_COLS), const2),
                  pl.BlockSpec((8, D_RWKV), const2),
                  pl.BlockSpec((4, LANES, D_RWKV), lambda b, i: (0, 0, 0)),
                  pl.BlockSpec((LANES, LANES), const2),
                  pl.BlockSpec((c, c), const2)] + ([] if first_layer else [tok]),
        out_specs=[tok, pl.BlockSpec((None, N_PAIRS, LANES, LANES), lambda b, i: (b, 0, 0, 0))]
                  + ([tok] if first_layer else []),
        out_shape=[tok_shape, jax.ShapeDtypeStruct((batch, N_PAIRS, LANES, LANES), f32)]
                  + ([tok_shape] if first_layer else []),
        scratch_shapes=[pltpu.VMEM((8, SHIFT_COLS), f32)],
        compiler_params=pltpu.CompilerParams(
            dimension_semantics=("parallel", "arbitrary"), vmem_limit_bytes=VMEM_LIMIT),
        name="rwkv_prompt",
    )(z, z, z, z, consts['shift_zero'], lp['mu_row'], lp['rwkv_vecs'], lp['rwkv_lowrank'],
      consts['seg_ones'], consts['tri'], *([] if first_layer else [v_first]))
    return out[0], out[1], (out[2] if first_layer else v_first)


def state_from_pair_tiles(st):
    b = st.shape[0]
    t = st.reshape(b, N_PAIRS, 2, HEAD_DIM, 2, HEAD_DIM)
    diag = jnp.stack([t[:, :, 0, :, 0, :], t[:, :, 1, :, 1, :]], axis=2)
    return jnp.swapaxes(diag, -1, -2).reshape(b, N_HEADS_RWKV, HEAD_DIM, HEAD_DIM)


def rwkv_params(lp_raw, layer):
    zrow = jnp.zeros((D_RWKV,), f32)
    vres0 = lp_raw['vres0'][layer - 1] if layer > 0 else zrow
    vecs = jnp.stack([lp_raw['w0'][layer], lp_raw['a0'][layer], vres0, lp_raw['k_k'][layer], lp_raw['k_a'][layer],
                      lp_raw['r_k'][layer].reshape(D_RWKV), lp_raw['lnx_w'][layer], lp_raw['lnx_b'][layer]])
    def rows(w, lo):
        return jnp.zeros((LANES, D_RWKV), f32).at[lo:lo + w.shape[0]].set(w)
    vres2 = lp_raw['vres2'][layer - 1] if layer > 0 else jnp.zeros((RANK_A, D_RWKV), f32)
    lowrank = jnp.stack([rows(lp_raw['w2'][layer], 0), rows(lp_raw['a2'][layer], RANK_W),
                         rows(vres2, RANK_W), rows(lp_raw['g2'][layer], RANK_W + RANK_A)]).astype(bf16)
    return vecs, lowrank


CHUNKS_PER_PAGE = PAGE_SIZE // STRIDE_CMP
PAGES_PER_STEP = 8
BLOCKS_PER_STEP = PAGES_PER_STEP * PAGE_SIZE // BLOCK_SLC
N_BLK_PAD = 256


POOL_PAGES_PER_STEP = 64


def _compress_pool_kernel(x_ref, pe_ref, w_ref, y_ref, xk_ref, xv_ref):
    xs_refs = (xk_ref, xv_ref)
    n_pages = x_ref.shape[0]
    n_chunks = n_pages * CHUNKS_PER_PAGE

    def to_token_major(p, carry):
        r0 = pl.multiple_of(p * PAGE_SIZE, PAGE_SIZE)
        page = x_ref[p]
        for c in range(2):
            tile = jnp.concatenate([page[c, g] for g in range(N_KV_HEADS)], axis=0)
            xs_refs[c][pl.ds(r0, PAGE_SIZE), :] = tile.T
        return carry

    lax.fori_loop(0, n_pages, to_token_major, 0)
    for c in range(2):
        acc = [jnp.zeros((n_chunks, D_KV), f32) for _ in range(CMP_R)]
        for s in range(STRIDE_CMP):
            xs = xs_refs[c][pl.ds(s, n_chunks, stride=STRIDE_CMP), :]
            rows = slice(s * KV_COLS + c * D_KV, s * KV_COLS + (c + 1) * D_KV)
            for i in range(CMP_R):
                acc[i] = acc[i] + jnp.dot((xs + pe_ref[i, :, rows]).astype(bf16),
                                          w_ref[i, rows, c * D_KV:(c + 1) * D_KV], preferred_element_type=f32)
        for i in range(CMP_R):
            y_ref[:, i * KV_COLS + c * D_KV:i * KV_COLS + (c + 1) * D_KV] = acc[i]


def compress_pool(cache_cmp, wb, peb):
    depth, n_pool = cache_cmp.shape[:2]
    rows = n_pool * CHUNKS_PER_PAGE
    pp = POOL_PAGES_PER_STEP
    tm = pp * CHUNKS_PER_PAGE
    assert n_pool % pp == 0
    return pl.pallas_call(
        _compress_pool_kernel,
        grid=(depth, n_pool // pp),
        in_specs=[pl.BlockSpec((None, pp, 2, N_KV_HEADS, HEAD_DIM, PAGE_SIZE), lambda l, i: (l, i, 0, 0, 0, 0)),
                  pl.BlockSpec((None, CMP_R, 1, CHUNK_COLS), lambda l, i: (l, 0, 0, 0)),
                  pl.BlockSpec((None, CMP_R, CHUNK_COLS, KV_COLS), lambda l, i: (l, 0, 0, 0))],
        out_specs=pl.BlockSpec((None, tm, 2 * KV_COLS), lambda l, i: (l, i, 0)),
        out_shape=jax.ShapeDtypeStruct((depth, rows, 2 * KV_COLS), f32),
        scratch_shapes=[pltpu.VMEM((pp * PAGE_SIZE, D_KV), f32), pltpu.VMEM((pp * PAGE_SIZE, D_KV), f32)],
        compiler_params=pltpu.CompilerParams(
            dimension_semantics=("parallel", "parallel"), vmem_limit_bytes=VMEM_LIMIT),
        name="compress_pool",
    )(cache_tiles(cache_cmp), peb, wb)


def _softmax_2d(s, mask):
    s = jnp.where(mask, s, NEG)
    m = jnp.max(s, axis=-1, keepdims=True)
    p = jnp.where(mask, jnp.exp(s - m), 0.0)
    l = jnp.sum(p, axis=-1, keepdims=True)
    return p * jnp.where(l > 0.0, 1.0 / l, 0.0)


def _pad_rows(x, rows):
    return jnp.concatenate([x, jnp.zeros((rows - x.shape[0], x.shape[1]), x.dtype)], axis=0)


def head_dim_major(tile_ref, which):
    return jnp.concatenate([tile_ref[which, g] for g in range(N_KV_HEADS)], axis=0).astype(bf16)


def cache_tiles(cache):
    nd = cache.ndim
    return jnp.transpose(cache, tuple(range(nd - 4)) + (nd - 3, nd - 2, nd - 1, nd - 4))


def _sample_attn_kernel(pt_ref, q_ref, gate_ref, y_ref, *rest, past, n_new):
    pages = rest[:PAGES_PER_STEP]
    (ksn_ref, kwn_ref, cw_ref, smap_ref, o_ref,
     oc_ref, ow_ref, sel_ref, imp_ref, m_ref, l_ref, acc_ref) = rest[PAGES_PER_STEP:]
    del pt_ref
    c = pl.program_id(1)
    n_steps = pl.num_programs(1)
    rows = N_HEADS_ATTN * n_new
    q = q_ref[...]
    qpos = past + lax.broadcasted_iota(jnp.int32, (rows, 1), 0) % n_new
    n_blk = -(-(past + n_new) // BLOCK_SLC)

    @pl.when(c == 0)
    def _():
        y = y_ref[...]
        n_chunks = y.shape[0]
        comp = y[:, :KV_COLS] + pltpu.roll(y[:, KV_COLS:], n_chunks - 1, 0)
        s = lax.dot_general(q, comp[:, :D_KV].astype(bf16), NT_DIMS, preferred_element_type=f32)
        c_end = lax.broadcasted_iota(jnp.int32, (1, n_chunks), 1) * STRIDE_CMP + (BLOCK_CMP - 1)
        valid = (c_end <= qpos) & (lax.broadcasted_iota(jnp.int32, (1, n_chunks), 1) < n_chunks - 1)
        p_c = _softmax_2d(s, valid)
        oc_ref[...] = jnp.dot(p_c.astype(bf16), comp[:, D_KV:].astype(bf16), preferred_element_type=f32)

        hq = HEADS_PER_KV * n_new
        p_sum = [sum(p_c[g * hq + hh * n_new:g * hq + (hh + 1) * n_new] for hh in range(HEADS_PER_KV))
                 for g in range(N_KV_HEADS)]
        p_sum = _pad_rows(jnp.concatenate(p_sum, axis=0), LANES).astype(bf16)
        imp = lax.dot_general(smap_ref[...], p_sum, NT_DIMS, preferred_element_type=f32)
        jj = lax.broadcasted_iota(jnp.int32, (N_BLK_PAD, LANES), 0)
        qp = past + lax.broadcasted_iota(jnp.int32, (N_BLK_PAD, LANES), 1) % n_new
        cur = qp // BLOCK_SLC
        forced = (jj == 0) | ((jj <= cur) & (jj > cur - N_LOCAL))
        imp = jnp.where(jj * BLOCK_SLC <= qp, jnp.where(forced, FORCED_SCORE, imp), NEG)
        imp_ref[...] = imp

        def rank_step(j, rank):
            row = imp_ref[pl.ds(j, 1), :]
            return rank + jnp.where((row > imp) | ((row == imp) & (jj > j)), 1.0, 0.0)

        rank = lax.fori_loop(0, n_blk, rank_step, jnp.zeros((N_BLK_PAD, LANES), f32))
        sel = jnp.where(rank < float(min(N_SELECT, n_blk)), 1.0, 0.0).T
        sel_ref[...] = jnp.concatenate(
            [sel[g * n_new:(g + 1) * n_new] for g in range(N_KV_HEADS) for _ in range(HEADS_PER_KV)], axis=0)

        w_buf = cw_ref.shape[-1]
        kt_w, vt_w = head_dim_major(cw_ref, 0), head_dim_major(cw_ref, 1)
        kwn = _pad_rows(kwn_ref[...], LANES)
        s1 = jnp.dot(q, kt_w, preferred_element_type=f32)
        s2 = lax.dot_general(q, kwn[:, :D_KV].astype(bf16), NT_DIMS, preferred_element_type=f32)
        wpos = jnp.concatenate([past - w_buf + lax.broadcasted_iota(jnp.int32, (1, w_buf), 1),
                                past + lax.broadcasted_iota(jnp.int32, (1, LANES), 1)], axis=1)
        dist = qpos - wpos
        p_w = _softmax_2d(jnp.concatenate([s1, s2], axis=1), (dist >= 0) & (dist < WINDOW)).astype(bf16)
        ow_ref[...] = (lax.dot_general(p_w[:, :w_buf], vt_w, NT_DIMS, preferred_element_type=f32)
                       + jnp.dot(p_w[:, w_buf:], kwn[:, D_KV:].astype(bf16), preferred_element_type=f32))

        m_ref[...] = jnp.full(m_ref.shape, NEG, f32)
        l_ref[...] = jnp.zeros(l_ref.shape, f32)
        acc_ref[...] = jnp.zeros(acc_ref.shape, f32)

    def online_update(s, msk, pv):
        s = jnp.where(msk, s, NEG)
        m_old = m_ref[...]
        m_new = jnp.maximum(m_old, jnp.max(s, axis=-1, keepdims=True))
        alpha = jnp.exp(m_old - m_new)
        p = jnp.where(msk, jnp.exp(s - m_new), 0.0)
        l_ref[...] = alpha * l_ref[...] + jnp.sum(p, axis=-1, keepdims=True)
        acc_ref[...] = alpha * acc_ref[...] + pv(p.astype(bf16))
        m_ref[...] = m_new

    kt = jnp.concatenate([head_dim_major(pg, 0) for pg in pages], axis=1)
    vt = jnp.concatenate([head_dim_major(pg, 1) for pg in pages], axis=1)
    n_keys = kt.shape[1]
    blk = lax.broadcasted_iota(jnp.int32, (N_BLK_PAD, n_keys), 0)
    key_blk = c * BLOCKS_PER_STEP + lax.broadcasted_iota(jnp.int32, (N_BLK_PAD, n_keys), 1) // BLOCK_SLC
    expand = jnp.where(blk == key_blk, 1.0, 0.0).astype(bf16)
    msk = jnp.dot(sel_ref[...].astype(bf16), expand, preferred_element_type=f32) > 0.5
    online_update(jnp.dot(q, kt, preferred_element_type=f32), msk,
                  lambda p: lax.dot_general(p, vt, NT_DIMS, preferred_element_type=f32))

    @pl.when(c == n_steps - 1)
    def _():
        ksn = _pad_rows(ksn_ref[...], LANES)
        kpos = past + lax.broadcasted_iota(jnp.int32, (1, LANES), 1)
        last_blk = past // BLOCK_SLC
        msk_new = (sel_ref[:, last_blk:last_blk + 1] > 0.5) & (kpos <= qpos)
        online_update(lax.dot_general(q, ksn[:, :D_KV].astype(bf16), NT_DIMS, preferred_element_type=f32), msk_new,
                      lambda p: jnp.dot(p, ksn[:, D_KV:].astype(bf16), preferred_element_type=f32))
        o_s = acc_ref[...] / l_ref[...]
        o_c, o_w = oc_ref[...], ow_ref[...]
        lane = lax.broadcasted_iota(jnp.int32, (n_new, LANES), 1)
        heads = []
        for h in range(N_HEADS_ATTN):
            r = slice(h * n_new, (h + 1) * n_new)
            gc = gate_ref[:, N_BRANCH * h:N_BRANCH * h + 1]
            gs = gate_ref[:, N_BRANCH * h + 1:N_BRANCH * h + 2]
            gw = gate_ref[:, N_BRANCH * h + 2:N_BRANCH * h + 3]
            o_h = gc * o_c[r] + gs * o_s[r] + gw * o_w[r]
            heads.append(o_h if (h // HEADS_PER_KV) == (h % 2) else pltpu.roll(o_h, HEAD_DIM, 1))
        for pair in range(N_HEADS_ATTN // 2):
            o_ref[:, pair * LANES:(pair + 1) * LANES] = jnp.where(lane < HEAD_DIM, heads[2 * pair], heads[2 * pair + 1])


def sample_tables(past, n_new):
    n_chunks = past // STRIDE_CMP
    start = np.arange(n_chunks) * STRIDE_CMP
    first, last = start // BLOCK_SLC, (start + BLOCK_CMP - 1) // BLOCK_SLC
    j = np.arange(N_BLK_PAD)
    smap = (first[None, :] <= j[:, None]) & (last[None, :] >= j[:, None]) & (np.arange(n_chunks)[None, :] < n_chunks - 1)
    return jnp.asarray(smap, bf16)


def sample_attn(q_rows, gates, y_req, cache_slc, layer, page_table, kv_s_new, kv_w_new, cache_win, smap):
    bd, n_pages = page_table.shape
    n_new = kv_s_new.shape[1]
    past = n_pages * PAGE_SIZE
    rows = N_HEADS_ATTN * n_new
    w_buf = cache_win.shape[-1]
    tile = (None, None, 2, N_KV_HEADS, HEAD_DIM)
    assert n_pages % PAGES_PER_STEP == 0 and -(-(past + n_new) // BLOCK_SLC) <= N_BLK_PAD and n_new <= LANES
    per_b = lambda b, c, pt: (b, 0, 0)
    page = lambda j: (lambda b, c, pt: (layer, pt[b, c * PAGES_PER_STEP + j], 0, 0, 0, 0))
    grid_spec = pltpu.PrefetchScalarGridSpec(
        num_scalar_prefetch=1,
        grid=(bd, n_pages // PAGES_PER_STEP),
        in_specs=[pl.BlockSpec((None, rows, LANES), per_b),
                  pl.BlockSpec((None, n_new, LANES), per_b),
                  pl.BlockSpec((None, past // STRIDE_CMP, 2 * KV_COLS), per_b)]
                 + [pl.BlockSpec(tile + (PAGE_SIZE,), page(j)) for j in range(PAGES_PER_STEP)]
                 + [pl.BlockSpec((None, n_new, KV_COLS), per_b),
                    pl.BlockSpec((None, n_new, KV_COLS), per_b),
                    pl.BlockSpec(tile + (w_buf,), lambda b, c, pt: (layer, b, 0, 0, 0, 0)),
                    pl.BlockSpec(smap.shape, lambda b, c, pt: (0, 0))],
        out_specs=pl.BlockSpec((None, n_new, D_ATTN), per_b),
        scratch_shapes=[pltpu.VMEM((rows, LANES), f32), pltpu.VMEM((rows, LANES), f32),
                        pltpu.VMEM((rows, N_BLK_PAD), f32), pltpu.VMEM((N_BLK_PAD, LANES), f32),
                        pltpu.VMEM((rows, 1), f32), pltpu.VMEM((rows, 1), f32), pltpu.VMEM((rows, LANES), f32)])
    return pl.pallas_call(
        functools.partial(_sample_attn_kernel, past=past, n_new=n_new),
        grid_spec=grid_spec,
        out_shape=jax.ShapeDtypeStruct((bd, n_new, D_ATTN), f32),
        compiler_params=pltpu.CompilerParams(
            dimension_semantics=("parallel", "arbitrary"), vmem_limit_bytes=VMEM_LIMIT),
        name="sample_attn",
    )(page_table, q_rows, gates, y_req, *([cache_slc] * PAGES_PER_STEP), kv_s_new, kv_w_new, cache_win, smap)


def _split(x, sizes):
    offs = [int(o) for o in np.cumsum(sizes)[:-1]]
    return jnp.split(x, offs, axis=-1)


def rwkv7_time_mix(zs, shift_prev, wkv0, v_first, lp):
    B, T = zs.shape[:2]
    zf = zs.astype(f32)
    prev = jnp.concatenate([shift_prev.astype(f32)[:, None], zf[:, :-1]], axis=1)
    xs = zf + (prev - zf) * lp['mu']
    r, k, v, xw, xa, xg = _split(xs, [D_RWKV] * 3 + [RANK_W, RANK_A, RANK_G])
    w_raw = -jax.nn.softplus(-(lp['w0'] + jnp.tanh(xw) @ lp['w2'])) - 0.5
    decay = jnp.exp(-jnp.exp(w_raw))
    a = jax.nn.sigmoid(lp['a0'] + xa @ lp['a2'])
    if lp['vres'] is None:
        v_first = v
    else:
        vres0, vres2 = lp['vres']
        v = v + (v_first - v) * jax.nn.sigmoid(vres0 + xa @ vres2)
    g = jax.nn.sigmoid(xg) @ lp['g2']
    heads = lambda t: t.reshape(B, T, N_HEADS_RWKV, HEAD_DIM)
    kk = heads(k * lp['k_k'])
    kk = kk * lax.rsqrt(jnp.maximum(jnp.sum(kk * kk, axis=-1, keepdims=True), 1e-24))
    k = k * (1.0 + (a - 1.0) * lp['k_a'])
    r_h, k_h, v_h, a_h, d_h = heads(r), heads(k), heads(v), heads(a), heads(decay)

    def step(S, inp):
        r_t, d_t, kk_t, a_t, k_t, v_t = inp
        sa = jnp.einsum('bhvk,bhk->bhv', S, -kk_t)
        S = S * d_t[:, :, None, :] + sa[..., None] * (kk_t * a_t)[:, :, None, :] + v_t[..., None] * k_t[:, :, None, :]
        return S, jnp.einsum('bhvk,bhk->bhv', S, r_t)

    tm = lambda t: jnp.moveaxis(t, 1, 0)
    S_T, ys = lax.scan(step, wkv0.astype(f32), (tm(r_h), tm(d_h), tm(kk), tm(a_h), tm(k_h), tm(v_h)))
    y = jnp.moveaxis(ys, 0, 1)
    mean = jnp.mean(y, axis=-1, keepdims=True)
    var = jnp.mean(jnp.square(y - mean), axis=-1, keepdims=True)
    y = (y - mean) * lax.rsqrt(var + GN_EPS) * lp['lnx_w'].reshape(N_HEADS_RWKV, HEAD_DIM) + lp['lnx_b'].reshape(N_HEADS_RWKV, HEAD_DIM)
    y = y + jnp.sum(r_h * k_h * lp['r_k'], axis=-1, keepdims=True) * v_h
    y = y.reshape(B, T, D_RWKV) * g
    return y, S_T, zs[:, -1], v_first


def sample_trunk_layer(x, layer, shift_prev, wkv0, v_first, lp, consts, caches):
    B, Tn = x.shape[:2]
    rows = B * Tn
    page_table = caches['page_table']
    x2 = x.reshape(rows, D_MODEL)
    z = norm_proj(x2, lp['norm1_g'], lp['w_in'])
    q_ext, kv_c, kv_s, kv_w, _, _, gates = attn_prep(
        z, consts['rope_s'], consts['seg_ones'], lp['q_g_lanes'], lp['k_g_lanes'], lp['gate_b_lanes'], 1, rows)
    q_rows = q_ext.reshape(N_HEADS_ATTN, B, Tn, LANES).transpose(1, 0, 2, 3).reshape(B, N_HEADS_ATTN * Tn, LANES)
    n_pool = caches['slc'].shape[1]
    y_req = caches['y_pool'][layer].reshape(n_pool, CHUNKS_PER_PAGE, 2 * KV_COLS)[page_table]
    y_req = y_req.reshape(B, page_table.shape[1] * CHUNKS_PER_PAGE, 2 * KV_COLS)
    new_rows = lambda kv: kv.reshape(B, Tn, KV_COLS)
    o_attn = sample_attn(q_rows, gates.reshape(B, Tn, LANES), y_req, caches['slc'], layer, page_table,
                         new_rows(kv_s), new_rows(kv_w), caches['win'], consts['smap_s'])
    tok = lambda kv: kv.reshape(B, Tn, 2, N_KV_HEADS, HEAD_DIM)
    win = jnp.concatenate([caches['win_raw'][layer][:, Tn:], tok(kv_w)], axis=1)
    attn_state = (tok(kv_c), tok(kv_s), win)
    z_rwkv = split_z(z)[2].reshape(B, Tn, SHIFT_COLS)
    o_rwkv, wkv_t, shift_t, v_first = rwkv7_time_mix(z_rwkv, shift_prev, wkv0, v_first, lp)
    h = out_proj(x2, o_attn.reshape(rows, D_ATTN), o_rwkv.reshape(rows, D_RWKV), lp['w_out'])
    y = ffn(h, lp['norm2_g'], lp['w_up'], lp['w_down']).reshape(B, Tn, D_MODEL)
    return y, attn_state, wkv_t, shift_t, v_first


def prompt_trunk_layer(x, v_first, lp, consts):
    B, T = x.shape[:2]
    x2 = x.reshape(B * T, D_MODEL)
    z = norm_proj(x2, lp['norm1_g'], lp['w_in'])
    q_ext, kv_c, kv_s, kv_w, kb, vvb, gates = attn_prep(
        z, consts['rope'], consts['seg_ones'], lp['q_g_lanes'], lp['k_g_lanes'], lp['gate_b_lanes'], B, T)
    kc, vvc = compress(kv_c, lp['cmp_wb'], lp['cmp_peb'], B)
    o_attn = nsa_prompt(q_ext, gates, kc, vvc, kb, vvb, consts['nsa'], B, T)
    pages = lambda kv: kv.reshape(B, T // PAGE_SIZE, PAGE_SIZE, 2, N_KV_HEADS, HEAD_DIM)
    attn_state = (pages(kv_c), pages(kv_s),
                  kv_w.reshape(B, T, 2, N_KV_HEADS, HEAD_DIM)[:, -min(WINDOW, T):])
    o_rwkv, st, v_first = rwkv_prompt(z, v_first, lp, consts, B, T)
    z_last = z.reshape(B, T, IN_COLS_PAD)[:, -1]
    shift_t = jnp.concatenate([z_last[:, Z_R:Z_KV], z_last[:, Z_LR:]], axis=-1)
    h = out_proj(x2, o_attn.reshape(B * T, D_ATTN), o_rwkv, lp['w_out'])
    y = ffn(h, lp['norm2_g'], lp['w_up'], lp['w_down']).reshape(B, T, D_MODEL)
    return y, attn_state, state_from_pair_tiles(st), shift_t, v_first


def kernel(x_prompt, x_sample, cache_cmp_kv, cache_slc_kv, cache_win_kv, state_wkv, state_shift, page_table,
           norm1_g, w_in, q_norm_g, k_norm_g, gate_b, cmp_pe, cmp_w, shift_mu, w0, w2, a0, a2, vres0, vres2,
           g2, k_k, k_a, r_k, lnx_w, lnx_b, w_out, norm2_g, w_up, w_down):
    B, T = x_prompt.shape[:2]
    Tn = x_sample.shape[1]
    past = page_table.shape[1] * PAGE_SIZE
    pos_p = jnp.arange(T, dtype=jnp.int32)
    pos_s = past + jnp.arange(Tn, dtype=jnp.int32)
    w_in_b = permute_w_in(w_in).astype(bf16)
    w_out_b, w_up_b, w_down_b = w_out.astype(bf16), w_up.astype(bf16), w_down.astype(bf16)
    seg = np.arange(LANES) // HEAD_DIM
    consts = {'rope': rope_lane_tables(pos_p), 'nsa': nsa_tables(T),
              'seg_ones': jnp.asarray(seg[:, None] == seg[None, :], bf16),
              'tri': jnp.asarray(np.tril(np.ones((RW_CHUNK, RW_CHUNK))), bf16),
              'shift_zero': jnp.zeros((B, 1, SHIFT_COLS), f32)}
    rwkv_raw = {'w0': w0, 'w2': w2, 'a0': a0, 'a2': a2, 'vres0': vres0, 'vres2': vres2, 'g2': g2, 'k_k': k_k,
                'k_a': k_a, 'r_k': r_k, 'lnx_w': lnx_w, 'lnx_b': lnx_b}
    Bd = x_sample.shape[0]
    n_pool, w_buf = cache_slc_kv.shape[1], cache_win_kv.shape[2]
    consts['rope_s'] = tuple(jnp.tile(t, (Bd, 1)) for t in rope_lane_tables(pos_s))
    consts['smap_s'] = sample_tables(past, Tn)
    cmp_wb, cmp_peb = jax.vmap(compress_weights)(cmp_pe, cmp_w)
    caches = {'page_table': page_table,
              'slc': cache_tiles(cache_slc_kv), 'win': cache_tiles(cache_win_kv), 'win_raw': cache_win_kv,
              'y_pool': compress_pool(cache_cmp_kv, cmp_wb, cmp_peb)}
    xp, xs = x_prompt, x_sample
    vf_p, vf_s = None, None
    p_cmp, p_slc, p_win, p_wkv, p_shift = [], [], [], [], []
    s_cmp, s_slc, s_win, s_wkv, s_shift = [], [], [], [], []
    for l in range(DEPTH):
        lp = {'norm1_g': norm1_g[l], 'w_in': w_in_b[l], 'q_norm_g': q_norm_g[l], 'k_norm_g': k_norm_g[l],
              'gate_b': gate_b[l], 'cmp_pe': cmp_pe[l], 'cmp_w': cmp_w[l], 'mu': shift_mu[l], 'w0': w0[l],
              'w2': w2[l], 'a0': a0[l], 'a2': a2[l],
              'vres': None if l == 0 else (vres0[l - 1], vres2[l - 1]),
              'g2': g2[l], 'k_k': k_k[l], 'k_a': k_a[l], 'r_k': r_k[l], 'lnx_w': lnx_w[l], 'lnx_b': lnx_b[l],
              'w_out': w_out_b[l], 'norm2_g': norm2_g[l], 'w_up': w_up_b[l], 'w_down': w_down_b[l]}
        lp['cmp_wb'], lp['cmp_peb'] = cmp_wb[l], cmp_peb[l]
        lp['q_g_lanes'] = jnp.tile(q_norm_g[l], 2).reshape(1, LANES)
        lp['k_g_lanes'] = jnp.tile(k_norm_g[l], (1, 2))
        lp['gate_b_lanes'] = jnp.pad(gate_b[l], (0, LANES - N_BRANCH * N_HEADS_ATTN)).reshape(1, LANES)
        lp['mu_row'] = shift_mu[l].reshape(1, SHIFT_COLS)
        lp['rwkv_vecs'], lp['rwkv_lowrank'] = rwkv_params(rwkv_raw, l)
        xp, (c_kv, sl_kv, w_kv), wkv_t, sh_t, vf_p = prompt_trunk_layer(xp, vf_p, lp, consts)
        p_cmp.append(c_kv); p_slc.append(sl_kv); p_win.append(w_kv); p_wkv.append(wkv_t); p_shift.append(sh_t)
        xs, (c_kv, sl_kv, w_kv), wkv_t, sh_t, vf_s = sample_trunk_layer(
            xs, l, state_shift[l], state_wkv[l], vf_s, lp, consts, caches)
        s_cmp.append(c_kv); s_slc.append(sl_kv); s_win.append(w_kv); s_wkv.append(wkv_t); s_shift.append(sh_t)
    return (xp, xs, jnp.stack(p_cmp), jnp.stack(p_slc), jnp.stack(p_win), jnp.stack(p_wkv), jnp.stack(p_shift),
            jnp.stack(s_cmp), jnp.stack(s_slc), jnp.stack(s_win), jnp.stack(s_wkv), jnp.stack(s_shift))
```
